```python
import jax, jax.numpy as jnp
from jax import lax
import numpy as np

D_MODEL = 2048
BATCH = 4
SEQ = 2048
DEPTH = 1
DEC_BATCH = 128
DEC_SEQ = 8
PAST_LEN = 8192
PAGE_SIZE = 128

HEAD_DIM = 64
A_Q_HEADS = 16
A_KV_HEADS = 2
A_GROUP = A_Q_HEADS // A_KV_HEADS
A_WINDOW = 128
B_PATTERNS = ((128, 1), (512, 4), (2048, 16))
N_B_GROUPS = len(B_PATTERNS)
B_HEADS = 8
B_KEYS = 128
BLOCK = 128
D_FF = 4 * D_MODEL
ROPE_THETA = 10000.0
ALPHA = (2.0 * DEPTH) ** 0.25
BETA = (8.0 * DEPTH) ** -0.25
LN_EPS = 1e-5
NEG = -1e30

A_Q_W = A_Q_HEADS * HEAD_DIM
A_KV_W = A_KV_HEADS * HEAD_DIM
B_W = N_B_GROUPS * B_HEADS * HEAD_DIM
B_OUT_W = B_HEADS * HEAD_DIM
PROJ_W = A_Q_W + 2 * A_KV_W + 3 * B_W + 2 * D_MODEL

kernel_name = 'hybrid_swa_sink_dilated_decoder_step'


def _col_splits():
    sizes = (A_Q_W, A_KV_W, A_KV_W, B_W, B_W, B_W, D_MODEL)
    return [int(s) for s in np.cumsum(sizes)]


def layer_norm(x, g, b):
    xf = x.astype(jnp.float32)
    mu = xf.mean(-1, keepdims=True)
    var = jnp.square(xf - mu).mean(-1, keepdims=True)
    return ((xf - mu) * lax.rsqrt(var + LN_EPS) * g + b).astype(x.dtype)


def rope(x, pos):
    half = HEAD_DIM // 2
    inv = 1.0 / (ROPE_THETA ** (jnp.arange(half, dtype=jnp.float32) / half))
    ang = pos.astype(jnp.float32)[:, None] * inv[None, :]
    cos, sin = jnp.cos(ang)[:, None, :], jnp.sin(ang)[:, None, :]
    xf = x.astype(jnp.float32)
    x1, x2 = xf[..., :half], xf[..., half:]
    return jnp.concatenate([x1 * cos - x2 * sin, x2 * cos + x1 * sin], -1).astype(x.dtype)


def _softmax_stats(scores, sink):
    lse = jax.nn.logsumexp(scores, axis=-1, keepdims=True)
    if sink is not None:
        lse = jnp.logaddexp(lse, sink)
    return jnp.exp(scores - lse), lse[..., 0]


def banded_window_attn(q, k, v, dilation, sink=None):
    N, S, KV, G, HD = q.shape
    L = S // dilation
    M = N * dilation

    def to_residue(t):
        t = t.reshape((N, L, dilation) + t.shape[2:])
        t = jnp.moveaxis(t, 2, 1)
        return t.reshape((M, L) + t.shape[3:])

    nb = -(-L // BLOCK)
    pad = nb * BLOCK - L

    def blocks(t):
        t = jnp.pad(t, [(0, 0), (0, pad)] + [(0, 0)] * (t.ndim - 2))
        return t.reshape((M, nb, BLOCK) + t.shape[2:])

    qb = blocks(to_residue(q))
    kb = blocks(to_residue(k))
    vb = blocks(to_residue(v))
    kk = jnp.concatenate([jnp.concatenate([jnp.zeros_like(kb[:, :1]), kb[:, :-1]], 1), kb], 2)
    vv = jnp.concatenate([jnp.concatenate([jnp.zeros_like(vb[:, :1]), vb[:, :-1]], 1), vb], 2)
    scores = jnp.einsum('mbqkgd,mbskd->mbkgqs', qb.astype(jnp.float32), kk.astype(jnp.float32)) * (HEAD_DIM ** -0.5)
    qi = np.arange(BLOCK)[:, None]
    sj = np.arange(2 * BLOCK)[None, :]
    dist = BLOCK + qi - sj
    band = (dist >= 0) & (dist < B_KEYS)
    first = (np.arange(nb)[:, None, None] > 0) | (sj[None] >= BLOCK)
    mask = band[None] & first
    scores = jnp.where(mask[None, :, None, None], scores, NEG)
    sink_b = None if sink is None else sink.astype(jnp.float32)[None, None, :, :, None, None]
    p, lse = _softmax_stats(scores, sink_b)
    out = jnp.einsum('mbkgqs,mbskd->mbqkgd', p.astype(vv.dtype), vv)
    lse = jnp.moveaxis(lse, -1, 2)

    def from_residue(t):
        t = t.reshape((M, nb * BLOCK) + t.shape[3:])[:, :L]
        t = t.reshape((N, dilation, L) + t.shape[2:])
        t = jnp.moveaxis(t, 1, 2)
        return t.reshape((N, S) + t.shape[3:])

    return from_residue(out), from_residue(lse)


def gathered_window_attn(q, kv_buf, dilation, sink=None):
    N, T, KV, G, HD = q.shape
    Lb = kv_buf.shape[1] - T
    idx = Lb + np.arange(T)[:, None] - dilation * np.arange(B_KEYS)[None, :]
    valid = idx >= 0
    g = jnp.take(kv_buf, np.clip(idx, 0, None).reshape(-1), axis=1)
    g = g.reshape((N, T, B_KEYS) + kv_buf.shape[2:])
    kg, vg = g[:, :, :, 0], g[:, :, :, 1]
    scores = jnp.einsum('ntkgd,ntjkd->ntkgj', q.astype(jnp.float32), kg.astype(jnp.float32)) * (HEAD_DIM ** -0.5)
    scores = jnp.where(valid[None, :, None, None, :], scores, NEG)
    sink_b = None if sink is None else sink.astype(jnp.float32)[None, None, :, :, None]
    p, lse = _softmax_stats(scores, sink_b)
    out = jnp.einsum('ntkgj,ntjkd->ntkgd', p.astype(vg.dtype), vg)
    return out, lse


def project_inputs(x, pos, w_in):
    N, S, _ = x.shape
    h = jnp.einsum('bsd,de->bse', x, w_in)
    qa, ka, va, qb, kb, vb, ga, gb = jnp.split(h, _col_splits(), axis=-1)
    qa = rope(qa.reshape(N, S, A_Q_HEADS, HEAD_DIM), pos).reshape(N, S, A_KV_HEADS, A_GROUP, HEAD_DIM)
    ka = rope(ka.reshape(N, S, A_KV_HEADS, HEAD_DIM), pos)
    va = va.reshape(N, S, A_KV_HEADS, HEAD_DIM)
    nbh = N_B_GROUPS * B_HEADS
    qb = rope(qb.reshape(N, S, nbh, HEAD_DIM), pos).reshape(N, S, N_B_GROUPS, B_HEADS, HEAD_DIM)
    kb = rope(kb.reshape(N, S, nbh, HEAD_DIM), pos).reshape(N, S, N_B_GROUPS, B_HEADS, HEAD_DIM)
    vb = vb.reshape(N, S, N_B_GROUPS, B_HEADS, HEAD_DIM)
    return qa, ka, va, qb, kb, vb, jax.nn.sigmoid(ga), jax.nn.sigmoid(gb)


def combine_dilations(outs, lses):
    w = jax.nn.softmax(jnp.stack(lses, 0), axis=0)
    o = jnp.sum(w[..., None] * jnp.stack(outs, 0).astype(jnp.float32), axis=0).astype(outs[0].dtype)
    N, S = o.shape[:2]
    return o.reshape(N, S, B_OUT_W)


def finish_layer(x, o_a, o_b, ga, gb, w_branch_a, w_branch_b, w_out, ln1_g, ln1_b,
                 w_ff1, b_ff1, w_ff2, b_ff2, ln2_g, ln2_b):
    N, S, _ = x.shape
    br_a = jnp.einsum('bse,ed->bsd', o_a.reshape(N, S, A_Q_W), w_branch_a)
    br_b = jnp.einsum('bse,ed->bsd', o_b, w_branch_b)
    mix = jnp.einsum('bsd,de->bse', ga * br_a + gb * br_b, w_out)
    h = layer_norm(ALPHA * x + mix, ln1_g, ln1_b)
    u = jnp.square(jax.nn.relu(jnp.einsum('bsd,df->bsf', h, w_ff1) + b_ff1))
    f = jnp.einsum('bsf,fd->bsd', u, w_ff2) + b_ff2
    return layer_norm(ALPHA * h + f, ln2_g, ln2_b)


def setup_inputs(seed: int = 0) -> dict:
    key = jax.random.key(seed)
    ks = jax.random.split(key, 19)
    f32 = jnp.float32

    def nrm(k, shape, s):
        return jax.random.normal(k, shape, f32) * s

    la = min(A_WINDOW, PAST_LEN)
    lb = [min(w, PAST_LEN) for (w, _) in B_PATTERNS]
    v_scale = np.ones((PROJ_W,), np.float32)
    va0 = A_Q_W + A_KV_W
    v_scale[va0:va0 + A_KV_W] = BETA
    vb0 = A_Q_W + 2 * A_KV_W + 2 * B_W
    v_scale[vb0:vb0 + B_W] = BETA
    return {
        'x_prompt': nrm(ks[0], (BATCH, SEQ, D_MODEL), 1.0),
        'x_sample': nrm(ks[1], (DEC_BATCH, DEC_SEQ, D_MODEL), 1.0),
        'cache_a_kv': nrm(ks[2], (DEPTH, DEC_BATCH, la, 2, A_KV_HEADS, HEAD_DIM), 1.0),
        'cache_b1_kv': nrm(ks[3], (DEPTH, DEC_BATCH, lb[0], 2, B_HEADS, HEAD_DIM), 1.0),
        'cache_b2_kv': nrm(ks[4], (DEPTH, DEC_BATCH, lb[1], 2, B_HEADS, HEAD_DIM), 1.0),
        'cache_b3_kv': nrm(ks[5], (DEPTH, DEC_BATCH, lb[2], 2, B_HEADS, HEAD_DIM), 1.0),
        'w_in': nrm(ks[6], (DEPTH, D_MODEL, PROJ_W), D_MODEL ** -0.5) * jnp.asarray(v_scale),
        'a_sinks': nrm(ks[7], (DEPTH, A_KV_HEADS, A_GROUP), 0.5),
        'w_branch_a': nrm(ks[8], (DEPTH, A_Q_W, D_MODEL), BETA * A_Q_W ** -0.5),
        'w_branch_b': nrm(ks[9], (DEPTH, B_OUT_W, D_MODEL), BETA * B_OUT_W ** -0.5),
        'w_out': nrm(ks[10], (DEPTH, D_MODEL, D_MODEL), BETA * D_MODEL ** -0.5),
        'ln1_g': 1.0 + nrm(ks[11], (DEPTH, D_MODEL), 0.02),
        'ln1_b': nrm(ks[12], (DEPTH, D_MODEL), 0.02),
        'w_ff1': nrm(ks[13], (DEPTH, D_MODEL, D_FF), BETA * D_MODEL ** -0.5),
        'b_ff1': nrm(ks[14], (DEPTH, D_FF), 0.02),
        'w_ff2': nrm(ks[15], (DEPTH, D_FF, D_MODEL), BETA * D_FF ** -0.5),
        'b_ff2': nrm(ks[16], (DEPTH, D_MODEL), 0.02),
        'ln2_g': 1.0 + nrm(ks[17], (DEPTH, D_MODEL), 0.02),
        'ln2_b': nrm(ks[18], (DEPTH, D_MODEL), 0.02),
    }


def reference(x_prompt, x_sample, cache_a_kv, cache_b1_kv, cache_b2_kv, cache_b3_kv,
              w_in, a_sinks, w_branch_a, w_branch_b, w_out, ln1_g, ln1_b,
              w_ff1, b_ff1, w_ff2, b_ff2, ln2_g, ln2_b):
    S = x_prompt.shape[1]
    T = x_sample.shape[1]
    pos_p = jnp.arange(S)
    pos_s = PAST_LEN + jnp.arange(T)
    b_caches = (cache_b1_kv, cache_b2_kv, cache_b3_kv)
    hp, hs = x_prompt, x_sample
    a_p, a_s = [], []
    b_p = [[] for _ in B_PATTERNS]
    b_s = [[] for _ in B_PATTERNS]
    for l in range(DEPTH):
        rest = (w_branch_a[l], w_branch_b[l], w_out[l], ln1_g[l], ln1_b[l],
                w_ff1[l], b_ff1[l], w_ff2[l], b_ff2[l], ln2_g[l], ln2_b[l])
        qa, ka, va, qb, kb, vb, ga, gb = project_inputs(hp, pos_p, w_in[l])
        o_a, _ = banded_window_attn(qa, ka, va, 1, a_sinks[l])
        outs, lses = [], []
        for g, (win, dil) in enumerate(B_PATTERNS):
            o, lse = banded_window_attn(qb[:, :, g, :, None, :], kb[:, :, g], vb[:, :, g], dil)
            outs.append(o)
            lses.append(lse)
            b_p[g].append(jnp.stack([kb[:, :, g], vb[:, :, g]], 2)[:, S - min(win, S):])
        a_p.append(jnp.stack([ka, va], 2)[:, S - min(A_WINDOW, S):])
        o_b = combine_dilations(outs, lses)
        hp_next = finish_layer(hp, o_a, o_b, ga, gb, *rest)
        qa, ka, va, qb, kb, vb, ga, gb = project_inputs(hs, pos_s, w_in[l])
        buf_a = jnp.concatenate([cache_a_kv[l], jnp.stack([ka, va], 2)], 1)
        o_a, _ = gathered_window_attn(qa, buf_a, 1, a_sinks[l])
        a_s.append(buf_a[:, buf_a.shape[1] - min(A_WINDOW, PAST_LEN + T):])
        outs, lses = [], []
        for g, (win, dil) in enumerate(B_PATTERNS):
            buf = jnp.concatenate([b_caches[g][l], jnp.stack([kb[:, :, g], vb[:, :, g]], 2)], 1)
            o, lse = gathered_window_attn(qb[:, :, g, :, None, :], buf, dil)
            outs.append(o)
            lses.append(lse)
            b_s[g].append(buf[:, buf.shape[1] - min(win, PAST_LEN + T):])
        o_b = combine_dilations(outs, lses)
        hs = finish_layer(hs, o_a, o_b, ga, gb, *rest)
        hp = hp_next
    return (hp, hs,
            jnp.stack(a_p, 0), jnp.stack(a_s, 0),
            jnp.stack(b_p[0], 0), jnp.stack(b_s[0], 0),
            jnp.stack(b_p[1], 0), jnp.stack(b_s[1], 0),
            jnp.stack(b_p[2], 0), jnp.stack(b_s[2], 0))
```

```python
import functools

import numpy as np
import jax
import jax.numpy as jnp
from jax import lax
from jax.experimental import pallas as pl
from jax.experimental.pallas import tpu as pltpu

D_MODEL = 2048
HEAD_DIM = 64
HALF = HEAD_DIM // 2
A_Q_HEADS = 16
A_KV_HEADS = 2
A_GROUP = A_Q_HEADS // A_KV_HEADS
A_WINDOW = 128
B_PATTERNS = ((128, 1), (512, 4), (2048, 16))
N_B_GROUPS = len(B_PATTERNS)
B_HEADS = 8
B_KEYS = 128
BLOCK = 128
D_FF = 4 * D_MODEL
ROPE_THETA = 10000.0
ALPHA = 2.0 ** 0.25
LN_EPS = 1e-5
NEG = -1e30
PAST_LEN = 8192

A_Q_W = A_Q_HEADS * HEAD_DIM
A_KV_W = A_KV_HEADS * HEAD_DIM
B_W = N_B_GROUPS * B_HEADS * HEAD_DIM
B_OUT_W = B_HEADS * HEAD_DIM
LANES = 128
VMEM_LIMIT = 60 * 1024 * 1024

F32 = jnp.float32
BF16 = jnp.bfloat16


def _cparams(sem):
    return pltpu.CompilerParams(dimension_semantics=sem, vmem_limit_bytes=VMEM_LIMIT)


PROJ_TN = 512
_T_QA, _T_KVA, _T_QB, _T_KVB, _T_G = 0, 2, 3, 6, 12
PROJ_TILES = 20


def _reorder_w_in(w_in):
    o = 0
    qa = w_in[:, o:o + A_Q_W]; o += A_Q_W
    ka = w_in[:, o:o + A_KV_W]; o += A_KV_W
    va = w_in[:, o:o + A_KV_W]; o += A_KV_W
    qb = w_in[:, o:o + B_W]; o += B_W
    kb = w_in[:, o:o + B_W]; o += B_W
    vb = w_in[:, o:o + B_W]; o += B_W
    gates = w_in[:, o:]
    pad = jnp.zeros((w_in.shape[0], PROJ_TN - 2 * A_KV_W), w_in.dtype)
    cols = [qa, ka, va, pad, qb]
    for g in range(N_B_GROUPS):
        cols += [kb[:, g * B_OUT_W:(g + 1) * B_OUT_W], vb[:, g * B_OUT_W:(g + 1) * B_OUT_W]]
    cols.append(gates)
    return jnp.concatenate(cols, axis=1).astype(BF16)


def _rope_tables(pos):
    inv = 1.0 / (ROPE_THETA ** (jnp.arange(HALF, dtype=F32) / HALF))
    ang = pos.astype(F32)[:, None] * inv[None, :]
    cos, sin = jnp.cos(ang), jnp.sin(ang)
    cosf = jnp.concatenate([cos, cos, cos, cos], axis=1)
    sinf = jnp.concatenate([-sin, sin, -sin, sin], axis=1)
    return cosf, sinf


def _rope_chunk(t, cos, sin, lo):
    sw = jnp.where(lo, pltpu.roll(t, LANES - HALF, 1), pltpu.roll(t, HALF, 1))
    return t * cos + sw * sin


def _proj_kernel(x_ref, w_ref, cos_ref, sin_ref,
                 qa_ref, kva_ref, qb_ref, kvb0_ref, kvb1_ref, kvb2_ref, g_ref, xb_ref):
    j = pl.program_id(1)

    @pl.when(j == 0)
    def _():
        xb_ref[...] = x_ref[...].astype(BF16)

    acc = jnp.dot(xb_ref[...], w_ref[...], preferred_element_type=F32)
    lane = lax.broadcasted_iota(jnp.int32, (1, LANES), 1)
    lo = (lane % HEAD_DIM) < HALF
    scale = HEAD_DIM ** -0.5

    def rope_store(ref, ncols, mul):
        cos, sin = cos_ref[...], sin_ref[...]
        for c in range(ncols // LANES):
            r = _rope_chunk(acc[:, c * LANES:(c + 1) * LANES], cos, sin, lo)
            if mul != 1.0:
                r = r * mul
            ref[:, c * LANES:(c + 1) * LANES] = r.astype(ref.dtype)

    @pl.when(j < _T_KVA)
    def _():
        rope_store(qa_ref, PROJ_TN, scale)

    @pl.when(j == _T_KVA)
    def _():
        rope_store(kva_ref, A_KV_W, 1.0)
        kva_ref[:, A_KV_W:2 * A_KV_W] = acc[:, A_KV_W:2 * A_KV_W]

    @pl.when((j >= _T_QB) & (j < _T_KVB))
    def _():
        rope_store(qb_ref, PROJ_TN, scale)

    for g, ref in enumerate((kvb0_ref, kvb1_ref, kvb2_ref)):
        @pl.when(j == _T_KVB + 2 * g)
        def _(ref=ref):
            rope_store(ref, PROJ_TN, 1.0)

        @pl.when(j == _T_KVB + 2 * g + 1)
        def _(ref=ref):
            ref[...] = acc

    @pl.when(j >= _T_G)
    def _():
        g_ref[...] = jax.nn.sigmoid(acc).astype(g_ref.dtype)


def _proj(x2d, w_cat, cosf, sinf, q_dtype, tm):
    m = x2d.shape[0]
    assert m % tm == 0
    grid = (m // tm, PROJ_TILES)

    def clamp(j, lo, n):
        return jnp.clip(j - lo, 0, n - 1)

    in_specs = [
        pl.BlockSpec((tm, D_MODEL), lambda i, j: (i, 0)),
        pl.BlockSpec((D_MODEL, PROJ_TN), lambda i, j: (0, j)),
        pl.BlockSpec((tm, LANES), lambda i, j: (i, 0)),
        pl.BlockSpec((tm, LANES), lambda i, j: (i, 0)),
    ]
    out_specs = [
        pl.BlockSpec((tm, PROJ_TN), lambda i, j: (i, clamp(j, _T_QA, 2))),
        pl.BlockSpec((tm, 2 * A_KV_W), lambda i, j: (i, 0)),
        pl.BlockSpec((tm, PROJ_TN), lambda i, j: (i, clamp(j, _T_QB, 3))),
        pl.BlockSpec((tm, PROJ_TN), lambda i, j: (i, clamp(j, _T_KVB, 2))),
        pl.BlockSpec((tm, PROJ_TN), lambda i, j: (i, clamp(j, _T_KVB + 2, 2))),
        pl.BlockSpec((tm, PROJ_TN), lambda i, j: (i, clamp(j, _T_KVB + 4, 2))),
        pl.BlockSpec((tm, PROJ_TN), lambda i, j: (i, clamp(j, _T_G, 8))),
    ]
    out_shape = [
        jax.ShapeDtypeStruct((m, A_Q_W), q_dtype),
        jax.ShapeDtypeStruct((m, 2 * A_KV_W), F32),
        jax.ShapeDtypeStruct((m, B_W), q_dtype),
        jax.ShapeDtypeStruct((m, 2 * B_OUT_W), F32),
        jax.ShapeDtypeStruct((m, 2 * B_OUT_W), F32),
        jax.ShapeDtypeStruct((m, 2 * B_OUT_W), F32),
        jax.ShapeDtypeStruct((m, 2 * D_MODEL), BF16),
    ]
    return pl.pallas_call(
        _proj_kernel, grid=grid, in_specs=in_specs, out_specs=out_specs, out_shape=out_shape,
        scratch_shapes=[pltpu.VMEM((tm, D_MODEL), BF16)],
        compiler_params=_cparams(("parallel", "arbitrary")), name="proj",
    )(x2d, w_cat, cosf, sinf)


def _band_mask(first_block):
    qi = lax.broadcasted_iota(jnp.int32, (BLOCK, 2 * BLOCK), 0)
    sj = lax.broadcasted_iota(jnp.int32, (BLOCK, 2 * BLOCK), 1)
    dist = BLOCK + qi - sj
    lo = jnp.where(first_block, BLOCK, 0)
    return (dist >= 0) & (dist < B_KEYS) & (sj >= lo)


def _attn_b_kernel(q_ref, kc_ref, kp_ref, vc_ref, vp_ref, o_ref, lse_ref):
    valid = _band_mask(pl.program_id(2) == 0)
    q = q_ref[...]
    k2 = jnp.concatenate([kp_ref[...], kc_ref[...]], axis=0).astype(BF16)
    v2 = jnp.concatenate([vp_ref[...], vc_ref[...]], axis=0).astype(BF16)
    head = lax.broadcasted_iota(jnp.int32, (1, B_OUT_W), 1) // HEAD_DIM
    o_acc = jnp.zeros((BLOCK, B_OUT_W), F32)
    lse_acc = jnp.zeros((BLOCK, B_OUT_W), F32)
    for h in range(B_HEADS):
        hm = head == h
        qm = jnp.where(hm, q, jnp.zeros_like(q))
        s = lax.dot_general(qm, k2, (((1,), (1,)), ((), ())), preferred_element_type=F32)
        s = jnp.where(valid, s, NEG)
        m = jnp.max(s, axis=-1, keepdims=True)
        p = jnp.exp(s - m)
        l = jnp.sum(p, axis=-1, keepdims=True)
        o = jnp.dot(p.astype(BF16), v2, preferred_element_type=F32)
        o_acc = jnp.where(hm, o * (1.0 / l), o_acc)
        lse_acc = jnp.where(hm, m + jnp.log(l), lse_acc)
    o_ref[...] = o_acc.astype(o_ref.dtype)
    lse_ref[...] = lse_acc


def _attn_b_prompt(qb, kvb, g, dil, n_seq, seq):
    t = qb.shape[0]
    l = seq // dil
    nb = l // BLOCK
    qv = qb.reshape(n_seq, l, dil * B_W)
    kvv = kvb.reshape(n_seq, l, dil * 2 * B_OUT_W)
    blk = (None, BLOCK, B_OUT_W)
    in_specs = [
        pl.BlockSpec(blk, lambda n, r, b: (n, b, r * N_B_GROUPS + g)),
        pl.BlockSpec(blk, lambda n, r, b: (n, b, 2 * r)),
        pl.BlockSpec(blk, lambda n, r, b: (n, jnp.maximum(b - 1, 0), 2 * r)),
        pl.BlockSpec(blk, lambda n, r, b: (n, b, 2 * r + 1)),
        pl.BlockSpec(blk, lambda n, r, b: (n, jnp.maximum(b - 1, 0), 2 * r + 1)),
    ]
    out_specs = [pl.BlockSpec(blk, lambda n, r, b: (n, b, r))] * 2
    out_shape = [jax.ShapeDtypeStruct((n_seq, l, dil * B_OUT_W), BF16),
                 jax.ShapeDtypeStruct((n_seq, l, dil * B_OUT_W), F32)]
    o, lse = pl.pallas_call(
        _attn_b_kernel, grid=(n_seq, dil, nb), in_specs=in_specs, out_specs=out_specs,
        out_shape=out_shape, compiler_params=_cparams(("parallel", "parallel", "arbitrary")),
        name=f"attn_b{g}",
    )(qv, kvv, kvv, kvv, kvv)
    return o.reshape(t, B_OUT_W), lse.reshape(t, B_OUT_W)


def _dup_heads(x128, lo):
    xr = pltpu.roll(x128, HEAD_DIM, 1)
    return jnp.where(lo, x128, xr), jnp.where(lo, xr, x128)


def _attn_a_kernel(sink_ref, q_ref, kvc_ref, kvp_ref, o_ref):
    valid = _band_mask(pl.program_id(1) == 0)
    kv2 = jnp.concatenate([kvp_ref[...], kvc_ref[...]], axis=0)
    lane = lax.broadcasted_iota(jnp.int32, (1, LANES), 1)
    lo = lane < HEAD_DIM
    kdup = [t.astype(BF16) for t in _dup_heads(kv2[:, 0:LANES], lo)]
    vdup = [t.astype(BF16) for t in _dup_heads(kv2[:, LANES:2 * LANES], lo)]
    for p in range(A_Q_HEADS // 2):
        kh = (2 * p) // A_GROUP
        q128 = q_ref[:, p * LANES:(p + 1) * LANES]
        halves = []
        for e in range(2):
            qm = jnp.where(lo if e == 0 else jnp.logical_not(lo), q128, jnp.zeros_like(q128))
            s = lax.dot_general(qm, kdup[kh], (((1,), (1,)), ((), ())), preferred_element_type=F32)
            s = jnp.where(valid, s, NEG)
            sink = sink_ref[2 * p + e]
            m = jnp.maximum(jnp.max(s, axis=-1, keepdims=True), sink)
            pe = jnp.exp(s - m)
            l = jnp.sum(pe, axis=-1, keepdims=True) + jnp.exp(sink - m)
            o = jnp.dot(pe.astype(BF16), vdup[kh], preferred_element_type=F32)
            halves.append(o * (1.0 / l))
        o_ref[:, p * LANES:(p + 1) * LANES] = jnp.where(lo, halves[0], halves[1]).astype(o_ref.dtype)


def _attn_a_prompt(sinks, qa, kva, n_seq, seq):
    t = qa.shape[0]
    nb = seq // BLOCK
    qv = qa.reshape(n_seq, seq, A_Q_W)
    kvv = kva.reshape(n_seq, seq, 2 * A_KV_W)
    in_specs = [
        pl.BlockSpec(memory_space=pltpu.SMEM),
        pl.BlockSpec((None, BLOCK, A_Q_W), lambda n, b: (n, b, 0)),
        pl.BlockSpec((None, BLOCK, 2 * A_KV_W), lambda n, b: (n, b, 0)),
        pl.BlockSpec((None, BLOCK, 2 * A_KV_W), lambda n, b: (n, jnp.maximum(b - 1, 0), 0)),
    ]
    o = pl.pallas_call(
        _attn_a_kernel, grid=(n_seq, nb), in_specs=in_specs,
        out_specs=pl.BlockSpec((None, BLOCK, A_Q_W), lambda n, b: (n, b, 0)),
        out_shape=jax.ShapeDtypeStruct((n_seq, seq, A_Q_W), BF16),
        compiler_params=_cparams(("parallel", "arbitrary")), name="attn_a",
    )(sinks, qv, kvv, kvv)
    return o.reshape(t, A_Q_W)


def _combine_kernel(o0_ref, o1_ref, o2_ref, l0_ref, l1_ref, l2_ref, ob_ref):
    l0, l1, l2 = l0_ref[...], l1_ref[...], l2_ref[...]
    m = jnp.maximum(jnp.maximum(l0, l1), l2)
    e0, e1, e2 = jnp.exp(l0 - m), jnp.exp(l1 - m), jnp.exp(l2 - m)
    num = e0 * o0_ref[...].astype(F32) + e1 * o1_ref[...].astype(F32) + e2 * o2_ref[...].astype(F32)
    ob_ref[...] = (num / (e0 + e1 + e2)).astype(ob_ref.dtype)


def _combine(outs, lses, tm):
    t = outs[0].shape[0]
    spec = pl.BlockSpec((tm, B_OUT_W), lambda i: (i, 0))
    return pl.pallas_call(
        _combine_kernel, grid=(t // tm,), in_specs=[spec] * 6, out_specs=spec,
        out_shape=jax.ShapeDtypeStruct((t, B_OUT_W), BF16),
        compiler_params=_cparams(("parallel",)), name="combine",
    )(*outs, *lses)


T_NEW = 8
KROWS = B_KEYS + 16


def _softmax_rows(s, valid, sink=None):
    s = jnp.where(valid, s, NEG)
    m = jnp.max(s, axis=-1, keepdims=True)
    if sink is not None:
        m = jnp.maximum(m, sink)
    p = jnp.exp(s - m)
    l = jnp.sum(p, axis=-1, keepdims=True)
    if sink is not None:
        l = l + jnp.exp(sink - m)
    return p, l, m


def _sample_mask(nq):
    slot = lax.broadcasted_iota(jnp.int32, (T_NEW * B_HEADS, KROWS), 0) // B_HEADS
    col = lax.broadcasted_iota(jnp.int32, (T_NEW * B_HEADS, KROWS), 1)
    in_plane = (col < B_KEYS) & (col >= slot + 1)
    new_slot = col - B_KEYS
    in_new = (col >= B_KEYS) & (new_slot <= slot) & (new_slot < nq)
    return in_plane | in_new


def _pick_rows(x, rows):
    if rows == list(range(T_NEW)):
        return x
    ridx = lax.broadcasted_iota(jnp.int32, (T_NEW, 1), 0)
    out = jnp.zeros_like(x)
    for slot, r in enumerate(rows):
        out = jnp.where(ridx == slot, jnp.broadcast_to(x[r:r + 1, :], x.shape), out)
    return out


def _place_rows(acc, val, rows):
    if rows == list(range(T_NEW)):
        return val
    ridx = lax.broadcasted_iota(jnp.int32, (T_NEW, 1), 0)
    for slot, r in enumerate(rows):
        acc = jnp.where(ridx == r, jnp.broadcast_to(val[slot:slot + 1, :], acc.shape), acc)
    return acc


def _attn_sample_kernel(sink_ref, qa_ref, kva_ref, qb_ref, kvb0_ref, kvb1_ref, kvb2_ref,
                        ca_ref, cb0_ref, cb1_ref, cb2_ref, oa_ref, ob_ref):
    pad8 = lambda w: jnp.zeros((T_NEW, w), F32)

    lane = lax.broadcasted_iota(jnp.int32, (1, LANES), 1)
    lo = lane < HEAD_DIM
    kva_new = kva_ref[...]
    kfull = jnp.concatenate([ca_ref[:, 0:LANES], kva_new[:, 0:LANES], pad8(LANES)], axis=0)
    vfull = jnp.concatenate([ca_ref[:, LANES:2 * LANES], kva_new[:, LANES:2 * LANES], pad8(LANES)], axis=0)
    kdup = [t.astype(BF16) for t in _dup_heads(kfull, lo)]
    vdup = [t.astype(BF16) for t in _dup_heads(vfull, lo)]
    tok = lax.broadcasted_iota(jnp.int32, (A_GROUP * T_NEW, KROWS), 0) % T_NEW
    col = lax.broadcasted_iota(jnp.int32, (A_GROUP * T_NEW, KROWS), 1)
    valid_a = ((col < A_WINDOW) & (col >= tok + 1)) | ((col >= A_WINDOW) & (col - A_WINDOW <= tok))
    hrow = lax.broadcasted_iota(jnp.int32, (A_GROUP * T_NEW, 1), 0) // T_NEW
    for kh in range(A_KV_HEADS):
        tiles = []
        for p in range(A_GROUP // 2):
            q128 = qa_ref[:, (kh * (A_GROUP // 2) + p) * LANES:(kh * (A_GROUP // 2) + p + 1) * LANES]
            tiles.append(jnp.where(lo, q128, 0.0))
            tiles.append(jnp.where(lo, 0.0, q128))
        qs = jnp.concatenate(tiles, axis=0).astype(BF16)
        sink = jnp.zeros((A_GROUP * T_NEW, 1), F32)
        for hh in range(A_GROUP):
            sink = jnp.where(hrow == hh, sink_ref[kh * A_GROUP + hh], sink)
        s = lax.dot_general(qs, kdup[kh], (((1,), (1,)), ((), ())), preferred_element_type=F32)
        p_, l, _ = _softmax_rows(s, valid_a, sink)
        o = jnp.dot(p_.astype(BF16), vdup[kh], preferred_element_type=F32) * (1.0 / l)
        for p in range(A_GROUP // 2):
            o0 = o[(2 * p) * T_NEW:(2 * p + 1) * T_NEW, :]
            o1 = o[(2 * p + 1) * T_NEW:(2 * p + 2) * T_NEW, :]
            c0 = (kh * (A_GROUP // 2) + p) * LANES
            oa_ref[:, c0:c0 + LANES] = jnp.where(lo, o0, o1)

    head = lax.broadcasted_iota(jnp.int32, (B_HEADS, B_OUT_W), 1) // HEAD_DIM
    hsel = head == lax.broadcasted_iota(jnp.int32, (B_HEADS, B_OUT_W), 0)
    hself = hsel.astype(F32)
    outs, lses = [], []
    for g, (cref, kvref) in enumerate(((cb0_ref, kvb0_ref), (cb1_ref, kvb1_ref), (cb2_ref, kvb2_ref))):
        dil = B_PATTERNS[g][1]
        n_res = min(dil, T_NEW)
        kv_new = kvref[...]
        q_new = qb_ref[:, g * B_OUT_W:(g + 1) * B_OUT_W]
        o_g = jnp.zeros((T_NEW, B_OUT_W), F32)
        lse_g = jnp.zeros((T_NEW, B_OUT_W), F32)
        for r in range(n_res):
            rows = list(range(r, T_NEW, dil))
            nq = len(rows)
            c0 = r * 2 * B_OUT_W
            knew = _pick_rows(kv_new[:, 0:B_OUT_W], rows)
            vnew = _pick_rows(kv_new[:, B_OUT_W:2 * B_OUT_W], rows)
            kfull = jnp.concatenate([cref[:, c0:c0 + B_OUT_W], knew, pad8(B_OUT_W)], axis=0).astype(BF16)
            vfull = jnp.concatenate([cref[:, c0 + B_OUT_W:c0 + 2 * B_OUT_W], vnew, pad8(B_OUT_W)], axis=0).astype(BF16)
            qsel = _pick_rows(q_new, rows)
            qbd = (qsel[:, None, :] * hself[None, :, :]).reshape(T_NEW * B_HEADS, B_OUT_W).astype(BF16)
            s = lax.dot_general(qbd, kfull, (((1,), (1,)), ((), ())), preferred_element_type=F32)
            p_, l, m = _softmax_rows(s, _sample_mask(nq))
            o = jnp.dot(p_.astype(BF16), vfull, preferred_element_type=F32) * (1.0 / l)
            o_slot = jnp.sum(o.reshape(T_NEW, B_HEADS, B_OUT_W) * hself[None], axis=1)
            lse = jnp.broadcast_to(m + jnp.log(l), (T_NEW * B_HEADS, B_OUT_W))
            lse_slot = jnp.sum(lse.reshape(T_NEW, B_HEADS, B_OUT_W) * hself[None], axis=1)
            o_g = _place_rows(o_g, o_slot, rows)
            lse_g = _place_rows(lse_g, lse_slot, rows)
        outs.append(o_g)
        lses.append(lse_g)
    m = jnp.maximum(jnp.maximum(lses[0], lses[1]), lses[2])
    es = [jnp.exp(x - m) for x in lses]
    num = es[0] * outs[0] + es[1] * outs[1] + es[2] * outs[2]
    ob_ref[...] = num / (es[0] + es[1] + es[2])


def _attn_sample(sinks, qa, kva, qb, kvbs, cache_a, cache_bs):
    nt = qa.shape[0]
    n = nt // T_NEW
    planes = []
    for g, (win, dil) in enumerate(B_PATTERNS):
        planes.append(cache_bs[g].reshape(n, win // dil, dil * 2 * B_OUT_W))
    tok = lambda w: pl.BlockSpec((T_NEW, w), lambda i: (i, 0))
    in_specs = [
        pl.BlockSpec(memory_space=pltpu.SMEM),
        tok(A_Q_W), tok(2 * A_KV_W), tok(B_W), tok(2 * B_OUT_W), tok(2 * B_OUT_W), tok(2 * B_OUT_W),
        pl.BlockSpec((None, A_WINDOW, 2 * A_KV_W), lambda i: (i, 0, 0)),
    ]
    for g, (win, dil) in enumerate(B_PATTERNS):
        w = min(dil, T_NEW) * 2 * B_OUT_W
        in_specs.append(pl.BlockSpec((None, B_KEYS, w), lambda i: (i, 0, 0)))
    return pl.pallas_call(
        _attn_sample_kernel, grid=(n,), in_specs=in_specs,
        out_specs=[tok(A_Q_W), tok(B_OUT_W)],
        out_shape=[jax.ShapeDtypeStruct((nt, A_Q_W), F32), jax.ShapeDtypeStruct((nt, B_OUT_W), F32)],
        compiler_params=_cparams(("parallel",)), name="attn_sample",
    )(sinks, qa, kva, qb, *kvbs, cache_a, *planes)


def _layer_norm(z, g, b):
    mu = jnp.mean(z, axis=-1, keepdims=True)
    zc = z - mu
    var = jnp.mean(zc * zc, axis=-1, keepdims=True)
    return zc * lax.rsqrt(var + LN_EPS) * g + b


def _mix_kernel(x_ref, oa_ref, ob_ref, g_ref, wa_ref, wb_ref, wo_ref, lng_ref, lnb_ref, h_ref):
    br_a = jnp.dot(oa_ref[...].astype(BF16), wa_ref[...], preferred_element_type=F32)
    br_b = jnp.dot(ob_ref[...].astype(BF16), wb_ref[...], preferred_element_type=F32)
    ga = g_ref[:, 0:D_MODEL].astype(F32)
    gb = g_ref[:, D_MODEL:2 * D_MODEL].astype(F32)
    merged = (ga * br_a + gb * br_b).astype(BF16)
    mix = jnp.dot(merged, wo_ref[...], preferred_element_type=F32)
    h_ref[...] = _layer_norm(ALPHA * x_ref[...] + mix, lng_ref[...], lnb_ref[...])


def _mix(x2d, oa, ob, gates, wa, wb, wo, ln_g, ln_b, tm):
    m = x2d.shape[0]
    row = lambda w: pl.BlockSpec((tm, w), lambda i: (i, 0))
    const = lambda a: pl.BlockSpec(a.shape, lambda i: (0, 0), pipeline_mode=pl.Buffered(1))
    return pl.pallas_call(
        _mix_kernel, grid=(m // tm,),
        in_specs=[row(D_MODEL), row(A_Q_W), row(B_OUT_W), row(2 * D_MODEL),
                  const(wa), const(wb), const(wo), const(ln_g), const(ln_b)],
        out_specs=row(D_MODEL), out_shape=jax.ShapeDtypeStruct((m, D_MODEL), F32),
        compiler_params=_cparams(("parallel",)), name="mix",
    )(x2d, oa, ob, gates, wa, wb, wo, ln_g, ln_b)


FFN_TF = 512


def _ffn_kernel(h_ref, w1_ref, b1_ref, w2_ref, b2_ref, lng_ref, lnb_ref, y_ref, hb_ref):
    f = pl.program_id(1)

    @pl.when(f == 0)
    def _():
        hb_ref[...] = h_ref[...].astype(BF16)
        y_ref[...] = jnp.zeros_like(y_ref)

    u = jnp.dot(hb_ref[...], w1_ref[...], preferred_element_type=F32) + b1_ref[...]
    u = jnp.square(jnp.maximum(u, 0.0)).astype(BF16)
    for c in range(D_MODEL // FFN_TF):
        cs = slice(c * FFN_TF, (c + 1) * FFN_TF)
        y_ref[:, cs] += jnp.dot(u, w2_ref[:, cs], preferred_element_type=F32)

    @pl.when(f == pl.num_programs(1) - 1)
    def _():
        z = ALPHA * h_ref[...] + y_ref[...] + b2_ref[...]
        y_ref[...] = _layer_norm(z, lng_ref[...], lnb_ref[...])


def _ffn(h, w1, b1, w2, b2, ln_g, ln_b, tm):
    m = h.shape[0]
    const = lambda a: pl.BlockSpec(a.shape, lambda i, f: (0, 0))
    return pl.pallas_call(
        _ffn_kernel, grid=(m // tm, D_FF // FFN_TF),
        in_specs=[pl.BlockSpec((tm, D_MODEL), lambda i, f: (i, 0)),
                  pl.BlockSpec((D_MODEL, FFN_TF), lambda i, f: (0, f)),
                  pl.BlockSpec((1, FFN_TF), lambda i, f: (0, f)),
                  pl.BlockSpec((FFN_TF, D_MODEL), lambda i, f: (f, 0)),
                  const(b2), const(ln_g), const(ln_b)],
        out_specs=pl.BlockSpec((tm, D_MODEL), lambda i, f: (i, 0)),
        out_shape=jax.ShapeDtypeStruct((m, D_MODEL), F32),
        scratch_shapes=[pltpu.VMEM((tm, D_MODEL), BF16)],
        compiler_params=_cparams(("parallel", "arbitrary")), name="ffn",
    )(h, w1, b1, w2, b2, ln_g, ln_b)


def _cache_roll_kernel(*refs):
    n_c = (len(refs) - 1) // 3
    caches, news, outs, sem = refs[:n_c], refs[n_c:2 * n_c], refs[2 * n_c:3 * n_c], refs[-1]
    copies = []
    for k in range(n_c):
        rows = caches[k].shape[1]
        copies.append(pltpu.make_async_copy(
            caches[k].at[:, pl.ds(T_NEW, rows - T_NEW), :], outs[k].at[:, pl.ds(0, rows - T_NEW), :],
            sem.at[2 * k]))
        copies.append(pltpu.make_async_copy(
            news[k], outs[k].at[:, pl.ds(rows - T_NEW, T_NEW), :], sem.at[2 * k + 1]))
    for c in copies:
        c.start()
    for c in copies:
        c.wait()


def _cache_roll(caches, news):
    n_c = len(caches)
    any_spec = pl.BlockSpec(memory_space=pl.ANY)
    return pl.pallas_call(
        _cache_roll_kernel, in_specs=[any_spec] * (2 * n_c), out_specs=[any_spec] * n_c,
        out_shape=[jax.ShapeDtypeStruct(c.shape, c.dtype) for c in caches],
        scratch_shapes=[pltpu.SemaphoreType.DMA((2 * n_c,))], name="cache_roll",
    )(*caches, *news)


def kernel(x_prompt, x_sample, cache_a_kv, cache_b1_kv, cache_b2_kv, cache_b3_kv, w_in, a_sinks,
           w_branch_a, w_branch_b, w_out, ln1_g, ln1_b, w_ff1, b_ff1, w_ff2, b_ff2, ln2_g, ln2_b):
    n_p, seq, _ = x_prompt.shape
    n_s, t_new, _ = x_sample.shape
    assert t_new == T_NEW and w_in.shape[0] == 1
    b_caches = (cache_b1_kv, cache_b2_kv, cache_b3_kv)

    w_cat = _reorder_w_in(w_in[0])
    wa, wb, wo = w_branch_a[0].astype(BF16), w_branch_b[0].astype(BF16), w_out[0].astype(BF16)
    w1, w2 = w_ff1[0].astype(BF16), w_ff2[0].astype(BF16)
    sinks = a_sinks[0].reshape(A_Q_HEADS).astype(F32)
    row = lambda v: v[0].reshape(1, -1)

    def finish(x2d, oa, ob, gates):
        h = _mix(x2d, oa, ob, gates, wa, wb, wo, row(ln1_g), row(ln1_b), tm=512)
        return _ffn(h, w1, row(b_ff1), w2, row(b_ff2), row(ln2_g), row(ln2_b), tm=512)

    xp = x_prompt.reshape(n_p * seq, D_MODEL)
    cos_p, sin_p = _rope_tables(jnp.tile(jnp.arange(seq), n_p))
    qa, kva, qb, kvb0, kvb1, kvb2, gates = _proj(xp, w_cat, cos_p, sin_p, BF16, tm=1024)
    kvbs = (kvb0, kvb1, kvb2)
    o_a = _attn_a_prompt(sinks, qa, kva, n_p, seq)
    outs, lses = [], []
    for g, (win, dil) in enumerate(B_PATTERNS):
        o, lse = _attn_b_prompt(qb, kvbs[g], g, dil, n_p, seq)
        outs.append(o)
        lses.append(lse)
    o_b = _combine(outs, lses, tm=1024)
    y_p = finish(xp, o_a, o_b, gates).reshape(n_p, seq, D_MODEL)
    a_kv_p = kva.reshape(n_p, seq, 2, A_KV_HEADS, HEAD_DIM)[None, :, seq - min(A_WINDOW, seq):]
    b_kv_p = [kvbs[g].reshape(n_p, seq, 2, B_HEADS, HEAD_DIM)[None, :, seq - min(win, seq):]
              for g, (win, dil) in enumerate(B_PATTERNS)]

    xs = x_sample.reshape(n_s * T_NEW, D_MODEL)
    cos_s, sin_s = _rope_tables(jnp.tile(PAST_LEN + jnp.arange(T_NEW), n_s))
    qa_s, kva_s, qb_s, kvb0_s, kvb1_s, kvb2_s, gates_s = _proj(xs, w_cat, cos_s, sin_s, F32, tm=n_s * T_NEW)
    kvbs_s = (kvb0_s, kvb1_s, kvb2_s)
    ca = cache_a_kv[0].reshape(n_s, -1, 2 * A_KV_W)
    cbs = [c[0].reshape(n_s, c.shape[2], 2 * B_OUT_W) for c in b_caches]
    o_a_s, o_b_s = _attn_sample(sinks, qa_s, kva_s, qb_s, kvbs_s, ca, cbs)
    y_s = finish(xs, o_a_s, o_b_s, gates_s).reshape(n_s, T_NEW, D_MODEL)
    news = [kva_s.reshape(n_s, T_NEW, 2 * A_KV_W)] + [k.reshape(n_s, T_NEW, 2 * B_OUT_W) for k in kvbs_s]
    rolled = _cache_roll([ca] + cbs, news)
    a_kv_s = rolled[0].reshape((1,) + cache_a_kv.shape[1:])
    b_kv_s = [rolled[1 + g].reshape((1,) + b_caches[g].shape[1:]) for g in range(N_B_GROUPS)]

    return (y_p, y_s, a_kv_p, a_kv_s,
            b_kv_p[0], b_kv_s[0], b_kv_p[1], b_kv_s[1], b_kv_p[2], b_kv_s[2])
```

```python
import functools

import numpy as np
import jax
import jax.numpy as jnp
from jax import lax
from jax.experimental import pallas as pl
from jax.experimental.pallas import tpu as pltpu

D_MODEL = 2048
HEAD_DIM = 64
HALF = HEAD_DIM // 2
A_Q_HEADS = 16
A_KV_HEADS = 2
A_GROUP = A_Q_HEADS // A_KV_HEADS
A_WINDOW = 128
B_PATTERNS = ((128, 1), (512, 4), (2048, 16))
N_B_GROUPS = len(B_PATTERNS)
B_HEADS = 8
B_KEYS = 128
BLOCK = 128
D_FF = 4 * D_MODEL
ROPE_THETA = 10000.0
ALPHA = 2.0 ** 0.25
LN_EPS = 1e-5
NEG = -1e30
PAST_LEN = 8192

A_Q_W = A_Q_HEADS * HEAD_DIM
A_KV_W = A_KV_HEADS * HEAD_DIM
B_W = N_B_GROUPS * B_HEADS * HEAD_DIM
B_OUT_W = B_HEADS * HEAD_DIM
LANES = 128
B_PAIRS = B_OUT_W // LANES
VMEM_LIMIT = 60 * 1024 * 1024

F32 = jnp.float32
BF16 = jnp.bfloat16
NT_DIMS = (((1,), (1,)), ((), ()))


def _cparams(sem):
    return pltpu.CompilerParams(dimension_semantics=sem, vmem_limit_bytes=VMEM_LIMIT)


PROJ_TN = 512
_T_QA, _T_KVA, _T_QB, _T_KVB, _T_G = 0, 2, 3, 6, 12
PROJ_TILES = 20


def _reorder_w_in(w_in):
    o = 0
    qa = w_in[:, o:o + A_Q_W]; o += A_Q_W
    ka = w_in[:, o:o + A_KV_W]; o += A_KV_W
    va = w_in[:, o:o + A_KV_W]; o += A_KV_W
    qb = w_in[:, o:o + B_W]; o += B_W
    kb = w_in[:, o:o + B_W]; o += B_W
    vb = w_in[:, o:o + B_W]; o += B_W
    gates = w_in[:, o:]
    pad = jnp.zeros((w_in.shape[0], PROJ_TN - 2 * A_KV_W), w_in.dtype)
    cols = [qa, ka, va, pad, qb]
    for g in range(N_B_GROUPS):
        cols += [kb[:, g * B_OUT_W:(g + 1) * B_OUT_W], vb[:, g * B_OUT_W:(g + 1) * B_OUT_W]]
    cols.append(gates)
    return jnp.concatenate(cols, axis=1).astype(BF16)


def _rope_tables(pos):
    inv = 1.0 / (ROPE_THETA ** (jnp.arange(HALF, dtype=F32) / HALF))
    ang = pos.astype(F32)[:, None] * inv[None, :]
    cos, sin = jnp.cos(ang), jnp.sin(ang)
    cosf = jnp.concatenate([cos, cos, cos, cos], axis=1)
    sinf = jnp.concatenate([-sin, sin, -sin, sin], axis=1)
    return cosf, sinf


def _rope_chunk(t, cos, sin, lo):
    sw = jnp.where(lo, pltpu.roll(t, LANES - HALF, 1), pltpu.roll(t, HALF, 1))
    return t * cos + sw * sin


def _proj_kernel(x_ref, w_ref, cos_ref, sin_ref,
                 qa_ref, kva_ref, kvta_ref, qb_ref,
                 kvb0_ref, kvt0_ref, kvb1_ref, kvt1_ref, kvb2_ref, kvt2_ref, g_ref, xb_ref):
    j = pl.program_id(1)

    @pl.when(j == 0)
    def _():
        xb_ref[...] = x_ref[...].astype(BF16)

    acc = jnp.dot(xb_ref[...], w_ref[...], preferred_element_type=F32)
    lane = lax.broadcasted_iota(jnp.int32, (1, LANES), 1)
    lo = (lane % HEAD_DIM) < HALF
    scale = HEAD_DIM ** -0.5

    def chunk(c, rope, mul=1.0):
        t = acc[:, c * LANES:(c + 1) * LANES]
        if rope:
            t = _rope_chunk(t, cos_ref[...], sin_ref[...], lo)
        return t if mul == 1.0 else t * mul

    @pl.when(j < _T_KVA)
    def _():
        for c in range(PROJ_TN // LANES):
            qa_ref[:, c * LANES:(c + 1) * LANES] = chunk(c, True, scale).astype(qa_ref.dtype)

    @pl.when(j == _T_KVA)
    def _():
        k, v = chunk(0, True), chunk(1, False)
        kva_ref[:, 0:LANES] = k
        kva_ref[:, LANES:2 * LANES] = v
        kvta_ref[0] = k.T
        kvta_ref[1] = v.T

    @pl.when((j >= _T_QB) & (j < _T_KVB))
    def _():
        for c in range(B_PAIRS):
            qb_ref[c] = chunk(c, True, scale)

    for g, (slab_ref, t_ref) in enumerate(((kvb0_ref, kvt0_ref), (kvb1_ref, kvt1_ref), (kvb2_ref, kvt2_ref))):
        for is_v in (0, 1):
            @pl.when(j == _T_KVB + 2 * g + is_v)
            def _(slab_ref=slab_ref, t_ref=t_ref, is_v=is_v):
                for c in range(B_PAIRS):
                    t = chunk(c, not is_v)
                    slab_ref[c] = t
                    t_ref[c * LANES:(c + 1) * LANES, :] = t.T

    @pl.when(j >= _T_G)
    def _():
        g_ref[...] = jax.nn.sigmoid(acc).astype(g_ref.dtype)


def _proj(x2d, w_cat, cosf, sinf, q_dtype, tm, n_seq, seq):
    m = x2d.shape[0]
    assert m == n_seq * seq and seq % tm == 0
    per_seq = seq // tm
    grid = (m // tm, PROJ_TILES)

    def clamp(j, lo, n):
        return jnp.clip(j - lo, 0, n - 1)

    in_specs = [
        pl.BlockSpec((tm, D_MODEL), lambda i, j: (i, 0)),
        pl.BlockSpec((D_MODEL, PROJ_TN), lambda i, j: (0, j)),
        pl.BlockSpec((tm, LANES), lambda i, j: (i, 0)),
        pl.BlockSpec((tm, LANES), lambda i, j: (i, 0)),
    ]
    out_specs = [
        pl.BlockSpec((tm, PROJ_TN), lambda i, j: (i, clamp(j, _T_QA, 2))),
        pl.BlockSpec((tm, 2 * A_KV_W), lambda i, j: (i, 0)),
        pl.BlockSpec((None, 2, A_KV_W, tm), lambda i, j: (i // per_seq, 0, 0, i % per_seq)),
        pl.BlockSpec((B_PAIRS, tm, LANES), lambda i, j: (clamp(j, _T_QB, N_B_GROUPS), i, 0)),
    ]
    out_shape = [
        jax.ShapeDtypeStruct((m, A_Q_W), q_dtype),
        jax.ShapeDtypeStruct((m, 2 * A_KV_W), F32),
        jax.ShapeDtypeStruct((n_seq, 2, A_KV_W, seq), F32),
        jax.ShapeDtypeStruct((N_B_GROUPS * B_PAIRS, m, LANES), F32),
    ]
    for g in range(N_B_GROUPS):
        t0 = _T_KVB + 2 * g
        out_specs.append(pl.BlockSpec((B_PAIRS, tm, LANES), lambda i, j, t0=t0: (clamp(j, t0, 2), i, 0)))
        out_specs.append(pl.BlockSpec((None, None, B_OUT_W, tm),
                                      lambda i, j, t0=t0: (i // per_seq, clamp(j, t0, 2), 0, i % per_seq)))
        out_shape.append(jax.ShapeDtypeStruct((2 * B_PAIRS, m, LANES), F32))
        out_shape.append(jax.ShapeDtypeStruct((n_seq, 2, B_OUT_W, seq), F32))
    out_specs.append(pl.BlockSpec((tm, PROJ_TN), lambda i, j: (i, clamp(j, _T_G, 8))))
    out_shape.append(jax.ShapeDtypeStruct((m, 2 * D_MODEL), BF16))
    return pl.pallas_call(
        _proj_kernel, grid=grid, in_specs=in_specs, out_specs=out_specs, out_shape=out_shape,
        scratch_shapes=[pltpu.VMEM((tm, D_MODEL), BF16)],
        compiler_params=_cparams(("parallel", "arbitrary")), name="proj",
    )(x2d, w_cat, cosf, sinf)


def _band_mask(first_block, rows):
    qi = lax.broadcasted_iota(jnp.int32, (rows, 2 * BLOCK), 0) % BLOCK
    sj = lax.broadcasted_iota(jnp.int32, (rows, 2 * BLOCK), 1)
    dist = BLOCK + qi - sj
    lo = jnp.where(first_block, BLOCK, 0)
    return (dist >= 0) & (dist < B_KEYS) & (sj >= lo)


def _softmax_rows(s, valid, sink=None):
    s = jnp.where(valid, s, NEG)
    m = jnp.max(s, axis=-1, keepdims=True)
    if sink is not None:
        m = jnp.maximum(m, sink)
    p = jnp.exp(s - m)
    l = jnp.sum(p, axis=-1, keepdims=True)
    if sink is not None:
        l = l + jnp.exp(sink - m)
    return p, l, m


def _stack_halves(q128, lo):
    return jnp.concatenate([jnp.where(lo, q128, 0.0), jnp.where(lo, 0.0, q128)], axis=0)


def _attn_b_kernel(*refs, dil, has_prev):
    if has_prev:
        q_ref, kvc_ref, kvp_ref, o_ref, lse_ref = refs
    else:
        q_ref, kvc_ref, o_ref, lse_ref = refs
    lane = lax.broadcasted_iota(jnp.int32, (1, LANES), 1)
    lo = lane < HEAD_DIM
    if has_prev:
        valid = _band_mask(pl.program_id(1) == 0, 2 * BLOCK)
    else:
        qi = lax.broadcasted_iota(jnp.int32, (2 * BLOCK, BLOCK), 0) % BLOCK
        sj = lax.broadcasted_iota(jnp.int32, (2 * BLOCK, BLOCK), 1)
        valid = qi >= sj

    def residue(r):
        rows = pl.ds(r, BLOCK, stride=dil) if dil > 1 else pl.ds(0, BLOCK)
        for p in range(B_PAIRS):
            qs = _stack_halves(q_ref[p, rows, :], lo).astype(BF16)
            k2, v2 = kvc_ref[p, rows, :], kvc_ref[B_PAIRS + p, rows, :]
            if has_prev:
                k2 = jnp.concatenate([kvp_ref[p, rows, :], k2], axis=0)
                v2 = jnp.concatenate([kvp_ref[B_PAIRS + p, rows, :], v2], axis=0)
            s = lax.dot_general(qs, k2.astype(BF16), NT_DIMS, preferred_element_type=F32)
            pr, l, m = _softmax_rows(s, valid)
            o = jnp.dot(pr.astype(BF16), v2.astype(BF16), preferred_element_type=F32) * (1.0 / l)
            lse = jnp.broadcast_to(m + jnp.log(l), o.shape)
            o_ref[p, rows, :] = jnp.where(lo, o[0:BLOCK], o[BLOCK:2 * BLOCK])
            lse_ref[p, rows, :] = jnp.where(lo, lse[0:BLOCK], lse[BLOCK:2 * BLOCK])

    if dil == 1:
        residue(0)
    else:
        def body(r, c):
            residue(r)
            return c
        lax.fori_loop(0, dil, body, 0)


def _attn_b_prompt(qb, kvb, g, dil, n_seq, seq):
    t = qb.shape[1]
    rows = BLOCK * dil
    nb = seq // rows
    qv = qb.reshape(qb.shape[0], n_seq, seq, LANES)
    kvv = kvb.reshape(kvb.shape[0], n_seq, seq, LANES)
    in_specs = [pl.BlockSpec((B_PAIRS, None, rows, LANES), lambda n, b: (g, n, b, 0)),
                pl.BlockSpec((2 * B_PAIRS, None, rows, LANES), lambda n, b: (0, n, b, 0))]
    args = [qv, kvv]
    if nb > 1:
        in_specs.append(pl.BlockSpec((2 * B_PAIRS, None, rows, LANES), lambda n, b: (0, n, jnp.maximum(b - 1, 0), 0)))
        args.append(kvv)
    out_spec = pl.BlockSpec((B_PAIRS, None, rows, LANES), lambda n, b: (0, n, b, 0))
    o, lse = pl.pallas_call(
        functools.partial(_attn_b_kernel, dil=dil, has_prev=nb > 1),
        grid=(n_seq, nb), in_specs=in_specs, out_specs=[out_spec, out_spec],
        out_shape=[jax.ShapeDtypeStruct((B_PAIRS, n_seq, seq, LANES), F32)] * 2,
        compiler_params=_cparams(("parallel", "arbitrary")), name=f"attn_b{g}",
    )(*args)
    return o.reshape(B_PAIRS, t, LANES), lse.reshape(B_PAIRS, t, LANES)


def _dup_heads(x128, lo):
    xr = pltpu.roll(x128, HEAD_DIM, 1)
    return jnp.where(lo, x128, xr), jnp.where(lo, xr, x128)


def _attn_a_kernel(sink_ref, q_ref, kvc_ref, kvp_ref, o_ref):
    valid = _band_mask(pl.program_id(1) == 0, BLOCK)
    kv2 = jnp.concatenate([kvp_ref[...], kvc_ref[...]], axis=0)
    lane = lax.broadcasted_iota(jnp.int32, (1, LANES), 1)
    lo = lane < HEAD_DIM
    kdup = [t.astype(BF16) for t in _dup_heads(kv2[:, 0:LANES], lo)]
    vdup = [t.astype(BF16) for t in _dup_heads(kv2[:, LANES:2 * LANES], lo)]
    for p in range(A_Q_HEADS // 2):
        kh = (2 * p) // A_GROUP
        q128 = q_ref[:, p * LANES:(p + 1) * LANES]
        halves = []
        for e in range(2):
            qm = jnp.where(lo if e == 0 else jnp.logical_not(lo), q128, jnp.zeros_like(q128))
            s = lax.dot_general(qm, kdup[kh], NT_DIMS, preferred_element_type=F32)
            pe, l, _ = _softmax_rows(s, valid, sink_ref[2 * p + e])
            o = jnp.dot(pe.astype(BF16), vdup[kh], preferred_element_type=F32)
            halves.append(o * (1.0 / l))
        o_ref[:, p * LANES:(p + 1) * LANES] = jnp.where(lo, halves[0], halves[1]).astype(o_ref.dtype)


def _attn_a_prompt(sinks, qa, kva, n_seq, seq):
    t = qa.shape[0]
    nb = seq // BLOCK
    qv = qa.reshape(n_seq, seq, A_Q_W)
    kvv = kva.reshape(n_seq, seq, 2 * A_KV_W)
    in_specs = [
        pl.BlockSpec(memory_space=pltpu.SMEM),
        pl.BlockSpec((None, BLOCK, A_Q_W), lambda n, b: (n, b, 0)),
        pl.BlockSpec((None, BLOCK, 2 * A_KV_W), lambda n, b: (n, b, 0)),
        pl.BlockSpec((None, BLOCK, 2 * A_KV_W), lambda n, b: (n, jnp.maximum(b - 1, 0), 0)),
    ]
    o = pl.pallas_call(
        _attn_a_kernel, grid=(n_seq, nb), in_specs=in_specs,
        out_specs=pl.BlockSpec((None, BLOCK, A_Q_W), lambda n, b: (n, b, 0)),
        out_shape=jax.ShapeDtypeStruct((n_seq, seq, A_Q_W), BF16),
        compiler_params=_cparams(("parallel", "arbitrary")), name="attn_a",
    )(sinks, qv, kvv, kvv)
    return o.reshape(t, A_Q_W)


def _combine3(outs, lses):
    m = jnp.maximum(jnp.maximum(lses[0], lses[1]), lses[2])
    es = [jnp.exp(x - m) for x in lses]
    num = es[0] * outs[0] + es[1] * outs[1] + es[2] * outs[2]
    return num / (es[0] + es[1] + es[2])


def _combine_kernel(o0_ref, o1_ref, o2_ref, l0_ref, l1_ref, l2_ref, ob_ref):
    for p in range(B_PAIRS):
        ob_ref[:, p * LANES:(p + 1) * LANES] = _combine3(
            [o0_ref[p], o1_ref[p], o2_ref[p]], [l0_ref[p], l1_ref[p], l2_ref[p]]).astype(ob_ref.dtype)


def _combine(outs, lses, tm):
    t = outs[0].shape[1]
    spec = pl.BlockSpec((B_PAIRS, tm, LANES), lambda i: (0, i, 0))
    return pl.pallas_call(
        _combine_kernel, grid=(t // tm,), in_specs=[spec] * 6,
        out_specs=pl.BlockSpec((tm, B_OUT_W), lambda i: (i, 0)),
        out_shape=jax.ShapeDtypeStruct((t, B_OUT_W), BF16),
        compiler_params=_cparams(("parallel",)), name="combine",
    )(*outs, *lses)


T_NEW = 8
SEQS_PER_LANE_BLOCK = LANES // T_NEW
S_PAIRS = 2
ROLL_ROWS = 64


def _roll_cache(c_ref, new_rolled, o_ref, kv, row0, nrows):
    lb = c_ref.shape[-1]
    lane = lax.broadcasted_iota(jnp.int32, (1, LANES), 1)
    tail = lane >= LANES - T_NEW
    for r0 in range(0, nrows, ROLL_ROWS):
        rs = slice(row0 + r0, row0 + r0 + ROLL_ROWS)
        t = pltpu.roll(c_ref[kv, rs, :], lb - T_NEW, 1)
        if lb > LANES:
            o_ref[kv, rs, 0:lb - LANES] = t[:, 0:lb - LANES]
        o_ref[kv, rs, lb - LANES:lb] = jnp.where(tail, new_rolled[r0:r0 + ROLL_ROWS, :], t[:, lb - LANES:lb])


def _sample_scores(qs, kt, nk, dil, lb):
    rows = qs.shape[0]
    s_c = jnp.dot(qs, kt.astype(BF16), preferred_element_type=F32)
    s_n = jnp.dot(qs, nk.astype(BF16), preferred_element_type=F32)
    tok = lax.broadcasted_iota(jnp.int32, (rows, 1), 0) % T_NEW
    pos = lax.broadcasted_iota(jnp.int32, (rows, lb), 1)
    d_c = lb + tok - pos
    valid_c = ((d_c & (dil - 1)) == 0) & (d_c <= (B_KEYS - 1) * dil)
    lane = lax.broadcasted_iota(jnp.int32, (rows, LANES), 1)
    d_n = tok - (lane - (LANES - T_NEW))
    valid_n = (lane >= LANES - T_NEW) & (d_n >= 0) & ((d_n & (dil - 1)) == 0)
    return jnp.where(valid_c, s_c, NEG), jnp.where(valid_n, s_n, NEG)


def _sample_attend(qs, kt, vt, nk, nv, dil, sink=None):
    lb = kt.shape[-1]
    s_c, s_n = _sample_scores(qs, kt, nk, dil, lb)
    m = jnp.maximum(jnp.max(s_c, axis=-1, keepdims=True), jnp.max(s_n, axis=-1, keepdims=True))
    if sink is not None:
        m = jnp.maximum(m, sink)
    p_c, p_n = jnp.exp(s_c - m), jnp.exp(s_n - m)
    l = jnp.sum(p_c, axis=-1, keepdims=True) + jnp.sum(p_n, axis=-1, keepdims=True)
    if sink is not None:
        l = l + jnp.exp(sink - m)
    o = lax.dot_general(p_c.astype(BF16), vt.astype(BF16), NT_DIMS, preferred_element_type=F32)
    o = o + lax.dot_general(p_n.astype(BF16), nv.astype(BF16), NT_DIMS, preferred_element_type=F32)
    return o * (1.0 / l), m + jnp.log(l)


def _sample_kernel(sink_ref, qa_ref, qb_ref, na_ref, n0_ref, n1_ref, n2_ref,
                   ca_ref, c0_ref, c1_ref, c2_ref,
                   oa_ref, ob_ref, ra_ref, r0_ref, r1_ref, r2_ref):
    n = pl.program_id(0)
    c = pl.program_id(1)
    shift = (LANES - T_NEW) - (n % SEQS_PER_LANE_BLOCK) * T_NEW
    lane = lax.broadcasted_iota(jnp.int32, (1, LANES), 1)
    lo = lane < HEAD_DIM

    for lp in range(S_PAIRS):
        rows = slice(lp * LANES, (lp + 1) * LANES)
        outs, lses = [], []
        for g, (c_ref, n_ref, r_ref) in enumerate(((c0_ref, n0_ref, r0_ref), (c1_ref, n1_ref, r1_ref),
                                                   (c2_ref, n2_ref, r2_ref))):
            dil = B_PATTERNS[g][1]
            nk = pltpu.roll(n_ref[0, rows, :], shift, 1)
            nv = pltpu.roll(n_ref[1, rows, :], shift, 1)
            qs = _stack_halves(qb_ref[g * B_PAIRS + c * S_PAIRS + lp], lo).astype(BF16)
            o, lse = _sample_attend(qs, c_ref[0, rows, :], c_ref[1, rows, :], nk, nv, dil)
            outs.append(jnp.where(lo, o[0:T_NEW], o[T_NEW:2 * T_NEW]))
            lse = jnp.broadcast_to(lse, o.shape)
            lses.append(jnp.where(lo, lse[0:T_NEW], lse[T_NEW:2 * T_NEW]))
            _roll_cache(c_ref, nk, r_ref, 0, lp * LANES, LANES)
            _roll_cache(c_ref, nv, r_ref, 1, lp * LANES, LANES)
        ob_ref[:, rows] = _combine3(outs, lses)

    @pl.when(c == 0)
    def _():
        nk = pltpu.roll(na_ref[0], shift, 1)
        nv = pltpu.roll(na_ref[1], shift, 1)
        _roll_cache(ca_ref, nk, ra_ref, 0, 0, A_KV_W)
        _roll_cache(ca_ref, nv, ra_ref, 1, 0, A_KV_W)
        hrow = lax.broadcasted_iota(jnp.int32, (A_GROUP * T_NEW, 1), 0) // T_NEW
        for kh in range(A_KV_HEADS):
            hs = slice(kh * HEAD_DIM, (kh + 1) * HEAD_DIM)
            dup = lambda x: jnp.concatenate([x[hs], x[hs]], axis=0)
            tiles = [_stack_halves(qa_ref[:, (kh * (A_GROUP // 2) + p) * LANES:(kh * (A_GROUP // 2) + p + 1) * LANES], lo)
                     for p in range(A_GROUP // 2)]
            qs = jnp.concatenate(tiles, axis=0).astype(BF16)
            sink = jnp.zeros((A_GROUP * T_NEW, 1), F32)
            for hh in range(A_GROUP):
                sink = jnp.where(hrow == hh, sink_ref[kh * A_GROUP + hh], sink)
            o, _ = _sample_attend(qs, dup(ca_ref[0]), dup(ca_ref[1]), dup(nk), dup(nv), 1, sink)
            for p in range(A_GROUP // 2):
                o0 = o[(2 * p) * T_NEW:(2 * p + 1) * T_NEW, :]
                o1 = o[(2 * p + 1) * T_NEW:(2 * p + 2) * T_NEW, :]
                c0 = (kh * (A_GROUP // 2) + p) * LANES
                oa_ref[:, c0:c0 + LANES] = jnp.where(lo, o0, o1)


def _sample_attn_roll(sinks, qa, qb, new_a, new_bs, cache_a, cache_bs):
    nt = qa.shape[0]
    n = nt // T_NEW
    steps = B_PAIRS // S_PAIRS
    rows = S_PAIRS * LANES
    in_specs = [
        pl.BlockSpec(memory_space=pltpu.SMEM),
        pl.BlockSpec((T_NEW, A_Q_W), lambda i, c: (i, 0)),
        pl.BlockSpec((N_B_GROUPS * B_PAIRS, T_NEW, LANES), lambda i, c: (0, i, 0)),
        pl.BlockSpec((2, A_KV_W, LANES), lambda i, c: (0, 0, i // SEQS_PER_LANE_BLOCK)),
    ]
    in_specs += [pl.BlockSpec((2, rows, LANES), lambda i, c: (0, c, i // SEQS_PER_LANE_BLOCK))] * N_B_GROUPS
    cache_specs = [pl.BlockSpec((None, 2, A_KV_W, cache_a.shape[-1]), lambda i, c: (i, 0, 0, 0))]
    cache_specs += [pl.BlockSpec((None, 2, rows, cb.shape[-1]), lambda i, c: (i, 0, c, 0)) for cb in cache_bs]
    out_specs = [pl.BlockSpec((T_NEW, A_Q_W), lambda i, c: (i, 0)),
                 pl.BlockSpec((T_NEW, rows), lambda i, c: (i, c))] + cache_specs
    out_shape = [jax.ShapeDtypeStruct((nt, A_Q_W), F32), jax.ShapeDtypeStruct((nt, B_OUT_W), F32),
                 jax.ShapeDtypeStruct(cache_a.shape, F32)] + [jax.ShapeDtypeStruct(cb.shape, F32) for cb in cache_bs]
    return pl.pallas_call(
        _sample_kernel, grid=(n, steps), in_specs=in_specs + cache_specs,
        out_specs=out_specs, out_shape=out_shape,
        compiler_params=_cparams(("parallel", "arbitrary")), name="sample_attn_roll",
    )(sinks, qa, qb, new_a, *new_bs, cache_a, *cache_bs)


def _layer_norm(z, g, b):
    mu = jnp.mean(z, axis=-1, keepdims=True)
    zc = z - mu
    var = jnp.mean(zc * zc, axis=-1, keepdims=True)
    return zc * lax.rsqrt(var + LN_EPS) * g + b


def _mix_kernel(x_ref, oa_ref, ob_ref, g_ref, wa_ref, wb_ref, wo_ref, lng_ref, lnb_ref, h_ref):
    br_a = jnp.dot(oa_ref[...].astype(BF16), wa_ref[...], preferred_element_type=F32)
    br_b = jnp.dot(ob_ref[...].astype(BF16), wb_ref[...], preferred_element_type=F32)
    ga = g_ref[:, 0:D_MODEL].astype(F32)
    gb = g_ref[:, D_MODEL:2 * D_MODEL].astype(F32)
    merged = (ga * br_a + gb * br_b).astype(BF16)
    mix = jnp.dot(merged, wo_ref[...], preferred_element_type=F32)
    h_ref[...] = _layer_norm(ALPHA * x_ref[...] + mix, lng_ref[...], lnb_ref[...])


def _mix(x2d, oa, ob, gates, wa, wb, wo, ln_g, ln_b, tm):
    m = x2d.shape[0]
    row = lambda w: pl.BlockSpec((tm, w), lambda i: (i, 0))
    const = lambda a: pl.BlockSpec(a.shape, lambda i: (0, 0), pipeline_mode=pl.Buffered(1))
    return pl.pallas_call(
        _mix_kernel, grid=(m // tm,),
        in_specs=[row(D_MODEL), row(A_Q_W), row(B_OUT_W), row(2 * D_MODEL),
                  const(wa), const(wb), const(wo), const(ln_g), const(ln_b)],
        out_specs=row(D_MODEL), out_shape=jax.ShapeDtypeStruct((m, D_MODEL), F32),
        compiler_params=_cparams(("parallel",)), name="mix",
    )(x2d, oa, ob, gates, wa, wb, wo, ln_g, ln_b)


FFN_TF = 512


def _ffn_kernel(h_ref, w1_ref, b1_ref, w2_ref, b2_ref, lng_ref, lnb_ref, y_ref, hb_ref):
    f = pl.program_id(1)

    @pl.when(f == 0)
    def _():
        hb_ref[...] = h_ref[...].astype(BF16)
        y_ref[...] = jnp.zeros_like(y_ref)

    u = jnp.dot(hb_ref[...], w1_ref[...], preferred_element_type=F32) + b1_ref[...]
    u = jnp.square(jnp.maximum(u, 0.0)).astype(BF16)
    for c in range(D_MODEL // FFN_TF):
        cs = slice(c * FFN_TF, (c + 1) * FFN_TF)
        y_ref[:, cs] += jnp.dot(u, w2_ref[:, cs], preferred_element_type=F32)

    @pl.when(f == pl.num_programs(1) - 1)
    def _():
        z = ALPHA * h_ref[...] + y_ref[...] + b2_ref[...]
        y_ref[...] = _layer_norm(z, lng_ref[...], lnb_ref[...])


def _ffn(h, w1, b1, w2, b2, ln_g, ln_b, tm):
    m = h.shape[0]
    const = lambda a: pl.BlockSpec(a.shape, lambda i, f: (0, 0))
    return pl.pallas_call(
        _ffn_kernel, grid=(m // tm, D_FF // FFN_TF),
        in_specs=[pl.BlockSpec((tm, D_MODEL), lambda i, f: (i, 0)),
                  pl.BlockSpec((D_MODEL, FFN_TF), lambda i, f: (0, f)),
                  pl.BlockSpec((1, FFN_TF), lambda i, f: (0, f)),
                  pl.BlockSpec((FFN_TF, D_MODEL), lambda i, f: (f, 0)),
                  const(b2), const(ln_g), const(ln_b)],
        out_specs=pl.BlockSpec((tm, D_MODEL), lambda i, f: (i, 0)),
        out_shape=jax.ShapeDtypeStruct((m, D_MODEL), F32),
        scratch_shapes=[pltpu.VMEM((tm, D_MODEL), BF16)],
        compiler_params=_cparams(("parallel", "arbitrary")), name="ffn",
    )(h, w1, b1, w2, b2, ln_g, ln_b)


def _to_cache_layout(kvt, heads):
    n, _, _, rows = kvt.shape
    return jnp.transpose(kvt.reshape(n, 2, heads, HEAD_DIM, rows), (0, 4, 1, 2, 3))[None]


def _from_cache_layout(cache):
    _, n, rows, _, heads, _ = cache.shape
    return jnp.transpose(cache[0], (0, 2, 3, 4, 1)).reshape(n, 2, heads * HEAD_DIM, rows)


def kernel(x_prompt, x_sample, cache_a_kv, cache_b1_kv, cache_b2_kv, cache_b3_kv, w_in, a_sinks,
           w_branch_a, w_branch_b, w_out, ln1_g, ln1_b, w_ff1, b_ff1, w_ff2, b_ff2, ln2_g, ln2_b):
    n_p, seq, _ = x_prompt.shape
    n_s, t_new, _ = x_sample.shape
    assert t_new == T_NEW and w_in.shape[0] == 1
    b_caches = (cache_b1_kv, cache_b2_kv, cache_b3_kv)

    w_cat = _reorder_w_in(w_in[0])
    wa, wb, wo = w_branch_a[0].astype(BF16), w_branch_b[0].astype(BF16), w_out[0].astype(BF16)
    w1, w2 = w_ff1[0].astype(BF16), w_ff2[0].astype(BF16)
    sinks = a_sinks[0].reshape(A_Q_HEADS).astype(F32)
    row = lambda v: v[0].reshape(1, -1)

    def finish(x2d, oa, ob, gates):
        h = _mix(x2d, oa, ob, gates, wa, wb, wo, row(ln1_g), row(ln1_b), tm=512)
        return _ffn(h, w1, row(b_ff1), w2, row(b_ff2), row(ln2_g), row(ln2_b), tm=512)

    xp = x_prompt.reshape(n_p * seq, D_MODEL)
    cos_p, sin_p = _rope_tables(jnp.tile(jnp.arange(seq), n_p))
    (qa, kva, kvt_a, qb, kvb0, kvt0, kvb1, kvt1, kvb2, kvt2, gates) = _proj(
        xp, w_cat, cos_p, sin_p, BF16, 512, n_p, seq)
    kvbs, kvts = (kvb0, kvb1, kvb2), (kvt0, kvt1, kvt2)
    o_a = _attn_a_prompt(sinks, qa, kva, n_p, seq)
    outs, lses = [], []
    for g, (win, dil) in enumerate(B_PATTERNS):
        o, lse = _attn_b_prompt(qb, kvbs[g], g, dil, n_p, seq)
        outs.append(o)
        lses.append(lse)
    o_b = _combine(outs, lses, tm=1024)
    y_p = finish(xp, o_a, o_b, gates).reshape(n_p, seq, D_MODEL)
    a_kv_p = _to_cache_layout(kvt_a[..., seq - min(A_WINDOW, seq):], A_KV_HEADS)
    b_kv_p = [_to_cache_layout(kvts[g][..., seq - min(win, seq):], B_HEADS)
              for g, (win, dil) in enumerate(B_PATTERNS)]

    nt = n_s * T_NEW
    xs = x_sample.reshape(nt, D_MODEL)
    cos_s, sin_s = _rope_tables(jnp.tile(PAST_LEN + jnp.arange(T_NEW), n_s))
    (qa_s, _, kvt_a_s, qb_s, _, kvt0_s, _, kvt1_s, _, kvt2_s, gates_s) = _proj(
        xs, w_cat, cos_s, sin_s, F32, 512, 1, nt)
    o_a_s, o_b_s, r_a, r_b1, r_b2, r_b3 = _sample_attn_roll(
        sinks, qa_s, qb_s, kvt_a_s[0], [kvt0_s[0], kvt1_s[0], kvt2_s[0]],
        _from_cache_layout(cache_a_kv), [_from_cache_layout(cb) for cb in b_caches])
    y_s = finish(xs, o_a_s, o_b_s, gates_s).reshape(n_s, T_NEW, D_MODEL)
    a_kv_s = _to_cache_layout(r_a, A_KV_HEADS)
    b_kv_s = [_to_cache_layout(r, B_HEADS) for r in (r_b1, r_b2, r_b3)]

    return (y_p, y_s, a_kv_p, a_kv_s,
            b_kv_p[0], b_kv_s[0], b_kv_p[1], b_kv_s[1], b_kv_p[2], b_kv_s[2])
```

```python
import functools

import numpy as np
import jax
import jax.numpy as jnp
from jax import lax
from jax.experimental import pallas as pl
from jax.experimental.pallas import tpu as pltpu

D_MODEL = 2048
HEAD_DIM = 64
HALF = HEAD_DIM // 2
A_Q_HEADS = 16
A_KV_HEADS = 2
A_GROUP = A_Q_HEADS // A_KV_HEADS
A_WINDOW = 128
B_PATTERNS = ((128, 1), (512, 4), (2048, 16))
N_B_GROUPS = len(B_PATTERNS)
B_HEADS = 8
B_KEYS = 128
BLOCK = 128
D_FF = 4 * D_MODEL
ROPE_THETA = 10000.0
ALPHA = 2.0 ** 0.25
LN_EPS = 1e-5
NEG = -1e30
PAST_LEN = 8192

A_Q_W = A_Q_HEADS * HEAD_DIM
A_KV_W = A_KV_HEADS * HEAD_DIM
B_W = N_B_GROUPS * B_HEADS * HEAD_DIM
B_OUT_W = B_HEADS * HEAD_DIM
LANES = 128
B_PAIRS = B_OUT_W // LANES
VMEM_LIMIT = 60 * 1024 * 1024

F32 = jnp.float32
BF16 = jnp.bfloat16
NT_DIMS = (((1,), (1,)), ((), ()))


def _cparams(sem):
    return pltpu.CompilerParams(dimension_semantics=sem, vmem_limit_bytes=VMEM_LIMIT)


PROJ_TN = 512
PROJ_SUB = 256
_T_QA, _T_KVA, _T_QB, _T_KVB, _T_G = 0, 2, 3, 6, 12
PROJ_TILES = 20


def _reorder_w_in(w_in):
    o = 0
    qa = w_in[:, o:o + A_Q_W]; o += A_Q_W
    ka = w_in[:, o:o + A_KV_W]; o += A_KV_W
    va = w_in[:, o:o + A_KV_W]; o += A_KV_W
    qb = w_in[:, o:o + B_W]; o += B_W
    kb = w_in[:, o:o + B_W]; o += B_W
    vb = w_in[:, o:o + B_W]; o += B_W
    gates = w_in[:, o:]
    pad = jnp.zeros((w_in.shape[0], PROJ_TN - 2 * A_KV_W), w_in.dtype)
    cols = [qa, ka, va, pad, qb]
    for g in range(N_B_GROUPS):
        cols += [kb[:, g * B_OUT_W:(g + 1) * B_OUT_W], vb[:, g * B_OUT_W:(g + 1) * B_OUT_W]]
    cols.append(gates)
    return jnp.concatenate(cols, axis=1).astype(BF16)


def _rope_tables(pos):
    inv = 1.0 / (ROPE_THETA ** (jnp.arange(HALF, dtype=F32) / HALF))
    ang = pos.astype(F32)[:, None] * inv[None, :]
    cos, sin = jnp.cos(ang), jnp.sin(ang)
    cosf = jnp.concatenate([cos, cos, cos, cos], axis=1)
    sinf = jnp.concatenate([-sin, sin, -sin, sin], axis=1)
    return cosf, sinf


def _rope_chunk(t, cos, sin, lo):
    sw = jnp.where(lo, pltpu.roll(t, LANES - HALF, 1), pltpu.roll(t, HALF, 1))
    return t * cos + sw * sin


def _proj_kernel(x_ref, w_ref, cos_ref, sin_ref,
                 qa_ref, kva_ref, kvta_ref, qb_ref,
                 kvb0_ref, kvt0_ref, kvb1_ref, kvt1_ref, kvb2_ref, kvt2_ref, g_ref, xb_ref):
    j = pl.program_id(1)

    @pl.when(j == 0)
    def _():
        xb_ref[...] = x_ref[...].astype(BF16)

    lane = lax.broadcasted_iota(jnp.int32, (1, LANES), 1)
    lo = (lane % HEAD_DIM) < HALF
    scale = HEAD_DIM ** -0.5

    def tile(n_halves=PROJ_TN // PROJ_SUB):
        xb = xb_ref[...]
        return [jnp.dot(xb, w_ref[:, h * PROJ_SUB:(h + 1) * PROJ_SUB], preferred_element_type=F32)
                for h in range(n_halves)]

    def chunk(acc, c, rope, mul=1.0):
        per = PROJ_SUB // LANES
        t = acc[c // per][:, (c % per) * LANES:(c % per + 1) * LANES]
        if rope:
            t = _rope_chunk(t, cos_ref[...], sin_ref[...], lo)
        return t if mul == 1.0 else t * mul

    @pl.when(j < _T_KVA)
    def _():
        acc = tile()
        for c in range(PROJ_TN // LANES):
            qa_ref[:, c * LANES:(c + 1) * LANES] = chunk(acc, c, True, scale).astype(qa_ref.dtype)

    @pl.when(j == _T_KVA)
    def _():
        acc = tile(1)
        k, v = chunk(acc, 0, True), chunk(acc, 1, False)
        kva_ref[:, 0:LANES] = k
        kva_ref[:, LANES:2 * LANES] = v
        kvta_ref[0] = k.T
        kvta_ref[1] = v.T

    @pl.when((j >= _T_QB) & (j < _T_KVB))
    def _():
        acc = tile()
        for c in range(B_PAIRS):
            qb_ref[c] = chunk(acc, c, True, scale)

    for g, (slab_ref, t_ref) in enumerate(((kvb0_ref, kvt0_ref), (kvb1_ref, kvt1_ref), (kvb2_ref, kvt2_ref))):
        for is_v in (0, 1):
            @pl.when(j == _T_KVB + 2 * g + is_v)
            def _(slab_ref=slab_ref, t_ref=t_ref, is_v=is_v):
                acc = tile()
                for c in range(B_PAIRS):
                    t = chunk(acc, c, not is_v)
                    slab_ref[c] = t
                    t_ref[c * LANES:(c + 1) * LANES, :] = t.T

    @pl.when(j >= _T_G)
    def _():
        for h, acc in enumerate(tile()):
            g_ref[:, h * PROJ_SUB:(h + 1) * PROJ_SUB] = jax.nn.sigmoid(acc).astype(g_ref.dtype)


def _proj(x2d, w_cat, cosf, sinf, q_dtype, tm, n_seq, seq):
    m = x2d.shape[0]
    assert m == n_seq * seq and seq % tm == 0
    per_seq = seq // tm
    grid = (m // tm, PROJ_TILES)

    def clamp(j, lo, n):
        return jnp.clip(j - lo, 0, n - 1)

    in_specs = [
        pl.BlockSpec((tm, D_MODEL), lambda i, j: (i, 0)),
        pl.BlockSpec((D_MODEL, PROJ_TN), lambda i, j: (0, j)),
        pl.BlockSpec((tm, LANES), lambda i, j: (i, 0)),
        pl.BlockSpec((tm, LANES), lambda i, j: (i, 0)),
    ]
    out_specs = [
        pl.BlockSpec((tm, PROJ_TN), lambda i, j: (i, clamp(j, _T_QA, 2))),
        pl.BlockSpec((tm, 2 * A_KV_W), lambda i, j: (i, 0)),
        pl.BlockSpec((None, 2, A_KV_W, tm), lambda i, j: (i // per_seq, 0, 0, i % per_seq)),
        pl.BlockSpec((B_PAIRS, tm, LANES), lambda i, j: (clamp(j, _T_QB, N_B_GROUPS), i, 0)),
    ]
    out_shape = [
        jax.ShapeDtypeStruct((m, A_Q_W), q_dtype),
        jax.ShapeDtypeStruct((m, 2 * A_KV_W), F32),
        jax.ShapeDtypeStruct((n_seq, 2, A_KV_W, seq), F32),
        jax.ShapeDtypeStruct((N_B_GROUPS * B_PAIRS, m, LANES), F32),
    ]
    for g in range(N_B_GROUPS):
        t0 = _T_KVB + 2 * g
        out_specs.append(pl.BlockSpec((B_PAIRS, tm, LANES), lambda i, j, t0=t0: (clamp(j, t0, 2), i, 0)))
        out_specs.append(pl.BlockSpec((None, None, B_OUT_W, tm),
                                      lambda i, j, t0=t0: (i // per_seq, clamp(j, t0, 2), 0, i % per_seq)))
        out_shape.append(jax.ShapeDtypeStruct((2 * B_PAIRS, m, LANES), F32))
        out_shape.append(jax.ShapeDtypeStruct((n_seq, 2, B_OUT_W, seq), F32))
    out_specs.append(pl.BlockSpec((tm, PROJ_TN), lambda i, j: (i, clamp(j, _T_G, 8))))
    out_shape.append(jax.ShapeDtypeStruct((m, 2 * D_MODEL), BF16))
    return pl.pallas_call(
        _proj_kernel, grid=grid, in_specs=in_specs, out_specs=out_specs, out_shape=out_shape,
        scratch_shapes=[pltpu.VMEM((tm, D_MODEL), BF16)],
        compiler_params=_cparams(("parallel", "arbitrary")), name="proj",
    )(x2d, w_cat, cosf, sinf)


def _band_mask(first_block, rows):
    qi = lax.broadcasted_iota(jnp.int32, (rows, 2 * BLOCK), 0) % BLOCK
    sj = lax.broadcasted_iota(jnp.int32, (rows, 2 * BLOCK), 1)
    dist = BLOCK + qi - sj
    lo = jnp.where(first_block, BLOCK, 0)
    return (dist >= 0) & (dist < B_KEYS) & (sj >= lo)


def _softmax_rows(s, valid, sink=None):
    s = jnp.where(valid, s, NEG)
    m = jnp.max(s, axis=-1, keepdims=True)
    if sink is not None:
        m = jnp.maximum(m, sink)
    p = jnp.exp(s - m)
    l = jnp.sum(p, axis=-1, keepdims=True)
    if sink is not None:
        l = l + jnp.exp(sink - m)
    return p, l, m


def _stack_halves(q128, lo):
    return jnp.concatenate([jnp.where(lo, q128, 0.0), jnp.where(lo, 0.0, q128)], axis=0)


def _attn_b_kernel(*refs, dil, has_prev):
    if has_prev:
        q_ref, kvc_ref, kvp_ref, o_ref, lse_ref = refs
    else:
        q_ref, kvc_ref, o_ref, lse_ref = refs
    lane = lax.broadcasted_iota(jnp.int32, (1, LANES), 1)
    lo = lane < HEAD_DIM
    if has_prev:
        valid = _band_mask(pl.program_id(1) == 0, 2 * BLOCK)
    else:
        qi = lax.broadcasted_iota(jnp.int32, (2 * BLOCK, BLOCK), 0) % BLOCK
        sj = lax.broadcasted_iota(jnp.int32, (2 * BLOCK, BLOCK), 1)
        valid = qi >= sj

    def residue(r):
        rows = pl.ds(r, BLOCK, stride=dil) if dil > 1 else pl.ds(0, BLOCK)
        for p in range(B_PAIRS):
            qs = _stack_halves(q_ref[p, rows, :], lo).astype(BF16)
            k2, v2 = kvc_ref[p, rows, :], kvc_ref[B_PAIRS + p, rows, :]
            if has_prev:
                k2 = jnp.concatenate([kvp_ref[p, rows, :], k2], axis=0)
                v2 = jnp.concatenate([kvp_ref[B_PAIRS + p, rows, :], v2], axis=0)
            s = lax.dot_general(qs, k2.astype(BF16), NT_DIMS, preferred_element_type=F32)
            pr, l, m = _softmax_rows(s, valid)
            o = jnp.dot(pr.astype(BF16), v2.astype(BF16), preferred_element_type=F32) * (1.0 / l)
            lse = jnp.broadcast_to(m + jnp.log(l), o.shape)
            o_ref[p, rows, :] = jnp.where(lo, o[0:BLOCK], o[BLOCK:2 * BLOCK])
            lse_ref[p, rows, :] = jnp.where(lo, lse[0:BLOCK], lse[BLOCK:2 * BLOCK])

    if dil == 1:
        residue(0)
    else:
        def body(r, c):
            residue(r)
            return c
        lax.fori_loop(0, dil, body, 0)


def _attn_b_prompt(qb, kvb, g, dil, n_seq, seq):
    t = qb.shape[1]
    rows = BLOCK * dil
    nb = seq // rows
    qv = qb.reshape(qb.shape[0], n_seq, seq, LANES)
    kvv = kvb.reshape(kvb.shape[0], n_seq, seq, LANES)
    in_specs = [pl.BlockSpec((B_PAIRS, None, rows, LANES), lambda n, b: (g, n, b, 0)),
                pl.BlockSpec((2 * B_PAIRS, None, rows, LANES), lambda n, b: (0, n, b, 0))]
    args = [qv, kvv]
    if nb > 1:
        in_specs.append(pl.BlockSpec((2 * B_PAIRS, None, rows, LANES), lambda n, b: (0, n, jnp.maximum(b - 1, 0), 0)))
        args.append(kvv)
    out_spec = pl.BlockSpec((B_PAIRS, None, rows, LANES), lambda n, b: (0, n, b, 0))
    o, lse = pl.pallas_call(
        functools.partial(_attn_b_kernel, dil=dil, has_prev=nb > 1),
        grid=(n_seq, nb), in_specs=in_specs, out_specs=[out_spec, out_spec],
        out_shape=[jax.ShapeDtypeStruct((B_PAIRS, n_seq, seq, LANES), F32)] * 2,
        compiler_params=_cparams(("parallel", "arbitrary")), name=f"attn_b{g}",
    )(*args)
    return o.reshape(B_PAIRS, t, LANES), lse.reshape(B_PAIRS, t, LANES)


def _dup_heads(x128, lo):
    xr = pltpu.roll(x128, HEAD_DIM, 1)
    return jnp.where(lo, x128, xr), jnp.where(lo, xr, x128)


def _attn_a_kernel(sink_ref, q_ref, kvc_ref, kvp_ref, o_ref):
    valid = _band_mask(pl.program_id(1) == 0, BLOCK)
    kv2 = jnp.concatenate([kvp_ref[...], kvc_ref[...]], axis=0)
    lane = lax.broadcasted_iota(jnp.int32, (1, LANES), 1)
    lo = lane < HEAD_DIM
    kdup = [t.astype(BF16) for t in _dup_heads(kv2[:, 0:LANES], lo)]
    vdup = [t.astype(BF16) for t in _dup_heads(kv2[:, LANES:2 * LANES], lo)]
    for p in range(A_Q_HEADS // 2):
        kh = (2 * p) // A_GROUP
        q128 = q_ref[:, p * LANES:(p + 1) * LANES]
        halves = []
        for e in range(2):
            qm = jnp.where(lo if e == 0 else jnp.logical_not(lo), q128, jnp.zeros_like(q128))
            s = lax.dot_general(qm, kdup[kh], NT_DIMS, preferred_element_type=F32)
            pe, l, _ = _softmax_rows(s, valid, sink_ref[2 * p + e])
            o = jnp.dot(pe.astype(BF16), vdup[kh], preferred_element_type=F32)
            halves.append(o * (1.0 / l))
        o_ref[:, p * LANES:(p + 1) * LANES] = jnp.where(lo, halves[0], halves[1]).astype(o_ref.dtype)


def _attn_a_prompt(sinks, qa, kva, n_seq, seq):
    t = qa.shape[0]
    nb = seq // BLOCK
    qv = qa.reshape(n_seq, seq, A_Q_W)
    kvv = kva.reshape(n_seq, seq, 2 * A_KV_W)
    in_specs = [
        pl.BlockSpec(memory_space=pltpu.SMEM),
        pl.BlockSpec((None, BLOCK, A_Q_W), lambda n, b: (n, b, 0)),
        pl.BlockSpec((None, BLOCK, 2 * A_KV_W), lambda n, b: (n, b, 0)),
        pl.BlockSpec((None, BLOCK, 2 * A_KV_W), lambda n, b: (n, jnp.maximum(b - 1, 0), 0)),
    ]
    o = pl.pallas_call(
        _attn_a_kernel, grid=(n_seq, nb), in_specs=in_specs,
        out_specs=pl.BlockSpec((None, BLOCK, A_Q_W), lambda n, b: (n, b, 0)),
        out_shape=jax.ShapeDtypeStruct((n_seq, seq, A_Q_W), BF16),
        compiler_params=_cparams(("parallel", "arbitrary")), name="attn_a",
    )(sinks, qv, kvv, kvv)
    return o.reshape(t, A_Q_W)


def _combine3(outs, lses):
    m = jnp.maximum(jnp.maximum(lses[0], lses[1]), lses[2])
    es = [jnp.exp(x - m) for x in lses]
    num = es[0] * outs[0] + es[1] * outs[1] + es[2] * outs[2]
    return num / (es[0] + es[1] + es[2])


def _combine_kernel(o0_ref, o1_ref, o2_ref, l0_ref, l1_ref, l2_ref, ob_ref):
    for p in range(B_PAIRS):
        ob_ref[:, p * LANES:(p + 1) * LANES] = _combine3(
            [o0_ref[p], o1_ref[p], o2_ref[p]], [l0_ref[p], l1_ref[p], l2_ref[p]]).astype(ob_ref.dtype)


def _combine(outs, lses, tm):
    t = outs[0].shape[1]
    spec = pl.BlockSpec((B_PAIRS, tm, LANES), lambda i: (0, i, 0))
    return pl.pallas_call(
        _combine_kernel, grid=(t // tm,), in_specs=[spec] * 6,
        out_specs=pl.BlockSpec((tm, B_OUT_W), lambda i: (i, 0)),
        out_shape=jax.ShapeDtypeStruct((t, B_OUT_W), BF16),
        compiler_params=_cparams(("parallel",)), name="combine",
    )(*outs, *lses)


T_NEW = 8
SEQS_PER_LANE_BLOCK = LANES // T_NEW
S_PAIRS = 2
SAMPLE_CHUNKS = B_PAIRS // S_PAIRS
ROLL_ROWS = 64


def _roll_cache(c_ref, new_rolled, o_ref, kv, row0, nrows):
    lb = c_ref.shape[-1]
    lane = lax.broadcasted_iota(jnp.int32, (1, LANES), 1)
    tail = lane >= LANES - T_NEW
    for r0 in range(0, nrows, ROLL_ROWS):
        rs = slice(row0 + r0, row0 + r0 + ROLL_ROWS)
        t = pltpu.roll(c_ref[kv, rs, :], lb - T_NEW, 1)
        if lb > LANES:
            o_ref[kv, rs, 0:lb - LANES] = t[:, 0:lb - LANES]
        o_ref[kv, rs, lb - LANES:lb] = jnp.where(tail, new_rolled[r0:r0 + ROLL_ROWS, :], t[:, lb - LANES:lb])


def _sample_scores(qs, kt, nk, dil, lb):
    rows = qs.shape[0]
    s_c = jnp.dot(qs, kt.astype(BF16), preferred_element_type=F32)
    s_n = jnp.dot(qs, nk.astype(BF16), preferred_element_type=F32)
    tok = lax.broadcasted_iota(jnp.int32, (rows, 1), 0) % T_NEW
    pos = lax.broadcasted_iota(jnp.int32, (rows, lb), 1)
    d_c = lb + tok - pos
    valid_c = ((d_c & (dil - 1)) == 0) & (d_c <= (B_KEYS - 1) * dil)
    lane = lax.broadcasted_iota(jnp.int32, (rows, LANES), 1)
    d_n = tok - (lane - (LANES - T_NEW))
    valid_n = (lane >= LANES - T_NEW) & (d_n >= 0) & ((d_n & (dil - 1)) == 0)
    return jnp.where(valid_c, s_c, NEG), jnp.where(valid_n, s_n, NEG)


def _sample_attend(qs, kt, vt, nk, nv, dil, sink=None):
    lb = kt.shape[-1]
    s_c, s_n = _sample_scores(qs, kt, nk, dil, lb)
    m = jnp.maximum(jnp.max(s_c, axis=-1, keepdims=True), jnp.max(s_n, axis=-1, keepdims=True))
    if sink is not None:
        m = jnp.maximum(m, sink)
    p_c, p_n = jnp.exp(s_c - m), jnp.exp(s_n - m)
    l = jnp.sum(p_c, axis=-1, keepdims=True) + jnp.sum(p_n, axis=-1, keepdims=True)
    if sink is not None:
        l = l + jnp.exp(sink - m)
    o = lax.dot_general(p_c.astype(BF16), vt.astype(BF16), NT_DIMS, preferred_element_type=F32)
    o = o + lax.dot_general(p_n.astype(BF16), nv.astype(BF16), NT_DIMS, preferred_element_type=F32)
    return o * (1.0 / l), m + jnp.log(l)


def _sample_body(n, c, sink_ref, qa_ref, qb_ref, na_ref, n0_ref, n1_ref, n2_ref,
                 ca_ref, c0_ref, c1_ref, c2_ref,
                 oa_ref, ob_ref, ra_ref, r0_ref, r1_ref, r2_ref):
    shift =(LANES - T_NEW) - (n % SEQS_PER_LANE_BLOCK) * T_NEW
    lane = lax.broadcasted_iota(jnp.int32, (1, LANES), 1)
    lo = lane < HEAD_DIM

    for lp in range(S_PAIRS):
        rows = slice(lp * LANES, (lp + 1) * LANES)
        outs, lses = [], []
        for g, (c_ref, n_ref, r_ref) in enumerate(((c0_ref, n0_ref, r0_ref), (c1_ref, n1_ref, r1_ref),
                                                   (c2_ref, n2_ref, r2_ref))):
            dil = B_PATTERNS[g][1]
            nk = pltpu.roll(n_ref[0, rows, :], shift, 1)
            nv = pltpu.roll(n_ref[1, rows, :], shift, 1)
            qs = _stack_halves(qb_ref[g * B_PAIRS + c * S_PAIRS + lp], lo).astype(BF16)
            o, lse = _sample_attend(qs, c_ref[0, rows, :], c_ref[1, rows, :], nk, nv, dil)
            outs.append(jnp.where(lo, o[0:T_NEW], o[T_NEW:2 * T_NEW]))
            lse = jnp.broadcast_to(lse, o.shape)
            lses.append(jnp.where(lo, lse[0:T_NEW], lse[T_NEW:2 * T_NEW]))
            _roll_cache(c_ref, nk, r_ref, 0, lp * LANES, LANES)
            _roll_cache(c_ref, nv, r_ref, 1, lp * LANES, LANES)
        ob_ref[:, rows] = _combine3(outs, lses)

    nk = pltpu.roll(na_ref[0], shift, 1)
    nv = pltpu.roll(na_ref[1], shift, 1)
    _roll_cache(ca_ref, nk, ra_ref, 0, 0, A_KV_W)
    _roll_cache(ca_ref, nv, ra_ref, 1, 0, A_KV_W)
    hrow = lax.broadcasted_iota(jnp.int32, (A_GROUP * T_NEW, 1), 0) // T_NEW
    for kh in range(A_KV_HEADS):
        hs = slice(kh * HEAD_DIM, (kh + 1) * HEAD_DIM)
        dup = lambda x: jnp.concatenate([x[hs], x[hs]], axis=0)
        tiles = [_stack_halves(qa_ref[:, (kh * (A_GROUP // 2) + p) * LANES:(kh * (A_GROUP // 2) + p + 1) * LANES], lo)
                 for p in range(A_GROUP // 2)]
        qs = jnp.concatenate(tiles, axis=0).astype(BF16)
        sink = jnp.zeros((A_GROUP * T_NEW, 1), F32)
        for hh in range(A_GROUP):
            sink = jnp.where(hrow == hh, sink_ref[kh * A_GROUP + hh], sink)
        o, _ = _sample_attend(qs, dup(ca_ref[0]), dup(ca_ref[1]), dup(nk), dup(nv), 1, sink)
        for p in range(A_GROUP // 2):
            o0 = o[(2 * p) * T_NEW:(2 * p + 1) * T_NEW, :]
            o1 = o[(2 * p + 1) * T_NEW:(2 * p + 2) * T_NEW, :]
            c0 = (kh * (A_GROUP // 2) + p) * LANES
            oa_ref[:, c0:c0 + LANES] = jnp.where(lo, o0, o1)


N_SAMPLE_IN, N_SAMPLE_OUT = 11, 6


def _sample_specs(idx, nt, cache_a, cache_bs):
    rows = S_PAIRS * LANES
    seq = lambda *g: idx(*g)[0]
    chunk = lambda *g: idx(*g)[1]
    lane_blk = lambda *g: idx(*g)[0] // SEQS_PER_LANE_BLOCK
    in_specs = [
        pl.BlockSpec(memory_space=pltpu.SMEM),
        pl.BlockSpec((T_NEW, A_Q_W), lambda *g: (seq(*g), 0)),
        pl.BlockSpec((N_B_GROUPS * B_PAIRS, T_NEW, LANES), lambda *g: (0, seq(*g), 0)),
        pl.BlockSpec((2, A_KV_W, LANES), lambda *g: (0, 0, lane_blk(*g))),
    ]
    in_specs += [pl.BlockSpec((2, rows, LANES), lambda *g: (0, chunk(*g), lane_blk(*g)))] * N_B_GROUPS
    cache_specs = [pl.BlockSpec((None, 2, A_KV_W, cache_a.shape[-1]), lambda *g: (seq(*g), 0, 0, 0))]
    cache_specs += [pl.BlockSpec((None, 2, rows, cb.shape[-1]), lambda *g: (seq(*g), 0, chunk(*g), 0))
                    for cb in cache_bs]
    out_specs = [pl.BlockSpec((T_NEW, A_Q_W), lambda *g: (seq(*g), 0)),
                 pl.BlockSpec((T_NEW, rows), lambda *g: (seq(*g), chunk(*g)))] + cache_specs
    out_shape = [jax.ShapeDtypeStruct((nt, A_Q_W), F32), jax.ShapeDtypeStruct((nt, B_OUT_W), F32),
                 jax.ShapeDtypeStruct(cache_a.shape, F32)] + [jax.ShapeDtypeStruct(cb.shape, F32) for cb in cache_bs]
    return in_specs + cache_specs, out_specs, out_shape


def _layer_norm(z, g, b):
    mu = jnp.mean(z, axis=-1, keepdims=True)
    zc = z - mu
    var = jnp.mean(zc * zc, axis=-1, keepdims=True)
    return zc * lax.rsqrt(var + LN_EPS) * g + b


def _mix_kernel(x_ref, oa_ref, ob_ref, g_ref, wa_ref, wb_ref, wo_ref, lng_ref, lnb_ref, h_ref):
    br_a = jnp.dot(oa_ref[...].astype(BF16), wa_ref[...], preferred_element_type=F32)
    br_b = jnp.dot(ob_ref[...].astype(BF16), wb_ref[...], preferred_element_type=F32)
    ga = g_ref[:, 0:D_MODEL].astype(F32)
    gb = g_ref[:, D_MODEL:2 * D_MODEL].astype(F32)
    merged = (ga * br_a + gb * br_b).astype(BF16)
    mix = jnp.dot(merged, wo_ref[...], preferred_element_type=F32)
    h_ref[...] = _layer_norm(ALPHA * x_ref[...] + mix, lng_ref[...], lnb_ref[...])


def _mix(x2d, oa, ob, gates, wa, wb, wo, ln_g, ln_b, tm):
    m = x2d.shape[0]
    row = lambda w: pl.BlockSpec((tm, w), lambda i: (i, 0))
    const = lambda a: pl.BlockSpec(a.shape, lambda i: (0, 0), pipeline_mode=pl.Buffered(1))
    return pl.pallas_call(
        _mix_kernel, grid=(m // tm,),
        in_specs=[row(D_MODEL), row(A_Q_W), row(B_OUT_W), row(2 * D_MODEL),
                  const(wa), const(wb), const(wo), const(ln_g), const(ln_b)],
        out_specs=row(D_MODEL), out_shape=jax.ShapeDtypeStruct((m, D_MODEL), F32),
        compiler_params=_cparams(("parallel",)), name="mix",
    )(x2d, oa, ob, gates, wa, wb, wo, ln_g, ln_b)


FFN_OUT_CHUNK = 512


N_FFN_IN = 7


def _ffn_kernel(*refs, with_sample):
    h_ref, w1_ref, b1_ref, w2_ref, b2_ref, lng_ref, lnb_ref = refs[:N_FFN_IN]
    if with_sample:
        sample_in = refs[N_FFN_IN:N_FFN_IN + N_SAMPLE_IN]
        y_ref = refs[N_FFN_IN + N_SAMPLE_IN]
        sample_out = refs[N_FFN_IN + N_SAMPLE_IN + 1:N_FFN_IN + N_SAMPLE_IN + 1 + N_SAMPLE_OUT]
    else:
        y_ref = refs[N_FFN_IN]
    hb_ref = refs[-1]
    f = pl.program_id(1)

    @pl.when(f == 0)
    def _():
        hb_ref[...] = h_ref[...].astype(BF16)
        y_ref[...] = jnp.zeros_like(y_ref)

    if with_sample:
        step = pl.program_id(0) * pl.num_programs(1) + f
        _sample_body(step // SAMPLE_CHUNKS, step % SAMPLE_CHUNKS, *sample_in, *sample_out)

    u = jnp.dot(hb_ref[...], w1_ref[...], preferred_element_type=F32) + b1_ref[...]
    u = jnp.square(jnp.maximum(u, 0.0)).astype(BF16)
    for c in range(D_MODEL // FFN_OUT_CHUNK):
        cs = slice(c * FFN_OUT_CHUNK, (c + 1) * FFN_OUT_CHUNK)
        y_ref[:, cs] += jnp.dot(u, w2_ref[:, cs], preferred_element_type=F32)

    @pl.when(f == pl.num_programs(1) - 1)
    def _():
        z = ALPHA * h_ref[...] + y_ref[...] + b2_ref[...]
        y_ref[...] = _layer_norm(z, lng_ref[...], lnb_ref[...])


def _ffn(h, w1, b1, w2, b2, ln_g, ln_b, tm, tf, sample_args=None):
    m = h.shape[0]
    grid = (m // tm, D_FF // tf)
    const = lambda a: pl.BlockSpec(a.shape, lambda i, f: (0, 0))
    once = pl.Buffered(1) if tm > 512 else None
    in_specs = [pl.BlockSpec((tm, D_MODEL), lambda i, f: (i, 0), pipeline_mode=once),
                pl.BlockSpec((D_MODEL, tf), lambda i, f: (0, f)),
                pl.BlockSpec((1, tf), lambda i, f: (0, f)),
                pl.BlockSpec((tf, D_MODEL), lambda i, f: (f, 0)),
                const(b2), const(ln_g), const(ln_b)]
    out_specs = [pl.BlockSpec((tm, D_MODEL), lambda i, f: (i, 0), pipeline_mode=once)]
    out_shape = [jax.ShapeDtypeStruct((m, D_MODEL), F32)]
    args = [h, w1, b1, w2, b2, ln_g, ln_b]
    if sample_args is not None:
        sinks, qa, qb, new_a, new_bs, cache_a, cache_bs = sample_args
        nt = qa.shape[0]
        assert grid[0] * grid[1] == (nt // T_NEW) * SAMPLE_CHUNKS

        def idx(i, f):
            step = i * grid[1] + f
            return step // SAMPLE_CHUNKS, step % SAMPLE_CHUNKS

        s_in, s_out, s_shape = _sample_specs(idx, nt, cache_a, cache_bs)
        in_specs += s_in
        out_specs += s_out
        out_shape += s_shape
        args += [sinks, qa, qb, new_a, *new_bs, cache_a, *cache_bs]
    out = pl.pallas_call(
        functools.partial(_ffn_kernel, with_sample=sample_args is not None),
        grid=grid, in_specs=in_specs, out_specs=out_specs, out_shape=out_shape,
        scratch_shapes=[pltpu.VMEM((tm, D_MODEL), BF16)],
        compiler_params=_cparams(("arbitrary", "arbitrary")),
        name="ffn_sample" if sample_args is not None else "ffn",
    )(*args)
    return out if sample_args is not None else out[0]


def _to_cache_layout(kvt, heads):
    n, _, _, rows = kvt.shape
    return jnp.transpose(kvt.reshape(n, 2, heads, HEAD_DIM, rows), (0, 4, 1, 2, 3))[None]


def _from_cache_layout(cache):
    _, n, rows, _, heads, _ = cache.shape
    return jnp.transpose(cache[0], (0, 2, 3, 4, 1)).reshape(n, 2, heads * HEAD_DIM, rows)


def kernel(x_prompt, x_sample, cache_a_kv, cache_b1_kv, cache_b2_kv, cache_b3_kv, w_in, a_sinks,
           w_branch_a, w_branch_b, w_out, ln1_g, ln1_b, w_ff1, b_ff1, w_ff2, b_ff2, ln2_g, ln2_b):
    n_p, seq, _ = x_prompt.shape
    n_s, t_new, _ = x_sample.shape
    assert t_new == T_NEW and w_in.shape[0] == 1
    b_caches = (cache_b1_kv, cache_b2_kv, cache_b3_kv)

    w_cat = _reorder_w_in(w_in[0])
    wa, wb, wo = w_branch_a[0].astype(BF16), w_branch_b[0].astype(BF16), w_out[0].astype(BF16)
    w1, w2 = w_ff1[0].astype(BF16), w_ff2[0].astype(BF16)
    sinks = a_sinks[0].reshape(A_Q_HEADS).astype(F32)
    row = lambda v: v[0].reshape(1, -1)

    def finish(x2d, oa, ob, gates, sample_args=None):
        h = _mix(x2d, oa, ob, gates, wa, wb, wo, row(ln1_g), row(ln1_b), tm=512)
        return _ffn(h, w1, row(b_ff1), w2, row(b_ff2), row(ln2_g), row(ln2_b), 512, 512, sample_args)

    xp = x_prompt.reshape(n_p * seq, D_MODEL)
    cos_p, sin_p = _rope_tables(jnp.tile(jnp.arange(seq), n_p))
    (qa, kva, kvt_a, qb, kvb0, kvt0, kvb1, kvt1, kvb2, kvt2, gates) = _proj(
        xp, w_cat, cos_p, sin_p, BF16, 512, n_p, seq)
    kvbs, kvts = (kvb0, kvb1, kvb2), (kvt0, kvt1, kvt2)
    nt = n_s * T_NEW
    xs = x_sample.reshape(nt, D_MODEL)
    cos_s, sin_s = _rope_tables(jnp.tile(PAST_LEN + jnp.arange(T_NEW), n_s))
    (qa_s, _, kvt_a_s, qb_s, _, kvt0_s, _, kvt1_s, _, kvt2_s, gates_s) = _proj(
        xs, w_cat, cos_s, sin_s, F32, 512, 1, nt)

    o_a = _attn_a_prompt(sinks, qa, kva, n_p, seq)
    outs, lses = [], []
    for g, (win, dil) in enumerate(B_PATTERNS):
        o, lse = _attn_b_prompt(qb, kvbs[g], g, dil, n_p, seq)
        outs.append(o)
        lses.append(lse)
    o_b = _combine(outs, lses, tm=1024)
    sample_args = (sinks, qa_s, qb_s, kvt_a_s[0], [kvt0_s[0], kvt1_s[0], kvt2_s[0]],
                   _from_cache_layout(cache_a_kv), [_from_cache_layout(cb) for cb in b_caches])
    y_p, o_a_s, o_b_s, r_a, r_b1, r_b2, r_b3 = finish(xp, o_a, o_b, gates, sample_args)
    y_p = y_p.reshape(n_p, seq, D_MODEL)
    a_kv_p = _to_cache_layout(kvt_a[..., seq - min(A_WINDOW, seq):], A_KV_HEADS)
    b_kv_p = [_to_cache_layout(kvts[g][..., seq - min(win, seq):], B_HEADS)
              for g, (win, dil) in enumerate(B_PATTERNS)]

    y_s = finish(xs, o_a_s, o_b_s, gates_s).reshape(n_s, T_NEW, D_MODEL)
    a_kv_s = _to_cache_layout(r_a, A_KV_HEADS)
    b_kv_s = [_to_cache_layout(r, B_HEADS) for r in (r_b1, r_b2, r_b3)]

    return (y_p, y_s, a_kv_p, a_kv_s,
            b_kv_p[0], b_kv_s[0], b_kv_p[1], b_kv_s[1], b_kv_p[2], b_kv_s[2])
```

```python
import functools

import numpy as np
import jax
import jax.numpy as jnp
from jax import lax
from jax.experimental import pallas as pl
from jax.experimental.pallas import tpu as pltpu

D_MODEL = 2048
HEAD_DIM = 64
HALF = HEAD_DIM // 2
A_Q_HEADS = 16
A_KV_HEADS = 2
A_GROUP = A_Q_HEADS // A_KV_HEADS
A_WINDOW = 128
B_PATTERNS = ((128, 1), (512, 4), (2048, 16))
N_B_GROUPS = len(B_PATTERNS)
B_HEADS = 8
B_KEYS = 128
BLOCK = 128
D_FF = 4 * D_MODEL
ROPE_THETA = 10000.0
ALPHA = 2.0 ** 0.25
LN_EPS = 1e-5
NEG = -1e30
PAST_LEN = 8192

A_Q_W = A_Q_HEADS * HEAD_DIM
A_KV_W = A_KV_HEADS * HEAD_DIM
B_W = N_B_GROUPS * B_HEADS * HEAD_DIM
B_OUT_W = B_HEADS * HEAD_DIM
LANES = 128
B_PAIRS = B_OUT_W // LANES
VMEM_LIMIT = 60 * 1024 * 1024

F32 = jnp.float32
BF16 = jnp.bfloat16
NT_DIMS = (((1,), (1,)), ((), ()))


def _cparams(sem):
    return pltpu.CompilerParams(dimension_semantics=sem, vmem_limit_bytes=VMEM_LIMIT)


PROJ_TN = 512
PROJ_SUB = 256
_T_QA, _T_KVA, _T_QB, _T_KVB, _T_G = 0, 2, 3, 6, 12
PROJ_TILES = 20


def _reorder_w_in(w_in):
    o = 0
    qa = w_in[:, o:o + A_Q_W]; o += A_Q_W
    ka = w_in[:, o:o + A_KV_W]; o += A_KV_W
    va = w_in[:, o:o + A_KV_W]; o += A_KV_W
    qb = w_in[:, o:o + B_W]; o += B_W
    kb = w_in[:, o:o + B_W]; o += B_W
    vb = w_in[:, o:o + B_W]; o += B_W
    gates = w_in[:, o:]
    pad = jnp.zeros((w_in.shape[0], PROJ_TN - 2 * A_KV_W), w_in.dtype)
    cols = [qa, ka, va, pad, qb]
    for g in range(N_B_GROUPS):
        cols += [kb[:, g * B_OUT_W:(g + 1) * B_OUT_W], vb[:, g * B_OUT_W:(g + 1) * B_OUT_W]]
    cols.append(gates)
    return jnp.concatenate(cols, axis=1).astype(BF16)


def _rope_tables(pos):
    inv = 1.0 / (ROPE_THETA ** (jnp.arange(HALF, dtype=F32) / HALF))
    ang = pos.astype(F32)[:, None] * inv[None, :]
    cos, sin = jnp.cos(ang), jnp.sin(ang)
    cosf = jnp.concatenate([cos, cos, cos, cos], axis=1)
    sinf = jnp.concatenate([-sin, sin, -sin, sin], axis=1)
    return cosf, sinf


def _rope_chunk(t, cos, sin, lo):
    sw = jnp.where(lo, pltpu.roll(t, LANES - HALF, 1), pltpu.roll(t, HALF, 1))
    return t * cos + sw * sin


def _proj_kernel(x_ref, w_ref, cos_ref, sin_ref,
                 qa_ref, kva_ref, kvta_ref, qb_ref,
                 kvb0_ref, kvt0_ref, kvb1_ref, kvt1_ref, kvb2_ref, kvt2_ref, g_ref, xb_ref):
    j = pl.program_id(1)

    @pl.when(j == 0)
    def _():
        xb_ref[...] = x_ref[...].astype(BF16)

    lane = lax.broadcasted_iota(jnp.int32, (1, LANES), 1)
    lo = (lane % HEAD_DIM) < HALF
    scale = HEAD_DIM ** -0.5

    def tile(n_halves=PROJ_TN // PROJ_SUB):
        xb = xb_ref[...]
        return [jnp.dot(xb, w_ref[:, h * PROJ_SUB:(h + 1) * PROJ_SUB], preferred_element_type=F32)
                for h in range(n_halves)]

    def chunk(acc, c, rope, mul=1.0):
        per = PROJ_SUB // LANES
        t = acc[c // per][:, (c % per) * LANES:(c % per + 1) * LANES]
        if rope:
            t = _rope_chunk(t, cos_ref[...], sin_ref[...], lo)
        return t if mul == 1.0 else t * mul

    @pl.when(j < _T_KVA)
    def _():
        acc = tile()
        for c in range(PROJ_TN // LANES):
            qa_ref[:, c * LANES:(c + 1) * LANES] = chunk(acc, c, True, scale).astype(qa_ref.dtype)

    @pl.when(j == _T_KVA)
    def _():
        acc = tile(1)
        k, v = chunk(acc, 0, True), chunk(acc, 1, False)
        kva_ref[:, 0:LANES] = k
        kva_ref[:, LANES:2 * LANES] = v
        kvta_ref[0] = k.T
        kvta_ref[1] = v.T

    @pl.when((j >= _T_QB) & (j < _T_KVB))
    def _():
        acc = tile()
        for c in range(B_PAIRS):
            qb_ref[c] = chunk(acc, c, True, scale)

    for g, (slab_ref, t_ref) in enumerate(((kvb0_ref, kvt0_ref), (kvb1_ref, kvt1_ref), (kvb2_ref, kvt2_ref))):
        for is_v in (0, 1):
            @pl.when(j == _T_KVB + 2 * g + is_v)
            def _(slab_ref=slab_ref, t_ref=t_ref, is_v=is_v):
                acc = tile()
                for c in range(B_PAIRS):
                    t = chunk(acc, c, not is_v)
                    slab_ref[c] = t
                    t_ref[c * LANES:(c + 1) * LANES, :] = t.T

    @pl.when(j >= _T_G)
    def _():
        for h, acc in enumerate(tile()):
            g_ref[:, h * PROJ_SUB:(h + 1) * PROJ_SUB] = jax.nn.sigmoid(acc).astype(g_ref.dtype)


def _proj(x2d, w_cat, cosf, sinf, q_dtype, tm, n_seq, seq):
    m = x2d.shape[0]
    assert m == n_seq * seq and seq % tm == 0
    per_seq = seq // tm
    grid = (m // tm, PROJ_TILES)

    def clamp(j, lo, n):
        return jnp.clip(j - lo, 0, n - 1)

    in_specs = [
        pl.BlockSpec((tm, D_MODEL), lambda i, j: (i, 0)),
        pl.BlockSpec((D_MODEL, PROJ_TN), lambda i, j: (0, j)),
        pl.BlockSpec((tm, LANES), lambda i, j: (i, 0)),
        pl.BlockSpec((tm, LANES), lambda i, j: (i, 0)),
    ]
    out_specs = [
        pl.BlockSpec((tm, PROJ_TN), lambda i, j: (i, clamp(j, _T_QA, 2))),
        pl.BlockSpec((tm, 2 * A_KV_W), lambda i, j: (i, 0)),
        pl.BlockSpec((None, 2, A_KV_W, tm), lambda i, j: (i // per_seq, 0, 0, i % per_seq)),
        pl.BlockSpec((B_PAIRS, tm, LANES), lambda i, j: (clamp(j, _T_QB, N_B_GROUPS), i, 0)),
    ]
    out_shape = [
        jax.ShapeDtypeStruct((m, A_Q_W), q_dtype),
        jax.ShapeDtypeStruct((m, 2 * A_KV_W), F32),
        jax.ShapeDtypeStruct((n_seq, 2, A_KV_W, seq), F32),
        jax.ShapeDtypeStruct((N_B_GROUPS * B_PAIRS, m, LANES), F32),
    ]
    for g in range(N_B_GROUPS):
        t0 = _T_KVB + 2 * g
        out_specs.append(pl.BlockSpec((B_PAIRS, tm, LANES), lambda i, j, t0=t0: (clamp(j, t0, 2), i, 0)))
        out_specs.append(pl.BlockSpec((None, None, B_OUT_W, tm),
                                      lambda i, j, t0=t0: (i // per_seq, clamp(j, t0, 2), 0, i % per_seq)))
        out_shape.append(jax.ShapeDtypeStruct((2 * B_PAIRS, m, LANES), F32))
        out_shape.append(jax.ShapeDtypeStruct((n_seq, 2, B_OUT_W, seq), F32))
    out_specs.append(pl.BlockSpec((tm, PROJ_TN), lambda i, j: (i, clamp(j, _T_G, 8))))
    out_shape.append(jax.ShapeDtypeStruct((m, 2 * D_MODEL), BF16))
    return pl.pallas_call(
        _proj_kernel, grid=grid, in_specs=in_specs, out_specs=out_specs, out_shape=out_shape,
        scratch_shapes=[pltpu.VMEM((tm, D_MODEL), BF16)],
        compiler_params=_cparams(("parallel", "arbitrary")), name="proj",
    )(x2d, w_cat, cosf, sinf)


def _band_mask(first_block, rows):
    qi = lax.broadcasted_iota(jnp.int32, (rows, 2 * BLOCK), 0) % BLOCK
    sj = lax.broadcasted_iota(jnp.int32, (rows, 2 * BLOCK), 1)
    dist = BLOCK + qi - sj
    lo = jnp.where(first_block, BLOCK, 0)
    return (dist >= 0) & (dist < B_KEYS) & (sj >= lo)


def _softmax_rows(s, valid, sink=None):
    s = jnp.where(valid, s, NEG)
    m = jnp.max(s, axis=-1, keepdims=True)
    if sink is not None:
        m = jnp.maximum(m, sink)
    p = jnp.exp(s - m)
    l = jnp.sum(p, axis=-1, keepdims=True)
    if sink is not None:
        l = l + jnp.exp(sink - m)
    return p, l, m


def _stack_halves(q128, lo):
    return jnp.concatenate([jnp.where(lo, q128, 0.0), jnp.where(lo, 0.0, q128)], axis=0)


def _attn_b_kernel(*refs, dil, has_prev):
    if has_prev:
        q_ref, kvc_ref, kvp_ref, o_ref, lse_ref = refs
    else:
        q_ref, kvc_ref, o_ref, lse_ref = refs
    lane = lax.broadcasted_iota(jnp.int32, (1, LANES), 1)
    lo = lane < HEAD_DIM
    if has_prev:
        valid = _band_mask(pl.program_id(1) == 0, 2 * BLOCK)
    else:
        qi = lax.broadcasted_iota(jnp.int32, (2 * BLOCK, BLOCK), 0) % BLOCK
        sj = lax.broadcasted_iota(jnp.int32, (2 * BLOCK, BLOCK), 1)
        valid = qi >= sj

    def residue(r):
        rows = pl.ds(r, BLOCK, stride=dil) if dil > 1 else pl.ds(0, BLOCK)
        for p in range(B_PAIRS):
            qs = _stack_halves(q_ref[p, rows, :], lo).astype(BF16)
            k2, v2 = kvc_ref[p, rows, :], kvc_ref[B_PAIRS + p, rows, :]
            if has_prev:
                k2 = jnp.concatenate([kvp_ref[p, rows, :], k2], axis=0)
                v2 = jnp.concatenate([kvp_ref[B_PAIRS + p, rows, :], v2], axis=0)
            s = lax.dot_general(qs, k2.astype(BF16), NT_DIMS, preferred_element_type=F32)
            pr, l, m = _softmax_rows(s, valid)
            o = jnp.dot(pr.astype(BF16), v2.astype(BF16), preferred_element_type=F32) * (1.0 / l)
            lse = jnp.broadcast_to(m + jnp.log(l), o.shape)
            o_ref[p, rows, :] = jnp.where(lo, o[0:BLOCK], o[BLOCK:2 * BLOCK])
            lse_ref[p, rows, :] = jnp.where(lo, lse[0:BLOCK], lse[BLOCK:2 * BLOCK])

    if dil == 1:
        residue(0)
    else:
        def body(r, c):
            residue(r)
            return c
        lax.fori_loop(0, dil, body, 0)


def _attn_b_prompt(qb, kvb, g, dil, n_seq, seq):
    t = qb.shape[1]
    rows = BLOCK * dil
    nb = seq // rows
    qv = qb.reshape(qb.shape[0], n_seq, seq, LANES)
    kvv = kvb.reshape(kvb.shape[0], n_seq, seq, LANES)
    in_specs = [pl.BlockSpec((B_PAIRS, None, rows, LANES), lambda n, b: (g, n, b, 0)),
                pl.BlockSpec((2 * B_PAIRS, None, rows, LANES), lambda n, b: (0, n, b, 0))]
    args = [qv, kvv]
    if nb > 1:
        in_specs.append(pl.BlockSpec((2 * B_PAIRS, None, rows, LANES), lambda n, b: (0, n, jnp.maximum(b - 1, 0), 0)))
        args.append(kvv)
    out_spec = pl.BlockSpec((B_PAIRS, None, rows, LANES), lambda n, b: (0, n, b, 0))
    o, lse = pl.pallas_call(
        functools.partial(_attn_b_kernel, dil=dil, has_prev=nb > 1),
        grid=(n_seq, nb), in_specs=in_specs, out_specs=[out_spec, out_spec],
        out_shape=[jax.ShapeDtypeStruct((B_PAIRS, n_seq, seq, LANES), F32)] * 2,
        compiler_params=_cparams(("parallel", "arbitrary")), name=f"attn_b{g}",
    )(*args)
    return o.reshape(B_PAIRS, t, LANES), lse.reshape(B_PAIRS, t, LANES)


def _dup_heads(x128, lo):
    xr = pltpu.roll(x128, HEAD_DIM, 1)
    return jnp.where(lo, x128, xr), jnp.where(lo, xr, x128)


def _attn_a_kernel(sink_ref, q_ref, kvc_ref, kvp_ref, o_ref):
    valid = _band_mask(pl.program_id(1) == 0, BLOCK)
    kv2 = jnp.concatenate([kvp_ref[...], kvc_ref[...]], axis=0)
    lane = lax.broadcasted_iota(jnp.int32, (1, LANES), 1)
    lo = lane < HEAD_DIM
    kdup = [t.astype(BF16) for t in _dup_heads(kv2[:, 0:LANES], lo)]
    vdup = [t.astype(BF16) for t in _dup_heads(kv2[:, LANES:2 * LANES], lo)]
    for p in range(A_Q_HEADS // 2):
        kh = (2 * p) // A_GROUP
        q128 = q_ref[:, p * LANES:(p + 1) * LANES]
        halves = []
        for e in range(2):
            qm = jnp.where(lo if e == 0 else jnp.logical_not(lo), q128, jnp.zeros_like(q128))
            s = lax.dot_general(qm, kdup[kh], NT_DIMS, preferred_element_type=F32)
            pe, l, _ = _softmax_rows(s, valid, sink_ref[2 * p + e])
            o = jnp.dot(pe.astype(BF16), vdup[kh], preferred_element_type=F32)
            halves.append(o * (1.0 / l))
        o_ref[:, p * LANES:(p + 1) * LANES] = jnp.where(lo, halves[0], halves[1]).astype(o_ref.dtype)


def _attn_a_prompt(sinks, qa, kva, n_seq, seq):
    t = qa.shape[0]
    nb = seq // BLOCK
    qv = qa.reshape(n_seq, seq, A_Q_W)
    kvv = kva.reshape(n_seq, seq, 2 * A_KV_W)
    in_specs = [
        pl.BlockSpec(memory_space=pltpu.SMEM),
        pl.BlockSpec((None, BLOCK, A_Q_W), lambda n, b: (n, b, 0)),
        pl.BlockSpec((None, BLOCK, 2 * A_KV_W), lambda n, b: (n, b, 0)),
        pl.BlockSpec((None, BLOCK, 2 * A_KV_W), lambda n, b: (n, jnp.maximum(b - 1, 0), 0)),
    ]
    o = pl.pallas_call(
        _attn_a_kernel, grid=(n_seq, nb), in_specs=in_specs,
        out_specs=pl.BlockSpec((None, BLOCK, A_Q_W), lambda n, b: (n, b, 0)),
        out_shape=jax.ShapeDtypeStruct((n_seq, seq, A_Q_W), BF16),
        compiler_params=_cparams(("parallel", "arbitrary")), name="attn_a",
    )(sinks, qv, kvv, kvv)
    return o.reshape(t, A_Q_W)


def _combine3(outs, lses):
    m = jnp.maximum(jnp.maximum(lses[0], lses[1]), lses[2])
    es = [jnp.exp(x - m) for x in lses]
    num = es[0] * outs[0] + es[1] * outs[1] + es[2] * outs[2]
    return num / (es[0] + es[1] + es[2])


def _combine_kernel(o0_ref, o1_ref, o2_ref, l0_ref, l1_ref, l2_ref, ob_ref):
    for p in range(B_PAIRS):
        ob_ref[:, p * LANES:(p + 1) * LANES] = _combine3(
            [o0_ref[p], o1_ref[p], o2_ref[p]], [l0_ref[p], l1_ref[p], l2_ref[p]]).astype(ob_ref.dtype)


def _combine(outs, lses, tm):
    t = outs[0].shape[1]
    spec = pl.BlockSpec((B_PAIRS, tm, LANES), lambda i: (0, i, 0))
    return pl.pallas_call(
        _combine_kernel, grid=(t // tm,), in_specs=[spec] * 6,
        out_specs=pl.BlockSpec((tm, B_OUT_W), lambda i: (i, 0)),
        out_shape=jax.ShapeDtypeStruct((t, B_OUT_W), BF16),
        compiler_params=_cparams(("parallel",)), name="combine",
    )(*outs, *lses)


T_NEW = 8
SEQS_PER_LANE_BLOCK = LANES // T_NEW
S_PAIRS = 2
SAMPLE_CHUNKS = B_PAIRS // S_PAIRS
ROLL_ROWS = 64


def _roll_cache(c_ref, new_rolled, o_ref, kv, row0, nrows):
    lb = c_ref.shape[-1]
    lane = lax.broadcasted_iota(jnp.int32, (1, LANES), 1)
    tail = lane >= LANES - T_NEW
    pieces = []
    for r0 in range(0, nrows, ROLL_ROWS):
        rs = slice(row0 + r0, row0 + r0 + ROLL_ROWS)
        x = c_ref[kv, rs, :]
        pieces.append(x.astype(BF16))
        t = pltpu.roll(x, lb - T_NEW, 1)
        if lb > LANES:
            o_ref[kv, rs, 0:lb - LANES] = t[:, 0:lb - LANES]
        o_ref[kv, rs, lb - LANES:lb] = jnp.where(tail, new_rolled[r0:r0 + ROLL_ROWS, :], t[:, lb - LANES:lb])
    return jnp.concatenate(pieces, axis=0)


def _sample_scores(qs, kt, nk, dil, lb):
    rows = qs.shape[0]
    s_c = jnp.dot(qs, kt, preferred_element_type=F32)
    s_n = jnp.dot(qs, nk.astype(BF16), preferred_element_type=F32)
    tok = lax.broadcasted_iota(jnp.int32, (rows, 1), 0) % T_NEW
    pos = lax.broadcasted_iota(jnp.int32, (rows, lb), 1)
    d_c = lb + tok - pos
    valid_c = ((d_c & (dil - 1)) == 0) & (d_c <= (B_KEYS - 1) * dil)
    lane = lax.broadcasted_iota(jnp.int32, (rows, LANES), 1)
    d_n = tok - (lane - (LANES - T_NEW))
    valid_n = (lane >= LANES - T_NEW) & (d_n >= 0) & ((d_n & (dil - 1)) == 0)
    return jnp.where(valid_c, s_c, NEG), jnp.where(valid_n, s_n, NEG)


def _sample_probs(qs, kt, nk, dil, sink=None):
    s_c, s_n = _sample_scores(qs, kt, nk, dil, kt.shape[-1])
    m = jnp.maximum(jnp.max(s_c, axis=-1, keepdims=True), jnp.max(s_n, axis=-1, keepdims=True))
    if sink is not None:
        m = jnp.maximum(m, sink)
    p_c, p_n = jnp.exp(s_c - m), jnp.exp(s_n - m)
    l = jnp.sum(p_c, axis=-1, keepdims=True) + jnp.sum(p_n, axis=-1, keepdims=True)
    if sink is not None:
        l = l + jnp.exp(sink - m)
    return p_c.astype(BF16), p_n.astype(BF16), l, m


def _sample_values(p_c, p_n, l, m, vt, nv):
    o = lax.dot_general(p_c, vt, NT_DIMS, preferred_element_type=F32)
    o = o + lax.dot_general(p_n, nv.astype(BF16), NT_DIMS, preferred_element_type=F32)
    return o * (1.0 / l), m + jnp.log(l)


def _sample_units(n, c, sink_ref, qa_ref, qb_ref, na_ref, n0_ref, n1_ref, n2_ref,
                  ca_ref, c0_ref, c1_ref, c2_ref,
                  oa_ref, ob_ref, ra_ref, r0_ref, r1_ref, r2_ref):
    shift = (LANES - T_NEW) - (n % SEQS_PER_LANE_BLOCK) * T_NEW
    lane = lax.broadcasted_iota(jnp.int32, (1, LANES), 1)
    lo = lane < HEAD_DIM
    pending, done = {}, {}
    units = []

    groups = ((c0_ref, n0_ref, r0_ref), (c1_ref, n1_ref, r1_ref), (c2_ref, n2_ref, r2_ref))
    for lp in range(S_PAIRS):
        for g, (c_ref, n_ref, r_ref) in enumerate(groups):
            def first(lp=lp, g=g, c_ref=c_ref, n_ref=n_ref, r_ref=r_ref):
                nrows = pl.ds(pl.multiple_of((c * S_PAIRS + lp) * LANES, LANES), LANES)
                nk = pltpu.roll(n_ref[0, nrows, :], shift, 1)
                nv = pltpu.roll(n_ref[1, nrows, :], shift, 1)
                qs = _stack_halves(qb_ref[g * B_PAIRS + c * S_PAIRS + lp], lo).astype(BF16)
                kt = _roll_cache(c_ref, nk, r_ref, 0, lp * LANES, LANES)
                vt = _roll_cache(c_ref, nv, r_ref, 1, lp * LANES, LANES)
                pending[lp, g] = _sample_probs(qs, kt, nk, B_PATTERNS[g][1]) + (vt, nv)

            def second(lp=lp, g=g):
                o, lse = _sample_values(*pending.pop((lp, g)))
                lse = jnp.broadcast_to(lse, o.shape)
                done[lp, g] = (jnp.where(lo, o[0:T_NEW], o[T_NEW:2 * T_NEW]),
                               jnp.where(lo, lse[0:T_NEW], lse[T_NEW:2 * T_NEW]))
                if g == N_B_GROUPS - 1:
                    outs, lses = zip(*[done.pop((lp, k)) for k in range(N_B_GROUPS)])
                    ob_ref[:, lp * LANES:(lp + 1) * LANES] = _combine3(outs, lses)

            units.append((first, second))

    hrow = lax.broadcasted_iota(jnp.int32, (A_GROUP * T_NEW, 1), 0) // T_NEW
    for kh in range(A_KV_HEADS):
        def first(kh=kh):
            if kh == 0:
                nk = pltpu.roll(na_ref[0], shift, 1)
                nv = pltpu.roll(na_ref[1], shift, 1)
                pending["a"] = (nk, nv, _roll_cache(ca_ref, nk, ra_ref, 0, 0, A_KV_W),
                                _roll_cache(ca_ref, nv, ra_ref, 1, 0, A_KV_W))
            nk, nv, kt_a, vt_a = pending["a"]
            hs = slice(kh * HEAD_DIM, (kh + 1) * HEAD_DIM)
            dup = lambda x: jnp.concatenate([x[hs], x[hs]], axis=0)
            tiles = [_stack_halves(qa_ref[:, (kh * (A_GROUP // 2) + p) * LANES:(kh * (A_GROUP // 2) + p + 1) * LANES], lo)
                     for p in range(A_GROUP // 2)]
            qs = jnp.concatenate(tiles, axis=0).astype(BF16)
            sink = jnp.zeros((A_GROUP * T_NEW, 1), F32)
            for hh in range(A_GROUP):
                sink = jnp.where(hrow == hh, sink_ref[kh * A_GROUP + hh], sink)
            pending["a", kh] = _sample_probs(qs, dup(kt_a), dup(nk), 1, sink) + (dup(vt_a), dup(nv))

        def second(kh=kh):
            o, _ = _sample_values(*pending.pop(("a", kh)))
            for p in range(A_GROUP // 2):
                o0 = o[(2 * p) * T_NEW:(2 * p + 1) * T_NEW, :]
                o1 = o[(2 * p + 1) * T_NEW:(2 * p + 2) * T_NEW, :]
                c0 = (kh * (A_GROUP // 2) + p) * LANES
                oa_ref[:, c0:c0 + LANES] = jnp.where(lo, o0, o1)

        units.append((first, second))
    return units


N_SAMPLE_IN, N_SAMPLE_OUT = 11, 6


def _sample_specs(idx, nt, cache_a, cache_bs):
    rows = S_PAIRS * LANES
    seq = lambda *g: idx(*g)[0]
    chunk = lambda *g: idx(*g)[1]
    lane_blk = lambda *g: idx(*g)[0] // SEQS_PER_LANE_BLOCK
    in_specs = [
        pl.BlockSpec(memory_space=pltpu.SMEM),
        pl.BlockSpec((T_NEW, A_Q_W), lambda *g: (seq(*g), 0)),
        pl.BlockSpec((N_B_GROUPS * B_PAIRS, T_NEW, LANES), lambda *g: (0, seq(*g), 0)),
        pl.BlockSpec((2, A_KV_W, LANES), lambda *g: (0, 0, lane_blk(*g))),
    ]
    in_specs += [pl.BlockSpec((2, B_OUT_W, LANES), lambda *g: (0, 0, lane_blk(*g)))] * N_B_GROUPS
    cache_specs = [pl.BlockSpec((None, 2, A_KV_W, cache_a.shape[-1]), lambda *g: (seq(*g), 0, 0, 0))]
    cache_specs += [pl.BlockSpec((None, 2, rows, cb.shape[-1]), lambda *g: (seq(*g), 0, chunk(*g), 0))
                    for cb in cache_bs]
    out_specs = [pl.BlockSpec((T_NEW, A_Q_W), lambda *g: (seq(*g), 0)),
                 pl.BlockSpec((T_NEW, rows), lambda *g: (seq(*g), chunk(*g)))] + cache_specs
    out_shape = [jax.ShapeDtypeStruct((nt, A_Q_W), F32), jax.ShapeDtypeStruct((nt, B_OUT_W), F32),
                 jax.ShapeDtypeStruct(cache_a.shape, F32)] + [jax.ShapeDtypeStruct(cb.shape, F32) for cb in cache_bs]
    return in_specs + cache_specs, out_specs, out_shape


def _layer_norm(z, g, b):
    mu = jnp.mean(z, axis=-1, keepdims=True)
    zc = z - mu
    var = jnp.mean(zc * zc, axis=-1, keepdims=True)
    return zc * lax.rsqrt(var + LN_EPS) * g + b


def _mix_kernel(x_ref, oa_ref, ob_ref, g_ref, wa_ref, wb_ref, wo_ref, lng_ref, lnb_ref, h_ref):
    br_a = jnp.dot(oa_ref[...].astype(BF16), wa_ref[...], preferred_element_type=F32)
    br_b = jnp.dot(ob_ref[...].astype(BF16), wb_ref[...], preferred_element_type=F32)
    ga = g_ref[:, 0:D_MODEL].astype(F32)
    gb = g_ref[:, D_MODEL:2 * D_MODEL].astype(F32)
    merged = (ga * br_a + gb * br_b).astype(BF16)
    mix = jnp.dot(merged, wo_ref[...], preferred_element_type=F32)
    h_ref[...] = _layer_norm(ALPHA * x_ref[...] + mix, lng_ref[...], lnb_ref[...])


def _mix(x2d, oa, ob, gates, wa, wb, wo, ln_g, ln_b, tm):
    m = x2d.shape[0]
    row = lambda w: pl.BlockSpec((tm, w), lambda i: (i, 0))
    const = lambda a: pl.BlockSpec(a.shape, lambda i: (0, 0), pipeline_mode=pl.Buffered(1))
    return pl.pallas_call(
        _mix_kernel, grid=(m // tm,),
        in_specs=[row(D_MODEL), row(A_Q_W), row(B_OUT_W), row(2 * D_MODEL),
                  const(wa), const(wb), const(wo), const(ln_g), const(ln_b)],
        out_specs=row(D_MODEL), out_shape=jax.ShapeDtypeStruct((m, D_MODEL), F32),
        compiler_params=_cparams(("parallel",)), name="mix",
    )(x2d, oa, ob, gates, wa, wb, wo, ln_g, ln_b)


FFN_OUT_CHUNK = 512
LN_ROWS = 128


N_FFN_IN = 7


def _ffn_kernel(*refs, with_sample, row_split):
    h_ref, w1_ref, b1_ref, w2_ref, b2_ref, lng_ref, lnb_ref = refs[:N_FFN_IN]
    if with_sample:
        sample_in = refs[N_FFN_IN:N_FFN_IN + N_SAMPLE_IN]
        y_ref = refs[N_FFN_IN + N_SAMPLE_IN]
        sample_out = refs[N_FFN_IN + N_SAMPLE_IN + 1:N_FFN_IN + N_SAMPLE_IN + 1 + N_SAMPLE_OUT]
    else:
        y_ref = refs[N_FFN_IN]
    hb_ref = refs[-1]
    f, r = pl.program_id(1), pl.program_id(2)
    n_f, n_r = pl.num_programs(1), pl.num_programs(2)
    sub = h_ref.shape[0] // row_split
    tf = w1_ref.shape[1]
    rows = pl.ds(pl.multiple_of(r * sub, sub), sub)

    @pl.when((f == 0) & (r == 0))
    def _():
        hb_ref[...] = h_ref[...].astype(BF16)
        y_ref[...] = jnp.zeros_like(y_ref)

    parts = {}

    def up(k):
        cs = slice(k * (tf // 2), (k + 1) * (tf // 2))
        t = jnp.dot(hb_ref[rows, :], w1_ref[:, cs], preferred_element_type=F32) + b1_ref[:, cs]
        parts[k] = jnp.square(jnp.maximum(t, 0.0)).astype(BF16)

    def down(k):
        if "u" not in parts:
            parts["u"] = jnp.concatenate([parts.pop(0), parts.pop(1)], axis=1)
        cs = slice(k * FFN_OUT_CHUNK, (k + 1) * FFN_OUT_CHUNK)
        y_ref[rows, cs] += jnp.dot(parts["u"], w2_ref[:, cs], preferred_element_type=F32)

    pieces = [functools.partial(up, 0), functools.partial(up, 1)]
    pieces += [functools.partial(down, k) for k in range(D_MODEL // FFN_OUT_CHUNK)]
    if with_sample:
        step = (pl.program_id(0) * n_f + f) * n_r + r
        units = _sample_units(step // SAMPLE_CHUNKS, step % SAMPLE_CHUNKS, *sample_in, *sample_out)
        units[0][0]()
        for k, piece in enumerate(pieces):
            if k + 1 < len(units):
                units[k + 1][0]()
            piece()
            units[k][1]()
        for k in range(len(pieces) + 1, len(units)):
            units[k][0]()
        for k in range(len(pieces), len(units)):
            units[k][1]()
    else:
        for piece in pieces:
            piece()

    @pl.when((f == n_f - 1) & (r == n_r - 1))
    def _():
        def ln_rows(k, carry):
            rs = pl.ds(pl.multiple_of(k * LN_ROWS, LN_ROWS), LN_ROWS)
            z = ALPHA * h_ref[rs, :] + y_ref[rs, :] + b2_ref[...]
            y_ref[rs, :] = _layer_norm(z, lng_ref[...], lnb_ref[...])
            return carry
        lax.fori_loop(0, h_ref.shape[0] // LN_ROWS, ln_rows, 0)


def _ffn(h, w1, b1, w2, b2, ln_g, ln_b, tm, tf, row_split, sample_args=None):
    m = h.shape[0]
    grid = (m // tm, D_FF // tf, row_split)
    const = lambda a: pl.BlockSpec(a.shape, lambda i, f, r: (0, 0))
    once = pl.Buffered(1) if row_split > 1 else None
    in_specs = [pl.BlockSpec((tm, D_MODEL), lambda i, f, r: (i, 0), pipeline_mode=once),
                pl.BlockSpec((D_MODEL, tf), lambda i, f, r: (0, f)),
                pl.BlockSpec((1, tf), lambda i, f, r: (0, f)),
                pl.BlockSpec((tf, D_MODEL), lambda i, f, r: (f, 0)),
                const(b2), const(ln_g), const(ln_b)]
    out_specs = [pl.BlockSpec((tm, D_MODEL), lambda i, f, r: (i, 0), pipeline_mode=once)]
    out_shape = [jax.ShapeDtypeStruct((m, D_MODEL), F32)]
    args = [h, w1, b1, w2, b2, ln_g, ln_b]
    if sample_args is not None:
        sinks, qa, qb, new_a, new_bs, cache_a, cache_bs = sample_args
        nt = qa.shape[0]
        assert grid[0] * grid[1] * grid[2] == (nt // T_NEW) * SAMPLE_CHUNKS

        def idx(i, f, r):
            step = (i * grid[1] + f) * grid[2] + r
            return step // SAMPLE_CHUNKS, step % SAMPLE_CHUNKS

        s_in, s_out, s_shape = _sample_specs(idx, nt, cache_a, cache_bs)
        in_specs += s_in
        out_specs += s_out
        out_shape += s_shape
        args += [sinks, qa, qb, new_a, *new_bs, cache_a, *cache_bs]
    out = pl.pallas_call(
        functools.partial(_ffn_kernel, with_sample=sample_args is not None, row_split=row_split),
        grid=grid, in_specs=in_specs, out_specs=out_specs, out_shape=out_shape,
        scratch_shapes=[pltpu.VMEM((tm, D_MODEL), BF16)],
        compiler_params=_cparams(("arbitrary", "arbitrary", "arbitrary")),
        name="ffn_sample" if sample_args is not None else "ffn",
    )(*args)
    return out if sample_args is not None else out[0]


def _to_cache_layout(kvt, heads):
    n, _, _, rows = kvt.shape
    return jnp.transpose(kvt.reshape(n, 2, heads, HEAD_DIM, rows), (0, 4, 1, 2, 3))[None]


def _from_cache_layout(cache):
    _, n, rows, _, heads, _ = cache.shape
    return jnp.transpose(cache[0], (0, 2, 3, 4, 1)).reshape(n, 2, heads * HEAD_DIM, rows)


def kernel(x_prompt, x_sample, cache_a_kv, cache_b1_kv, cache_b2_kv, cache_b3_kv, w_in, a_sinks,
           w_branch_a, w_branch_b, w_out, ln1_g, ln1_b, w_ff1, b_ff1, w_ff2, b_ff2, ln2_g, ln2_b):
    n_p, seq, _ = x_prompt.shape
    n_s, t_new, _ = x_sample.shape
    assert t_new == T_NEW and w_in.shape[0] == 1
    b_caches = (cache_b1_kv, cache_b2_kv, cache_b3_kv)

    w_cat = _reorder_w_in(w_in[0])
    wa, wb, wo = w_branch_a[0].astype(BF16), w_branch_b[0].astype(BF16), w_out[0].astype(BF16)
    w1, w2 = w_ff1[0].astype(BF16), w_ff2[0].astype(BF16)
    sinks = a_sinks[0].reshape(A_Q_HEADS).astype(F32)
    row = lambda v: v[0].reshape(1, -1)

    def finish(x2d, oa, ob, gates, sample_args=None):
        h = _mix(x2d, oa, ob, gates, wa, wb, wo, row(ln1_g), row(ln1_b), tm=512)
        tm, row_split = (1024, 2) if sample_args is not None else (512, 1)
        return _ffn(h, w1, row(b_ff1), w2, row(b_ff2), row(ln2_g), row(ln2_b), tm, 512, row_split, sample_args)

    xp = x_prompt.reshape(n_p * seq, D_MODEL)
    cos_p, sin_p = _rope_tables(jnp.tile(jnp.arange(seq), n_p))
    (qa, kva, kvt_a, qb, kvb0, kvt0, kvb1, kvt1, kvb2, kvt2, gates) = _proj(
        xp, w_cat, cos_p, sin_p, BF16, 512, n_p, seq)
    kvbs, kvts = (kvb0, kvb1, kvb2), (kvt0, kvt1, kvt2)
    nt = n_s * T_NEW
    xs = x_sample.reshape(nt, D_MODEL)
    cos_s, sin_s = _rope_tables(jnp.tile(PAST_LEN + jnp.arange(T_NEW), n_s))
    (qa_s, _, kvt_a_s, qb_s, _, kvt0_s, _, kvt1_s, _, kvt2_s, gates_s) = _proj(
        xs, w_cat, cos_s, sin_s, F32, 512, 1, nt)

    o_a = _attn_a_prompt(sinks, qa, kva, n_p, seq)
    outs, lses = [], []
    for g, (win, dil) in enumerate(B_PATTERNS):
        o, lse = _attn_b_prompt(qb, kvbs[g], g, dil, n_p, seq)
        outs.append(o)
        lses.append(lse)
    o_b = _combine(outs, lses, tm=1024)
    sample_args = (sinks, qa_s, qb_s, kvt_a_s[0], [kvt0_s[0], kvt1_s[0], kvt2_s[0]],
                   _from_cache_layout(cache_a_kv), [_from_cache_layout(cb) for cb in b_caches])
    y_p, o_a_s, o_b_s, r_a, r_b1, r_b2, r_b3 = finish(xp, o_a, o_b, gates, sample_args)
    y_p = y_p.reshape(n_p, seq, D_MODEL)
    a_kv_p = _to_cache_layout(kvt_a[..., seq - min(A_WINDOW, seq):], A_KV_HEADS)
    b_kv_p = [_to_cache_layout(kvts[g][..., seq - min(win, seq):], B_HEADS)
              for g, (win, dil) in enumerate(B_PATTERNS)]

    y_s = finish(xs, o_a_s, o_b_s, gates_s).reshape(n_s, T_NEW, D_MODEL)
    a_kv_s = _to_cache_layout(r_a, A_KV_HEADS)
    b_kv_s = [_to_cache_layout(r, B_HEADS) for r in (r_b1, r_b2, r_b3)]

    return (y_p, y_s, a_kv_p, a_kv_s,
            b_kv_p[0], b_kv_s[0], b_kv_p[1], b_kv_s[1], b_kv_p[2], b_kv_s[2])
```

```python
import functools

import numpy as np
import jax
import jax.numpy as jnp
from jax import lax
from jax.experimental import pallas as pl
from jax.experimental.pallas import tpu as pltpu

D_MODEL = 2048
HEAD_DIM = 64
HALF = HEAD_DIM // 2
A_Q_HEADS = 16
A_KV_HEADS = 2
A_GROUP = A_Q_HEADS // A_KV_HEADS
A_WINDOW = 128
B_PATTERNS = ((128, 1), (512, 4), (2048, 16))
N_B_GROUPS = len(B_PATTERNS)
B_HEADS = 8
B_KEYS = 128
BLOCK = 128
D_FF = 4 * D_MODEL
ROPE_THETA = 10000.0
ALPHA = 2.0 ** 0.25
LN_EPS = 1e-5
NEG = -1e30
PAST_LEN = 8192

A_Q_W = A_Q_HEADS * HEAD_DIM
A_KV_W = A_KV_HEADS * HEAD_DIM
B_W = N_B_GROUPS * B_HEADS * HEAD_DIM
B_OUT_W = B_HEADS * HEAD_DIM
LANES = 128
B_PAIRS = B_OUT_W // LANES
VMEM_LIMIT = 60 * 1024 * 1024

F32 = jnp.float32
BF16 = jnp.bfloat16
NT_DIMS = (((1,), (1,)), ((), ()))


def _cparams(sem):
    return pltpu.CompilerParams(dimension_semantics=sem, vmem_limit_bytes=VMEM_LIMIT)


PROJ_TN = 512
PROJ_SUB = 256
_T_QA, _T_KVA, _T_QB, _T_KVB = 0, 2, 3, 6
PROJ_TILES = 12


def _reorder_w_in(w_in):
    o = 0
    qa = w_in[:, o:o + A_Q_W]; o += A_Q_W
    ka = w_in[:, o:o + A_KV_W]; o += A_KV_W
    va = w_in[:, o:o + A_KV_W]; o += A_KV_W
    qb = w_in[:, o:o + B_W]; o += B_W
    kb = w_in[:, o:o + B_W]; o += B_W
    vb = w_in[:, o:o + B_W]; o += B_W
    gates = w_in[:, o:]
    pad = jnp.zeros((w_in.shape[0], PROJ_TN - 2 * A_KV_W), w_in.dtype)
    cols = [qa, ka, va, pad, qb]
    for g in range(N_B_GROUPS):
        cols += [kb[:, g * B_OUT_W:(g + 1) * B_OUT_W], vb[:, g * B_OUT_W:(g + 1) * B_OUT_W]]
    return jnp.concatenate(cols, axis=1).astype(BF16), gates.astype(BF16)


def _rope_tables(pos):
    inv = 1.0 / (ROPE_THETA ** (jnp.arange(HALF, dtype=F32) / HALF))
    ang = pos.astype(F32)[:, None] * inv[None, :]
    cos, sin = jnp.cos(ang), jnp.sin(ang)
    cosf = jnp.concatenate([cos, cos, cos, cos], axis=1)
    sinf = jnp.concatenate([-sin, sin, -sin, sin], axis=1)
    return cosf, sinf


def _rope_chunk(t, cos, sin, lo):
    sw = jnp.where(lo, pltpu.roll(t, LANES - HALF, 1), pltpu.roll(t, HALF, 1))
    return t * cos + sw * sin


def _proj_kernel(x_ref, w_ref, cos_ref, sin_ref,
                 qa_ref, kva_ref, kvta_ref, qb_ref,
                 kvb0_ref, kvt0_ref, kvb1_ref, kvt1_ref, kvb2_ref, kvt2_ref, xb_ref):
    j = pl.program_id(1)

    @pl.when(j == 0)
    def _():
        xb_ref[...] = x_ref[...].astype(BF16)

    lane = lax.broadcasted_iota(jnp.int32, (1, LANES), 1)
    lo = (lane % HEAD_DIM) < HALF
    scale = HEAD_DIM ** -0.5

    def tile(n_halves=PROJ_TN // PROJ_SUB):
        xb = xb_ref[...]
        return [jnp.dot(xb, w_ref[:, h * PROJ_SUB:(h + 1) * PROJ_SUB], preferred_element_type=F32)
                for h in range(n_halves)]

    def chunk(acc, c, rope, mul=1.0):
        per = PROJ_SUB // LANES
        t = acc[c // per][:, (c % per) * LANES:(c % per + 1) * LANES]
        if rope:
            t = _rope_chunk(t, cos_ref[...], sin_ref[...], lo)
        return t if mul == 1.0 else t * mul

    @pl.when(j < _T_KVA)
    def _():
        acc = tile()
        for c in range(PROJ_TN // LANES):
            qa_ref[:, c * LANES:(c + 1) * LANES] = chunk(acc, c, True, scale).astype(qa_ref.dtype)

    @pl.when(j == _T_KVA)
    def _():
        acc = tile(1)
        k, v = chunk(acc, 0, True), chunk(acc, 1, False)
        kva_ref[:, 0:LANES] = k
        kva_ref[:, LANES:2 * LANES] = v
        kvta_ref[0] = k.T
        kvta_ref[1] = v.T

    @pl.when((j >= _T_QB) & (j < _T_KVB))
    def _():
        acc = tile()
        for c in range(B_PAIRS):
            qb_ref[c] = chunk(acc, c, True, scale)

    for g, (slab_ref, t_ref) in enumerate(((kvb0_ref, kvt0_ref), (kvb1_ref, kvt1_ref), (kvb2_ref, kvt2_ref))):
        for is_v in (0, 1):
            @pl.when(j == _T_KVB + 2 * g + is_v)
            def _(slab_ref=slab_ref, t_ref=t_ref, is_v=is_v):
                acc = tile()
                for c in range(B_PAIRS):
                    t = chunk(acc, c, not is_v)
                    slab_ref[c] = t
                    t_ref[c * LANES:(c + 1) * LANES, :] = t.T


def _proj(x2d, w_cat, cosf, sinf, q_dtype, tm, n_seq, seq):
    m = x2d.shape[0]
    assert m == n_seq * seq and seq % tm == 0
    per_seq = seq // tm
    grid = (m // tm, PROJ_TILES)

    def clamp(j, lo, n):
        return jnp.clip(j - lo, 0, n - 1)

    in_specs = [
        pl.BlockSpec((tm, D_MODEL), lambda i, j: (i, 0)),
        pl.BlockSpec((D_MODEL, PROJ_TN), lambda i, j: (0, j)),
        pl.BlockSpec((tm, LANES), lambda i, j: (i, 0)),
        pl.BlockSpec((tm, LANES), lambda i, j: (i, 0)),
    ]
    out_specs = [
        pl.BlockSpec((tm, PROJ_TN), lambda i, j: (i, clamp(j, _T_QA, 2))),
        pl.BlockSpec((tm, 2 * A_KV_W), lambda i, j: (i, 0)),
        pl.BlockSpec((None, 2, A_KV_W, tm), lambda i, j: (i // per_seq, 0, 0, i % per_seq)),
        pl.BlockSpec((B_PAIRS, tm, LANES), lambda i, j: (clamp(j, _T_QB, N_B_GROUPS), i, 0)),
    ]
    out_shape = [
        jax.ShapeDtypeStruct((m, A_Q_W), q_dtype),
        jax.ShapeDtypeStruct((m, 2 * A_KV_W), F32),
        jax.ShapeDtypeStruct((n_seq, 2, A_KV_W, seq), F32),
        jax.ShapeDtypeStruct((N_B_GROUPS * B_PAIRS, m, LANES), F32),
    ]
    for g in range(N_B_GROUPS):
        t0 = _T_KVB + 2 * g
        out_specs.append(pl.BlockSpec((B_PAIRS, tm, LANES), lambda i, j, t0=t0: (clamp(j, t0, 2), i, 0)))
        out_specs.append(pl.BlockSpec((None, None, B_OUT_W, tm),
                                      lambda i, j, t0=t0: (i // per_seq, clamp(j, t0, 2), 0, i % per_seq)))
        out_shape.append(jax.ShapeDtypeStruct((2 * B_PAIRS, m, LANES), F32))
        out_shape.append(jax.ShapeDtypeStruct((n_seq, 2, B_OUT_W, seq), F32))
    return pl.pallas_call(
        _proj_kernel, grid=grid, in_specs=in_specs, out_specs=out_specs, out_shape=out_shape,
        scratch_shapes=[pltpu.VMEM((tm, D_MODEL), BF16)],
        compiler_params=_cparams(("parallel", "arbitrary")), name="proj",
    )(x2d, w_cat, cosf, sinf)


GATE_TN = 512


def _gates_kernel(x_ref, w_ref, g_ref, xb_ref):
    @pl.when(pl.program_id(1) == 0)
    def _():
        xb_ref[...] = x_ref[...].astype(BF16)

    g_ref[...] = jnp.dot(xb_ref[...], w_ref[...], preferred_element_type=F32).astype(g_ref.dtype)


def _gates(x2d, w_g, tm):
    m = x2d.shape[0]
    n = w_g.shape[1]
    return pl.pallas_call(
        _gates_kernel, grid=(m // tm, n // GATE_TN),
        in_specs=[pl.BlockSpec((tm, D_MODEL), lambda i, j: (i, 0)),
                  pl.BlockSpec((D_MODEL, GATE_TN), lambda i, j: (0, j))],
        out_specs=pl.BlockSpec((tm, GATE_TN), lambda i, j: (i, j)),
        out_shape=jax.ShapeDtypeStruct((m, n), BF16),
        scratch_shapes=[pltpu.VMEM((tm, D_MODEL), BF16)],
        compiler_params=_cparams(("parallel", "arbitrary")), name="gates",
    )(x2d, w_g)


def _band_mask(first_block, rows):
    qi = lax.broadcasted_iota(jnp.int32, (rows, 2 * BLOCK), 0) % BLOCK
    sj = lax.broadcasted_iota(jnp.int32, (rows, 2 * BLOCK), 1)
    dist = BLOCK + qi - sj
    lo = jnp.where(first_block, BLOCK, 0)
    return (dist >= 0) & (dist < B_KEYS) & (sj >= lo)


def _softmax_rows(s, valid, sink=None):
    s = jnp.where(valid, s, NEG)
    m = jnp.max(s, axis=-1, keepdims=True)
    if sink is not None:
        m = jnp.maximum(m, sink)
    p = jnp.exp(s - m)
    l = jnp.sum(p, axis=-1, keepdims=True)
    if sink is not None:
        l = l + jnp.exp(sink - m)
    return p, l, m


def _stack_halves(q128, lo):
    return jnp.concatenate([jnp.where(lo, q128, 0.0), jnp.where(lo, 0.0, q128)], axis=0)


def _attn_b_kernel(*refs, dil, has_prev):
    if has_prev:
        q_ref, kvc_ref, kvp_ref, o_ref, lse_ref = refs
    else:
        q_ref, kvc_ref, o_ref, lse_ref = refs
    lane = lax.broadcasted_iota(jnp.int32, (1, LANES), 1)
    lo = lane < HEAD_DIM
    if has_prev:
        valid = _band_mask(pl.program_id(1) == 0, 2 * BLOCK)
    else:
        qi = lax.broadcasted_iota(jnp.int32, (2 * BLOCK, BLOCK), 0) % BLOCK
        sj = lax.broadcasted_iota(jnp.int32, (2 * BLOCK, BLOCK), 1)
        valid = qi >= sj

    def residue(r):
        rows = pl.ds(r, BLOCK, stride=dil) if dil > 1 else pl.ds(0, BLOCK)
        for p in range(B_PAIRS):
            qs = _stack_halves(q_ref[p, rows, :], lo).astype(BF16)
            k2, v2 = kvc_ref[p, rows, :], kvc_ref[B_PAIRS + p, rows, :]
            if has_prev:
                k2 = jnp.concatenate([kvp_ref[p, rows, :], k2], axis=0)
                v2 = jnp.concatenate([kvp_ref[B_PAIRS + p, rows, :], v2], axis=0)
            s = lax.dot_general(qs, k2.astype(BF16), NT_DIMS, preferred_element_type=F32)
            pr, l, m = _softmax_rows(s, valid)
            o = jnp.dot(pr.astype(BF16), v2.astype(BF16), preferred_element_type=F32) * (1.0 / l)
            lse = jnp.broadcast_to(m + jnp.log(l), o.shape)
            o_ref[p, rows, :] = jnp.where(lo, o[0:BLOCK], o[BLOCK:2 * BLOCK])
            lse_ref[p, rows, :] = jnp.where(lo, lse[0:BLOCK], lse[BLOCK:2 * BLOCK])

    if dil == 1:
        residue(0)
    else:
        def body(r, c):
            residue(r)
            return c
        lax.fori_loop(0, dil, body, 0)


def _attn_b_prompt(qb, kvb, g, dil, n_seq, seq):
    t = qb.shape[1]
    rows = BLOCK * dil
    nb = seq // rows
    qv = qb.reshape(qb.shape[0], n_seq, seq, LANES)
    kvv = kvb.reshape(kvb.shape[0], n_seq, seq, LANES)
    in_specs = [pl.BlockSpec((B_PAIRS, None, rows, LANES), lambda n, b: (g, n, b, 0)),
                pl.BlockSpec((2 * B_PAIRS, None, rows, LANES), lambda n, b: (0, n, b, 0))]
    args = [qv, kvv]
    if nb > 1:
        in_specs.append(pl.BlockSpec((2 * B_PAIRS, None, rows, LANES), lambda n, b: (0, n, jnp.maximum(b - 1, 0), 0)))
        args.append(kvv)
    out_spec = pl.BlockSpec((B_PAIRS, None, rows, LANES), lambda n, b: (0, n, b, 0))
    o, lse = pl.pallas_call(
        functools.partial(_attn_b_kernel, dil=dil, has_prev=nb > 1),
        grid=(n_seq, nb), in_specs=in_specs, out_specs=[out_spec, out_spec],
        out_shape=[jax.ShapeDtypeStruct((B_PAIRS, n_seq, seq, LANES), F32)] * 2,
        compiler_params=_cparams(("parallel", "arbitrary")), name=f"attn_b{g}",
    )(*args)
    return o.reshape(B_PAIRS, t, LANES), lse.reshape(B_PAIRS, t, LANES)


def _dup_heads(x128, lo):
    xr = pltpu.roll(x128, HEAD_DIM, 1)
    return jnp.where(lo, x128, xr), jnp.where(lo, xr, x128)


def _attn_a_kernel(sink_ref, q_ref, kvc_ref, kvp_ref, o_ref):
    valid = _band_mask(pl.program_id(1) == 0, BLOCK)
    kv2 = jnp.concatenate([kvp_ref[...], kvc_ref[...]], axis=0)
    lane = lax.broadcasted_iota(jnp.int32, (1, LANES), 1)
    lo = lane < HEAD_DIM
    kdup = [t.astype(BF16) for t in _dup_heads(kv2[:, 0:LANES], lo)]
    vdup = [t.astype(BF16) for t in _dup_heads(kv2[:, LANES:2 * LANES], lo)]
    for p in range(A_Q_HEADS // 2):
        kh = (2 * p) // A_GROUP
        q128 = q_ref[:, p * LANES:(p + 1) * LANES]
        halves = []
        for e in range(2):
            qm = jnp.where(lo if e == 0 else jnp.logical_not(lo), q128, jnp.zeros_like(q128))
            s = lax.dot_general(qm, kdup[kh], NT_DIMS, preferred_element_type=F32)
            pe, l, _ = _softmax_rows(s, valid, sink_ref[2 * p + e])
            o = jnp.dot(pe.astype(BF16), vdup[kh], preferred_element_type=F32)
            halves.append(o * (1.0 / l))
        o_ref[:, p * LANES:(p + 1) * LANES] = jnp.where(lo, halves[0], halves[1]).astype(o_ref.dtype)


def _attn_a_prompt(sinks, qa, kva, n_seq, seq):
    t = qa.shape[0]
    nb = seq // BLOCK
    qv = qa.reshape(n_seq, seq, A_Q_W)
    kvv = kva.reshape(n_seq, seq, 2 * A_KV_W)
    in_specs = [
        pl.BlockSpec(memory_space=pltpu.SMEM),
        pl.BlockSpec((None, BLOCK, A_Q_W), lambda n, b: (n, b, 0)),
        pl.BlockSpec((None, BLOCK, 2 * A_KV_W), lambda n, b: (n, b, 0)),
        pl.BlockSpec((None, BLOCK, 2 * A_KV_W), lambda n, b: (n, jnp.maximum(b - 1, 0), 0)),
    ]
    o = pl.pallas_call(
        _attn_a_kernel, grid=(n_seq, nb), in_specs=in_specs,
        out_specs=pl.BlockSpec((None, BLOCK, A_Q_W), lambda n, b: (n, b, 0)),
        out_shape=jax.ShapeDtypeStruct((n_seq, seq, A_Q_W), BF16),
        compiler_params=_cparams(("parallel", "arbitrary")), name="attn_a",
    )(sinks, qv, kvv, kvv)
    return o.reshape(t, A_Q_W)


def _combine3(outs, lses):
    m = jnp.maximum(jnp.maximum(lses[0], lses[1]), lses[2])
    es = [jnp.exp(x - m) for x in lses]
    num = es[0] * outs[0] + es[1] * outs[1] + es[2] * outs[2]
    return num / (es[0] + es[1] + es[2])


def _combine_kernel(o0_ref, o1_ref, o2_ref, l0_ref, l1_ref, l2_ref, ob_ref):
    for p in range(B_PAIRS):
        ob_ref[:, p * LANES:(p + 1) * LANES] = _combine3(
            [o0_ref[p], o1_ref[p], o2_ref[p]], [l0_ref[p], l1_ref[p], l2_ref[p]]).astype(ob_ref.dtype)


def _combine(outs, lses, tm):
    t = outs[0].shape[1]
    spec = pl.BlockSpec((B_PAIRS, tm, LANES), lambda i: (0, i, 0))
    return pl.pallas_call(
        _combine_kernel, grid=(t // tm,), in_specs=[spec] * 6,
        out_specs=pl.BlockSpec((tm, B_OUT_W), lambda i: (i, 0)),
        out_shape=jax.ShapeDtypeStruct((t, B_OUT_W), BF16),
        compiler_params=_cparams(("parallel",)), name="combine",
    )(*outs, *lses)


T_NEW = 8
SEQS_PER_LANE_BLOCK = LANES // T_NEW
S_PAIRS = 2
SAMPLE_CHUNKS = B_PAIRS // S_PAIRS
ROLL_ROWS = 64


def _roll_cache(c_ref, new_rolled, o_ref, kv, row0, nrows):
    lb = c_ref.shape[-1]
    lane = lax.broadcasted_iota(jnp.int32, (1, LANES), 1)
    tail = lane >= LANES - T_NEW
    pieces = []
    for r0 in range(0, nrows, ROLL_ROWS):
        rs = slice(row0 + r0, row0 + r0 + ROLL_ROWS)
        x = c_ref[kv, rs, :]
        pieces.append(x.astype(BF16))
        t = pltpu.roll(x, lb - T_NEW, 1)
        if lb > LANES:
            o_ref[kv, rs, 0:lb - LANES] = t[:, 0:lb - LANES]
        o_ref[kv, rs, lb - LANES:lb] = jnp.where(tail, new_rolled[r0:r0 + ROLL_ROWS, :], t[:, lb - LANES:lb])
    return jnp.concatenate(pieces, axis=0)


def _sample_scores(qs, kt, nk, dil, lb):
    rows = qs.shape[0]
    s_c = jnp.dot(qs, kt, preferred_element_type=F32)
    s_n = jnp.dot(qs, nk.astype(BF16), preferred_element_type=F32)
    tok = lax.broadcasted_iota(jnp.int32, (rows, 1), 0) % T_NEW
    pos = lax.broadcasted_iota(jnp.int32, (rows, lb), 1)
    d_c = lb + tok - pos
    valid_c = ((d_c & (dil - 1)) == 0) & (d_c <= (B_KEYS - 1) * dil)
    lane = lax.broadcasted_iota(jnp.int32, (rows, LANES), 1)
    d_n = tok - (lane - (LANES - T_NEW))
    valid_n = (lane >= LANES - T_NEW) & (d_n >= 0) & ((d_n & (dil - 1)) == 0)
    return jnp.where(valid_c, s_c, NEG), jnp.where(valid_n, s_n, NEG)


def _sample_probs(qs, kt, nk, dil, sink=None):
    s_c, s_n = _sample_scores(qs, kt, nk, dil, kt.shape[-1])
    m = jnp.maximum(jnp.max(s_c, axis=-1, keepdims=True), jnp.max(s_n, axis=-1, keepdims=True))
    if sink is not None:
        m = jnp.maximum(m, sink)
    p_c, p_n = jnp.exp(s_c - m), jnp.exp(s_n - m)
    l = jnp.sum(p_c, axis=-1, keepdims=True) + jnp.sum(p_n, axis=-1, keepdims=True)
    if sink is not None:
        l = l + jnp.exp(sink - m)
    return p_c.astype(BF16), p_n.astype(BF16), l, m


def _sample_values(p_c, p_n, l, m, vt, nv):
    o = lax.dot_general(p_c, vt, NT_DIMS, preferred_element_type=F32)
    o = o + lax.dot_general(p_n, nv.astype(BF16), NT_DIMS, preferred_element_type=F32)
    return o * (1.0 / l), m + jnp.log(l)


def _sample_units(n, c, sink_ref, qa_ref, qb_ref, na_ref, n0_ref, n1_ref, n2_ref,
                  ca_ref, c0_ref, c1_ref, c2_ref,
                  oa_ref, ob_ref, ra_ref, r0_ref, r1_ref, r2_ref):
    shift = (LANES - T_NEW) - (n % SEQS_PER_LANE_BLOCK) * T_NEW
    lane = lax.broadcasted_iota(jnp.int32, (1, LANES), 1)
    lo = lane < HEAD_DIM
    pending, done = {}, {}
    units = []

    groups = ((c0_ref, n0_ref, r0_ref), (c1_ref, n1_ref, r1_ref), (c2_ref, n2_ref, r2_ref))
    for lp in range(S_PAIRS):
        for g, (c_ref, n_ref, r_ref) in enumerate(groups):
            def first(lp=lp, g=g, c_ref=c_ref, n_ref=n_ref, r_ref=r_ref):
                nrows = pl.ds(pl.multiple_of((c * S_PAIRS + lp) * LANES, LANES), LANES)
                nk = pltpu.roll(n_ref[0, nrows, :], shift, 1)
                nv = pltpu.roll(n_ref[1, nrows, :], shift, 1)
                qs = _stack_halves(qb_ref[g * B_PAIRS + c * S_PAIRS + lp], lo).astype(BF16)
                kt = _roll_cache(c_ref, nk, r_ref, 0, lp * LANES, LANES)
                vt = _roll_cache(c_ref, nv, r_ref, 1, lp * LANES, LANES)
                pending[lp, g] = _sample_probs(qs, kt, nk, B_PATTERNS[g][1]) + (vt, nv)

            def second(lp=lp, g=g):
                o, lse = _sample_values(*pending.pop((lp, g)))
                lse = jnp.broadcast_to(lse, o.shape)
                done[lp, g] = (jnp.where(lo, o[0:T_NEW], o[T_NEW:2 * T_NEW]),
                               jnp.where(lo, lse[0:T_NEW], lse[T_NEW:2 * T_NEW]))
                if g == N_B_GROUPS - 1:
                    outs, lses = zip(*[done.pop((lp, k)) for k in range(N_B_GROUPS)])
                    ob_ref[:, lp * LANES:(lp + 1) * LANES] = _combine3(outs, lses)

            units.append((first, second))

    hrow = lax.broadcasted_iota(jnp.int32, (A_GROUP * T_NEW, 1), 0) // T_NEW
    for kh in range(A_KV_HEADS):
        def first(kh=kh):
            if kh == 0:
                nk = pltpu.roll(na_ref[0], shift, 1)
                nv = pltpu.roll(na_ref[1], shift, 1)
                pending["a"] = (nk, nv, _roll_cache(ca_ref, nk, ra_ref, 0, 0, A_KV_W),
                                _roll_cache(ca_ref, nv, ra_ref, 1, 0, A_KV_W))
            nk, nv, kt_a, vt_a = pending["a"]
            hs = slice(kh * HEAD_DIM, (kh + 1) * HEAD_DIM)
            dup = lambda x: jnp.concatenate([x[hs], x[hs]], axis=0)
            tiles = [_stack_halves(qa_ref[:, (kh * (A_GROUP // 2) + p) * LANES:(kh * (A_GROUP // 2) + p + 1) * LANES], lo)
                     for p in range(A_GROUP // 2)]
            qs = jnp.concatenate(tiles, axis=0).astype(BF16)
            sink = jnp.zeros((A_GROUP * T_NEW, 1), F32)
            for hh in range(A_GROUP):
                sink = jnp.where(hrow == hh, sink_ref[kh * A_GROUP + hh], sink)
            pending["a", kh] = _sample_probs(qs, dup(kt_a), dup(nk), 1, sink) + (dup(vt_a), dup(nv))

        def second(kh=kh):
            o, _ = _sample_values(*pending.pop(("a", kh)))
            for p in range(A_GROUP // 2):
                o0 = o[(2 * p) * T_NEW:(2 * p + 1) * T_NEW, :]
                o1 = o[(2 * p + 1) * T_NEW:(2 * p + 2) * T_NEW, :]
                c0 = (kh * (A_GROUP // 2) + p) * LANES
                oa_ref[:, c0:c0 + LANES] = jnp.where(lo, o0, o1)

        units.append((first, second))
    return units


N_SAMPLE_IN, N_SAMPLE_OUT = 11, 6


def _sample_specs(idx, nt, cache_a, cache_bs):
    rows = S_PAIRS * LANES
    seq = lambda *g: idx(*g)[0]
    chunk = lambda *g: idx(*g)[1]
    lane_blk = lambda *g: idx(*g)[0] // SEQS_PER_LANE_BLOCK
    in_specs = [
        pl.BlockSpec(memory_space=pltpu.SMEM),
        pl.BlockSpec((T_NEW, A_Q_W), lambda *g: (seq(*g), 0)),
        pl.BlockSpec((N_B_GROUPS * B_PAIRS, T_NEW, LANES), lambda *g: (0, seq(*g), 0)),
        pl.BlockSpec((2, A_KV_W, LANES), lambda *g: (0, 0, lane_blk(*g))),
    ]
    in_specs += [pl.BlockSpec((2, B_OUT_W, LANES), lambda *g: (0, 0, lane_blk(*g)))] * N_B_GROUPS
    cache_specs = [pl.BlockSpec((None, 2, A_KV_W, cache_a.shape[-1]), lambda *g: (seq(*g), 0, 0, 0))]
    cache_specs += [pl.BlockSpec((None, 2, rows, cb.shape[-1]), lambda *g: (seq(*g), 0, chunk(*g), 0))
                    for cb in cache_bs]
    out_specs = [pl.BlockSpec((T_NEW, A_Q_W), lambda *g: (seq(*g), 0)),
                 pl.BlockSpec((T_NEW, rows), lambda *g: (seq(*g), chunk(*g)))] + cache_specs
    out_shape = [jax.ShapeDtypeStruct((nt, A_Q_W), F32), jax.ShapeDtypeStruct((nt, B_OUT_W), F32),
                 jax.ShapeDtypeStruct(cache_a.shape, F32)] + [jax.ShapeDtypeStruct(cb.shape, F32) for cb in cache_bs]
    return in_specs + cache_specs, out_specs, out_shape


def _layer_norm(z, g, b):
    mu = jnp.mean(z, axis=-1, keepdims=True)
    zc = z - mu
    var = jnp.mean(zc * zc, axis=-1, keepdims=True)
    return zc * lax.rsqrt(var + LN_EPS) * g + b


def _mix_kernel(x_ref, oa_ref, ob_ref, g_ref, wa_ref, wb_ref, wo_ref, lng_ref, lnb_ref, h_ref):
    br_a = jnp.dot(oa_ref[...].astype(BF16), wa_ref[...], preferred_element_type=F32)
    br_b = jnp.dot(ob_ref[...].astype(BF16), wb_ref[...], preferred_element_type=F32)
    ga = jax.nn.sigmoid(g_ref[:, 0:D_MODEL].astype(F32))
    gb = jax.nn.sigmoid(g_ref[:, D_MODEL:2 * D_MODEL].astype(F32))
    merged = (ga * br_a + gb * br_b).astype(BF16)
    mix = jnp.dot(merged, wo_ref[...], preferred_element_type=F32)
    h_ref[...] = _layer_norm(ALPHA * x_ref[...] + mix, lng_ref[...], lnb_ref[...])


def _mix(x2d, oa, ob, gates, wa, wb, wo, ln_g, ln_b, tm):
    m = x2d.shape[0]
    row = lambda w: pl.BlockSpec((tm, w), lambda i: (i, 0))
    const = lambda a: pl.BlockSpec(a.shape, lambda i: (0, 0), pipeline_mode=pl.Buffered(1))
    return pl.pallas_call(
        _mix_kernel, grid=(m // tm,),
        in_specs=[row(D_MODEL), row(A_Q_W), row(B_OUT_W), row(2 * D_MODEL),
                  const(wa), const(wb), const(wo), const(ln_g), const(ln_b)],
        out_specs=row(D_MODEL), out_shape=jax.ShapeDtypeStruct((m, D_MODEL), F32),
        compiler_params=_cparams(("parallel",)), name="mix",
    )(x2d, oa, ob, gates, wa, wb, wo, ln_g, ln_b)


FFN_OUT_CHUNK = 512
LN_ROWS = 128


N_FFN_IN = 7


def _ffn_kernel(*refs, with_sample, row_split):
    h_ref, w1_ref, b1_ref, w2_ref, b2_ref, lng_ref, lnb_ref = refs[:N_FFN_IN]
    if with_sample:
        sample_in = refs[N_FFN_IN:N_FFN_IN + N_SAMPLE_IN]
        y_ref = refs[N_FFN_IN + N_SAMPLE_IN]
        sample_out = refs[N_FFN_IN + N_SAMPLE_IN + 1:N_FFN_IN + N_SAMPLE_IN + 1 + N_SAMPLE_OUT]
    else:
        y_ref = refs[N_FFN_IN]
    hb_ref = refs[-1]
    f, r = pl.program_id(1), pl.program_id(2)
    n_f, n_r = pl.num_programs(1), pl.num_programs(2)
    sub = h_ref.shape[0] // row_split
    tf = w1_ref.shape[1]
    rows = pl.ds(pl.multiple_of(r * sub, sub), sub)

    @pl.when((f == 0) & (r == 0))
    def _():
        hb_ref[...] = h_ref[...].astype(BF16)
        y_ref[...] = jnp.zeros_like(y_ref)

    parts = {}

    def up(k):
        cs = slice(k * (tf // 2), (k + 1) * (tf // 2))
        t = jnp.dot(hb_ref[rows, :], w1_ref[:, cs], preferred_element_type=F32) + b1_ref[:, cs]
        parts[k] = jnp.square(jnp.maximum(t, 0.0)).astype(BF16)

    def down(k):
        if "u" not in parts:
            parts["u"] = jnp.concatenate([parts.pop(0), parts.pop(1)], axis=1)
        cs = slice(k * FFN_OUT_CHUNK, (k + 1) * FFN_OUT_CHUNK)
        y_ref[rows, cs] += jnp.dot(parts["u"], w2_ref[:, cs], preferred_element_type=F32)

    pieces = [functools.partial(up, 0), functools.partial(up, 1)]
    pieces += [functools.partial(down, k) for k in range(D_MODEL // FFN_OUT_CHUNK)]
    if with_sample:
        step = (pl.program_id(0) * n_f + f) * n_r + r
        units = _sample_units(step // SAMPLE_CHUNKS, step % SAMPLE_CHUNKS, *sample_in, *sample_out)
        units[0][0]()
        for k, piece in enumerate(pieces):
            if k + 1 < len(units):
                units[k + 1][0]()
            piece()
            units[k][1]()
        for k in range(len(pieces) + 1, len(units)):
            units[k][0]()
        for k in range(len(pieces), len(units)):
            units[k][1]()
    else:
        for piece in pieces:
            piece()

    @pl.when((f == n_f - 1) & (r == n_r - 1))
    def _():
        def ln_rows(k, carry):
            rs = pl.ds(pl.multiple_of(k * LN_ROWS, LN_ROWS), LN_ROWS)
            z = ALPHA * h_ref[rs, :] + y_ref[rs, :] + b2_ref[...]
            y_ref[rs, :] = _layer_norm(z, lng_ref[...], lnb_ref[...])
            return carry
        lax.fori_loop(0, h_ref.shape[0] // LN_ROWS, ln_rows, 0)


def _ffn(h, w1, b1, w2, b2, ln_g, ln_b, tm, tf, row_split, sample_args=None):
    m = h.shape[0]
    grid = (m // tm, D_FF // tf, row_split)
    const = lambda a: pl.BlockSpec(a.shape, lambda i, f, r: (0, 0))
    once = pl.Buffered(1) if row_split > 1 else None
    in_specs = [pl.BlockSpec((tm, D_MODEL), lambda i, f, r: (i, 0), pipeline_mode=once),
                pl.BlockSpec((D_MODEL, tf), lambda i, f, r: (0, f)),
                pl.BlockSpec((1, tf), lambda i, f, r: (0, f)),
                pl.BlockSpec((tf, D_MODEL), lambda i, f, r: (f, 0)),
                const(b2), const(ln_g), const(ln_b)]
    out_specs = [pl.BlockSpec((tm, D_MODEL), lambda i, f, r: (i, 0), pipeline_mode=once)]
    out_shape = [jax.ShapeDtypeStruct((m, D_MODEL), F32)]
    args = [h, w1, b1, w2, b2, ln_g, ln_b]
    if sample_args is not None:
        sinks, qa, qb, new_a, new_bs, cache_a, cache_bs = sample_args
        nt = qa.shape[0]
        assert grid[0] * grid[1] * grid[2] == (nt // T_NEW) * SAMPLE_CHUNKS

        def idx(i, f, r):
            step = (i * grid[1] + f) * grid[2] + r
            return step // SAMPLE_CHUNKS, step % SAMPLE_CHUNKS

        s_in, s_out, s_shape = _sample_specs(idx, nt, cache_a, cache_bs)
        in_specs += s_in
        out_specs += s_out
        out_shape += s_shape
        args += [sinks, qa, qb, new_a, *new_bs, cache_a, *cache_bs]
    out = pl.pallas_call(
        functools.partial(_ffn_kernel, with_sample=sample_args is not None, row_split=row_split),
        grid=grid, in_specs=in_specs, out_specs=out_specs, out_shape=out_shape,
        scratch_shapes=[pltpu.VMEM((tm, D_MODEL), BF16)],
        compiler_params=_cparams(("arbitrary", "arbitrary", "arbitrary")),
        name="ffn_sample" if sample_args is not None else "ffn",
    )(*args)
    return out if sample_args is not None else out[0]


def _to_cache_layout(kvt, heads):
    n, _, _, rows = kvt.shape
    return jnp.transpose(kvt.reshape(n, 2, heads, HEAD_DIM, rows), (0, 4, 1, 2, 3))[None]


def _from_cache_layout(cache):
    _, n, rows, _, heads, _ = cache.shape
    return jnp.transpose(cache[0], (0, 2, 3, 4, 1)).reshape(n, 2, heads * HEAD_DIM, rows)


def kernel(x_prompt, x_sample, cache_a_kv, cache_b1_kv, cache_b2_kv, cache_b3_kv, w_in, a_sinks,
           w_branch_a, w_branch_b, w_out, ln1_g, ln1_b, w_ff1, b_ff1, w_ff2, b_ff2, ln2_g, ln2_b):
    n_p, seq, _ = x_prompt.shape
    n_s, t_new, _ = x_sample.shape
    assert t_new == T_NEW and w_in.shape[0] == 1
    b_caches = (cache_b1_kv, cache_b2_kv, cache_b3_kv)

    w_cat, w_g = _reorder_w_in(w_in[0])
    wa, wb, wo = w_branch_a[0].astype(BF16), w_branch_b[0].astype(BF16), w_out[0].astype(BF16)
    w1, w2 = w_ff1[0].astype(BF16), w_ff2[0].astype(BF16)
    sinks = a_sinks[0].reshape(A_Q_HEADS).astype(F32)
    row = lambda v: v[0].reshape(1, -1)

    def finish(x2d, oa, ob, gates, sample_args=None):
        h = _mix(x2d, oa, ob, gates, wa, wb, wo, row(ln1_g), row(ln1_b), tm=512)
        tm, row_split = (1024, 2) if sample_args is not None else (512, 1)
        return _ffn(h, w1, row(b_ff1), w2, row(b_ff2), row(ln2_g), row(ln2_b), tm, 512, row_split, sample_args)

    xp = x_prompt.reshape(n_p * seq, D_MODEL)
    cos_p, sin_p = _rope_tables(jnp.tile(jnp.arange(seq), n_p))
    (qa, kva, kvt_a, qb, kvb0, kvt0, kvb1, kvt1, kvb2, kvt2) = _proj(
        xp, w_cat, cos_p, sin_p, BF16, 512, n_p, seq)
    gates = _gates(xp, w_g, tm=1024)
    kvbs, kvts = (kvb0, kvb1, kvb2), (kvt0, kvt1, kvt2)
    nt = n_s * T_NEW
    xs = x_sample.reshape(nt, D_MODEL)
    cos_s, sin_s = _rope_tables(jnp.tile(PAST_LEN + jnp.arange(T_NEW), n_s))
    (qa_s, _, kvt_a_s, qb_s, _, kvt0_s, _, kvt1_s, _, kvt2_s) = _proj(
        xs, w_cat, cos_s, sin_s, F32, 512, 1, nt)
    gates_s = _gates(xs, w_g, tm=nt)

    o_a = _attn_a_prompt(sinks, qa, kva, n_p, seq)
    outs, lses = [], []
    for g, (win, dil) in enumerate(B_PATTERNS):
        o, lse = _attn_b_prompt(qb, kvbs[g], g, dil, n_p, seq)
        outs.append(o)
        lses.append(lse)
    o_b = _combine(outs, lses, tm=1024)
    sample_args = (sinks, qa_s, qb_s, kvt_a_s[0], [kvt0_s[0], kvt1_s[0], kvt2_s[0]],
                   _from_cache_layout(cache_a_kv), [_from_cache_layout(cb) for cb in b_caches])
    y_p, o_a_s, o_b_s, r_a, r_b1, r_b2, r_b3 = finish(xp, o_a, o_b, gates, sample_args)
    y_p = y_p.reshape(n_p, seq, D_MODEL)
    a_kv_p = _to_cache_layout(kvt_a[..., seq - min(A_WINDOW, seq):], A_KV_HEADS)
    b_kv_p = [_to_cache_layout(kvts[g][..., seq - min(win, seq):], B_HEADS)
              for g, (win, dil) in enumerate(B_PATTERNS)]

    y_s = finish(xs, o_a_s, o_b_s, gates_s).reshape(n_s, T_NEW, D_MODEL)
    a_kv_s = _to_cache_layout(r_a, A_KV_HEADS)
    b_kv_s = [_to_cache_layout(r, B_HEADS) for r in (r_b1, r_b2, r_b3)]

    return (y_p, y_s, a_kv_p, a_kv_s,
            b_kv_p[0], b_kv_s[0], b_kv_p[1], b_kv_s[1], b_kv_p[2], b_kv_s[2])
```

```python
import functools

import numpy as np
import jax
import jax.numpy as jnp
from jax import lax
from jax.experimental import pallas as pl
from jax.experimental.pallas import tpu as pltpu

D_MODEL = 2048
HEAD_DIM = 64
HALF = HEAD_DIM // 2
A_Q_HEADS = 16
A_KV_HEADS = 2
A_GROUP = A_Q_HEADS // A_KV_HEADS
A_WINDOW = 128
B_PATTERNS = ((128, 1), (512, 4), (2048, 16))
N_B_GROUPS = len(B_PATTERNS)
B_HEADS = 8
B_KEYS = 128
BLOCK = 128
D_FF = 4 * D_MODEL
ROPE_THETA = 10000.0
ALPHA = 2.0 ** 0.25
LN_EPS = 1e-5
NEG = -1e30
PAST_LEN = 8192

A_Q_W = A_Q_HEADS * HEAD_DIM
A_KV_W = A_KV_HEADS * HEAD_DIM
B_W = N_B_GROUPS * B_HEADS * HEAD_DIM
B_OUT_W = B_HEADS * HEAD_DIM
LANES = 128
B_PAIRS = B_OUT_W // LANES
VMEM_LIMIT = 60 * 1024 * 1024

F32 = jnp.float32
BF16 = jnp.bfloat16
NT_DIMS = (((1,), (1,)), ((), ()))


def _cparams(sem):
    return pltpu.CompilerParams(dimension_semantics=sem, vmem_limit_bytes=VMEM_LIMIT)


PROJ_TN = 512
PROJ_SUB = 256
_T_QA, _T_KVA, _T_QB, _T_KVB = 0, 2, 3, 6
PROJ_TILES = 12


def _reorder_w_in(w_in):
    a_end = A_Q_W + 2 * A_KV_W
    qkv_end = a_end + 3 * B_W
    pad = jnp.zeros((w_in.shape[0], PROJ_TN - 2 * A_KV_W), BF16)
    w_qkv = jnp.concatenate([w_in[:, :a_end].astype(BF16), pad, w_in[:, a_end:qkv_end].astype(BF16)], axis=1)
    return w_qkv, w_in[:, qkv_end:].astype(BF16)


def _rope_tables(pos):
    inv = (np.float32(1.0) / np.power(np.float32(ROPE_THETA), np.arange(HALF, dtype=np.float32) / np.float32(HALF)))
    ang = (np.asarray(pos, np.float32)[:, None] * inv[None, :]).astype(np.float32)
    cos, sin = np.cos(ang).astype(np.float32), np.sin(ang).astype(np.float32)
    cosf = np.concatenate([cos, cos, cos, cos], axis=1)
    sinf = np.concatenate([-sin, sin, -sin, sin], axis=1)
    return jnp.asarray(cosf), jnp.asarray(sinf)


def _rope_chunk(t, cos, sin, lo):
    sw = jnp.where(lo, pltpu.roll(t, LANES - HALF, 1), pltpu.roll(t, HALF, 1))
    return t * cos + sw * sin


def _proj_kernel(x_ref, w_ref, cos_ref, sin_ref,
                 qa_ref, kva_ref, kvta_ref, qb_ref,
                 kvb0_ref, kvt0_ref, kvb1_ref, kvt1_ref, kvb2_ref, kvt2_ref, xb_ref):
    j = pl.program_id(1)

    @pl.when(j == 0)
    def _():
        xb_ref[...] = x_ref[...].astype(BF16)

    lane = lax.broadcasted_iota(jnp.int32, (1, LANES), 1)
    lo = (lane % HEAD_DIM) < HALF
    scale = HEAD_DIM ** -0.5

    def tile(n_halves=PROJ_TN // PROJ_SUB):
        xb = xb_ref[...]
        return [jnp.dot(xb, w_ref[:, h * PROJ_SUB:(h + 1) * PROJ_SUB], preferred_element_type=F32)
                for h in range(n_halves)]

    def chunk(acc, c, rope, mul=1.0):
        per = PROJ_SUB // LANES
        t = acc[c // per][:, (c % per) * LANES:(c % per + 1) * LANES]
        if rope:
            t = _rope_chunk(t, cos_ref[...], sin_ref[...], lo)
        return t if mul == 1.0 else t * mul

    @pl.when(j < _T_KVA)
    def _():
        acc = tile()
        for c in range(PROJ_TN // LANES):
            qa_ref[:, c * LANES:(c + 1) * LANES] = chunk(acc, c, True, scale).astype(qa_ref.dtype)

    @pl.when(j == _T_KVA)
    def _():
        acc = tile(1)
        k, v = chunk(acc, 0, True), chunk(acc, 1, False)
        kva_ref[:, 0:LANES] = k
        kva_ref[:, LANES:2 * LANES] = v
        kvta_ref[0] = k.T
        kvta_ref[1] = v.T

    @pl.when((j >= _T_QB) & (j < _T_KVB))
    def _():
        acc = tile()
        for c in range(B_PAIRS):
            qb_ref[c] = chunk(acc, c, True, scale)

    for g, (slab_ref, t_ref) in enumerate(((kvb0_ref, kvt0_ref), (kvb1_ref, kvt1_ref), (kvb2_ref, kvt2_ref))):
        for is_v in (0, 1):
            @pl.when(j == _T_KVB + N_B_GROUPS * is_v + g)
            def _(slab_ref=slab_ref, t_ref=t_ref, is_v=is_v):
                acc = tile()
                for c in range(B_PAIRS):
                    t = chunk(acc, c, not is_v)
                    slab_ref[c] = t
                    t_ref[c * LANES:(c + 1) * LANES, :] = t.T


def _proj(x2d, w_cat, cosf, sinf, q_dtype, tm, n_seq, seq):
    m = x2d.shape[0]
    assert m == n_seq * seq and seq % tm == 0
    per_seq = seq // tm
    grid = (m // tm, PROJ_TILES)

    def clamp(j, lo, n):
        return jnp.clip(j - lo, 0, n - 1)

    in_specs = [
        pl.BlockSpec((tm, D_MODEL), lambda i, j: (i, 0)),
        pl.BlockSpec((D_MODEL, PROJ_TN), lambda i, j: (0, j)),
        pl.BlockSpec((tm, LANES), lambda i, j: (i % per_seq, 0)),
        pl.BlockSpec((tm, LANES), lambda i, j: (i % per_seq, 0)),
    ]
    out_specs = [
        pl.BlockSpec((tm, PROJ_TN), lambda i, j: (i, clamp(j, _T_QA, 2))),
        pl.BlockSpec((tm, 2 * A_KV_W), lambda i, j: (i, 0)),
        pl.BlockSpec((None, 2, A_KV_W, tm), lambda i, j: (i // per_seq, 0, 0, i % per_seq)),
        pl.BlockSpec((B_PAIRS, tm, LANES), lambda i, j: (clamp(j, _T_QB, N_B_GROUPS), i, 0)),
    ]
    out_shape = [
        jax.ShapeDtypeStruct((m, A_Q_W), q_dtype),
        jax.ShapeDtypeStruct((m, 2 * A_KV_W), F32),
        jax.ShapeDtypeStruct((n_seq, 2, A_KV_W, seq), F32),
        jax.ShapeDtypeStruct((N_B_GROUPS * B_PAIRS, m, LANES), F32),
    ]
    for g in range(N_B_GROUPS):
        is_v = lambda j, g=g: (j >= _T_KVB + N_B_GROUPS + g).astype(jnp.int32)
        out_specs.append(pl.BlockSpec((B_PAIRS, tm, LANES), lambda i, j, is_v=is_v: (is_v(j), i, 0)))
        out_specs.append(pl.BlockSpec((None, None, B_OUT_W, tm),
                                      lambda i, j, is_v=is_v: (i // per_seq, is_v(j), 0, i % per_seq)))
        out_shape.append(jax.ShapeDtypeStruct((2 * B_PAIRS, m, LANES), F32))
        out_shape.append(jax.ShapeDtypeStruct((n_seq, 2, B_OUT_W, seq), F32))
    return pl.pallas_call(
        _proj_kernel, grid=grid, in_specs=in_specs, out_specs=out_specs, out_shape=out_shape,
        scratch_shapes=[pltpu.VMEM((tm, D_MODEL), BF16)],
        compiler_params=_cparams(("parallel", "arbitrary")), name="proj",
    )(x2d, w_cat, cosf, sinf)


GATE_TN = 512


def _gates_kernel(*refs):
    n_cast = (len(refs) - 4) // 2
    x_ref, w_ref = refs[:2]
    g_ref, xb_ref = refs[2 + n_cast], refs[-1]

    @pl.when(pl.program_id(1) == 0)
    def _():
        xb_ref[...] = x_ref[...].astype(BF16)

    g_ref[...] = jnp.dot(xb_ref[...], w_ref[...], preferred_element_type=F32).astype(g_ref.dtype)
    for src, dst in zip(refs[2:2 + n_cast], refs[3 + n_cast:3 + 2 * n_cast]):
        dst[...] = src[...].astype(BF16)


def _gates(x2d, w_g, tm, cast=()):
    m = x2d.shape[0]
    n = w_g.shape[1]
    grid = (m // tm, n // GATE_TN)
    steps = grid[0] * grid[1]
    slab = lambda a: pl.BlockSpec((a.shape[0] // steps, a.shape[1]), lambda i, j: (i * grid[1] + j, 0))
    assert all(a.shape[0] % (16 * steps) == 0 for a in cast)
    out = pl.pallas_call(
        _gates_kernel, grid=grid,
        in_specs=[pl.BlockSpec((tm, D_MODEL), lambda i, j: (i, 0)),
                  pl.BlockSpec((D_MODEL, GATE_TN), lambda i, j: (0, j))] + [slab(a) for a in cast],
        out_specs=[pl.BlockSpec((tm, GATE_TN), lambda i, j: (i, j))] + [slab(a) for a in cast],
        out_shape=[jax.ShapeDtypeStruct((m, n), BF16)] + [jax.ShapeDtypeStruct(a.shape, BF16) for a in cast],
        scratch_shapes=[pltpu.VMEM((tm, D_MODEL), BF16)],
        compiler_params=_cparams(("arbitrary", "arbitrary")), name="gates",
    )(x2d, w_g, *cast)
    return out if cast else out[0]


def _band_mask(first_block, rows):
    qi = lax.broadcasted_iota(jnp.int32, (rows, 2 * BLOCK), 0) % BLOCK
    sj = lax.broadcasted_iota(jnp.int32, (rows, 2 * BLOCK), 1)
    dist = BLOCK + qi - sj
    lo = jnp.where(first_block, BLOCK, 0)
    return (dist >= 0) & (dist < B_KEYS) & (sj >= lo)


def _softmax_rows(s, valid, sink=None):
    s = jnp.where(valid, s, NEG)
    m = jnp.max(s, axis=-1, keepdims=True)
    if sink is not None:
        m = jnp.maximum(m, sink)
    p = jnp.exp(s - m)
    l = jnp.sum(p, axis=-1, keepdims=True)
    if sink is not None:
        l = l + jnp.exp(sink - m)
    return p, l, m


def _stack_halves(q128, lo):
    return jnp.concatenate([jnp.where(lo, q128, 0.0), jnp.where(lo, 0.0, q128)], axis=0)


def _attn_b_kernel(*refs, dil, has_prev):
    if has_prev:
        q_ref, kvc_ref, kvp_ref, o_ref, lse_ref = refs
    else:
        q_ref, kvc_ref, o_ref, lse_ref = refs
    lane = lax.broadcasted_iota(jnp.int32, (1, LANES), 1)
    lo = lane < HEAD_DIM
    if has_prev:
        valid = _band_mask(pl.program_id(1) == 0, 2 * BLOCK)
    else:
        qi = lax.broadcasted_iota(jnp.int32, (2 * BLOCK, BLOCK), 0) % BLOCK
        sj = lax.broadcasted_iota(jnp.int32, (2 * BLOCK, BLOCK), 1)
        valid = qi >= sj

    def residue(r):
        rows = pl.ds(r, BLOCK, stride=dil) if dil > 1 else pl.ds(0, BLOCK)
        for p in range(B_PAIRS):
            qs = _stack_halves(q_ref[p, rows, :], lo).astype(BF16)
            k2, v2 = kvc_ref[p, rows, :], kvc_ref[B_PAIRS + p, rows, :]
            if has_prev:
                k2 = jnp.concatenate([kvp_ref[p, rows, :], k2], axis=0)
                v2 = jnp.concatenate([kvp_ref[B_PAIRS + p, rows, :], v2], axis=0)
            s = lax.dot_general(qs, k2.astype(BF16), NT_DIMS, preferred_element_type=F32)
            pr, l, m = _softmax_rows(s, valid)
            o = jnp.dot(pr.astype(BF16), v2.astype(BF16), preferred_element_type=F32) * (1.0 / l)
            lse = jnp.broadcast_to(m + jnp.log(l), o.shape)
            o_ref[p, rows, :] = jnp.where(lo, o[0:BLOCK], o[BLOCK:2 * BLOCK])
            lse_ref[p, rows, :] = jnp.where(lo, lse[0:BLOCK], lse[BLOCK:2 * BLOCK])

    if dil == 1:
        residue(0)
    else:
        def body(r, c):
            residue(r)
            return c
        lax.fori_loop(0, dil, body, 0, unroll=4)


def _attn_b_prompt(qb, kvb, g, dil, n_seq, seq):
    t = qb.shape[1]
    rows = BLOCK * dil
    nb = seq // rows
    qv = qb.reshape(qb.shape[0], n_seq, seq, LANES)
    kvv = kvb.reshape(kvb.shape[0], n_seq, seq, LANES)
    in_specs = [pl.BlockSpec((B_PAIRS, None, rows, LANES), lambda n, b: (g, n, b, 0)),
                pl.BlockSpec((2 * B_PAIRS, None, rows, LANES), lambda n, b: (0, n, b, 0))]
    args = [qv, kvv]
    if nb > 1:
        in_specs.append(pl.BlockSpec((2 * B_PAIRS, None, rows, LANES), lambda n, b: (0, n, jnp.maximum(b - 1, 0), 0)))
        args.append(kvv)
    out_spec = pl.BlockSpec((B_PAIRS, None, rows, LANES), lambda n, b: (0, n, b, 0))
    o, lse = pl.pallas_call(
        functools.partial(_attn_b_kernel, dil=dil, has_prev=nb > 1),
        grid=(n_seq, nb), in_specs=in_specs, out_specs=[out_spec, out_spec],
        out_shape=[jax.ShapeDtypeStruct((B_PAIRS, n_seq, seq, LANES), F32)] * 2,
        compiler_params=_cparams(("parallel", "arbitrary")), name=f"attn_b{g}",
    )(*args)
    return o.reshape(B_PAIRS, t, LANES), lse.reshape(B_PAIRS, t, LANES)


def _dup_heads(x128, lo):
    xr = pltpu.roll(x128, HEAD_DIM, 1)
    return jnp.where(lo, x128, xr), jnp.where(lo, xr, x128)


def _attn_a_kernel(sink_ref, q_ref, kvc_ref, kvp_ref, o_ref):
    valid = _band_mask(pl.program_id(1) == 0, BLOCK)
    kv2 = jnp.concatenate([kvp_ref[...], kvc_ref[...]], axis=0)
    lane = lax.broadcasted_iota(jnp.int32, (1, LANES), 1)
    lo = lane < HEAD_DIM
    kdup = [t.astype(BF16) for t in _dup_heads(kv2[:, 0:LANES], lo)]
    vdup = [t.astype(BF16) for t in _dup_heads(kv2[:, LANES:2 * LANES], lo)]
    for p in range(A_Q_HEADS // 2):
        kh = (2 * p) // A_GROUP
        q128 = q_ref[:, p * LANES:(p + 1) * LANES]
        halves = []
        for e in range(2):
            qm = jnp.where(lo if e == 0 else jnp.logical_not(lo), q128, jnp.zeros_like(q128))
            s = lax.dot_general(qm, kdup[kh], NT_DIMS, preferred_element_type=F32)
            pe, l, _ = _softmax_rows(s, valid, sink_ref[2 * p + e])
            o = jnp.dot(pe.astype(BF16), vdup[kh], preferred_element_type=F32)
            halves.append(o * (1.0 / l))
        o_ref[:, p * LANES:(p + 1) * LANES] = jnp.where(lo, halves[0], halves[1]).astype(o_ref.dtype)


def _attn_a_prompt(sinks, qa, kva, n_seq, seq):
    t = qa.shape[0]
    nb = seq // BLOCK
    qv = qa.reshape(n_seq, seq, A_Q_W)
    kvv = kva.reshape(n_seq, seq, 2 * A_KV_W)
    in_specs = [
        pl.BlockSpec(memory_space=pltpu.SMEM),
        pl.BlockSpec((None, BLOCK, A_Q_W), lambda n, b: (n, b, 0)),
        pl.BlockSpec((None, BLOCK, 2 * A_KV_W), lambda n, b: (n, b, 0)),
        pl.BlockSpec((None, BLOCK, 2 * A_KV_W), lambda n, b: (n, jnp.maximum(b - 1, 0), 0)),
    ]
    o = pl.pallas_call(
        _attn_a_kernel, grid=(n_seq, nb), in_specs=in_specs,
        out_specs=pl.BlockSpec((None, BLOCK, A_Q_W), lambda n, b: (n, b, 0)),
        out_shape=jax.ShapeDtypeStruct((n_seq, seq, A_Q_W), BF16),
        compiler_params=_cparams(("parallel", "arbitrary")), name="attn_a",
    )(sinks, qv, kvv, kvv)
    return o.reshape(t, A_Q_W)


def _combine3(outs, lses):
    m = jnp.maximum(jnp.maximum(lses[0], lses[1]), lses[2])
    es = [jnp.exp(x - m) for x in lses]
    num = es[0] * outs[0] + es[1] * outs[1] + es[2] * outs[2]
    return num / (es[0] + es[1] + es[2])


def _combine_kernel(o0_ref, o1_ref, o2_ref, l0_ref, l1_ref, l2_ref, ob_ref):
    for p in range(B_PAIRS):
        ob_ref[:, p * LANES:(p + 1) * LANES] = _combine3(
            [o0_ref[p], o1_ref[p], o2_ref[p]], [l0_ref[p], l1_ref[p], l2_ref[p]]).astype(ob_ref.dtype)


def _combine(outs, lses, tm):
    t = outs[0].shape[1]
    spec = pl.BlockSpec((B_PAIRS, tm, LANES), lambda i: (0, i, 0))
    return pl.pallas_call(
        _combine_kernel, grid=(t // tm,), in_specs=[spec] * 6,
        out_specs=pl.BlockSpec((tm, B_OUT_W), lambda i: (i, 0)),
        out_shape=jax.ShapeDtypeStruct((t, B_OUT_W), BF16),
        compiler_params=_cparams(("parallel",)), name="combine",
    )(*outs, *lses)


T_NEW = 8
SEQS_PER_LANE_BLOCK = LANES // T_NEW
S_PAIRS = 2
SAMPLE_CHUNKS = B_PAIRS // S_PAIRS
ROLL_ROWS = 64


def _roll_cache(c_ref, new_rolled, o_ref, kv, row0, nrows):
    lb = c_ref.shape[-1]
    lane = lax.broadcasted_iota(jnp.int32, (1, LANES), 1)
    tail = lane >= LANES - T_NEW
    pieces = []
    for r0 in range(0, nrows, ROLL_ROWS):
        rs = slice(row0 + r0, row0 + r0 + ROLL_ROWS)
        x = c_ref[kv, rs, :]
        pieces.append(x.astype(BF16))
        t = pltpu.roll(x, lb - T_NEW, 1)
        if lb > LANES:
            o_ref[kv, rs, 0:lb - LANES] = t[:, 0:lb - LANES]
        o_ref[kv, rs, lb - LANES:lb] = jnp.where(tail, new_rolled[r0:r0 + ROLL_ROWS, :], t[:, lb - LANES:lb])
    return jnp.concatenate(pieces, axis=0)


def _sample_scores(qs, kt, nk, dil, lb):
    rows = qs.shape[0]
    s_c = jnp.dot(qs, kt, preferred_element_type=F32)
    s_n = jnp.dot(qs, nk.astype(BF16), preferred_element_type=F32)
    tok = lax.broadcasted_iota(jnp.int32, (rows, 1), 0) % T_NEW
    pos = lax.broadcasted_iota(jnp.int32, (rows, lb), 1)
    d_c = lb + tok - pos
    valid_c = ((d_c & (dil - 1)) == 0) & (d_c <= (B_KEYS - 1) * dil)
    lane = lax.broadcasted_iota(jnp.int32, (rows, LANES), 1)
    d_n = tok - (lane - (LANES - T_NEW))
    valid_n = (lane >= LANES - T_NEW) & (d_n >= 0) & ((d_n & (dil - 1)) == 0)
    return jnp.where(valid_c, s_c, NEG), jnp.where(valid_n, s_n, NEG)


def _sample_probs(qs, kt, nk, dil, sink=None):
    s_c, s_n = _sample_scores(qs, kt, nk, dil, kt.shape[-1])
    m = jnp.maximum(jnp.max(s_c, axis=-1, keepdims=True), jnp.max(s_n, axis=-1, keepdims=True))
    if sink is not None:
        m = jnp.maximum(m, sink)
    p_c, p_n = jnp.exp(s_c - m), jnp.exp(s_n - m)
    l = jnp.sum(p_c, axis=-1, keepdims=True) + jnp.sum(p_n, axis=-1, keepdims=True)
    if sink is not None:
        l = l + jnp.exp(sink - m)
    return p_c.astype(BF16), p_n.astype(BF16), l, m


def _sample_values(p_c, p_n, l, m, vt, nv):
    o = lax.dot_general(p_c, vt, NT_DIMS, preferred_element_type=F32)
    o = o + lax.dot_general(p_n, nv.astype(BF16), NT_DIMS, preferred_element_type=F32)
    return o * (1.0 / l), m + jnp.log(l)


def _sample_units(n, c, sink_ref, qa_ref, qb_ref, na_ref, n0_ref, n1_ref, n2_ref,
                  ca_ref, c0_ref, c1_ref, c2_ref,
                  oa_ref, ob_ref, ra_ref, r0_ref, r1_ref, r2_ref):
    shift = (LANES - T_NEW) - (n % SEQS_PER_LANE_BLOCK) * T_NEW
    lane = lax.broadcasted_iota(jnp.int32, (1, LANES), 1)
    lo = lane < HEAD_DIM
    pending, done = {}, {}
    units = []

    groups = ((c0_ref, n0_ref, r0_ref), (c1_ref, n1_ref, r1_ref), (c2_ref, n2_ref, r2_ref))
    for lp in range(S_PAIRS):
        for g, (c_ref, n_ref, r_ref) in enumerate(groups):
            def first(lp=lp, g=g, c_ref=c_ref, n_ref=n_ref, r_ref=r_ref):
                nrows = pl.ds(pl.multiple_of((c * S_PAIRS + lp) * LANES, LANES), LANES)
                nk = pltpu.roll(n_ref[0, nrows, :], shift, 1)
                nv = pltpu.roll(n_ref[1, nrows, :], shift, 1)
                qs = _stack_halves(qb_ref[g * B_PAIRS + c * S_PAIRS + lp], lo).astype(BF16)
                kt = _roll_cache(c_ref, nk, r_ref, 0, lp * LANES, LANES)
                vt = _roll_cache(c_ref, nv, r_ref, 1, lp * LANES, LANES)
                pending[lp, g] = _sample_probs(qs, kt, nk, B_PATTERNS[g][1]) + (vt, nv)

            def second(lp=lp, g=g):
                o, lse = _sample_values(*pending.pop((lp, g)))
                lse = jnp.broadcast_to(lse, o.shape)
                done[lp, g] = (jnp.where(lo, o[0:T_NEW], o[T_NEW:2 * T_NEW]),
                               jnp.where(lo, lse[0:T_NEW], lse[T_NEW:2 * T_NEW]))
                if g == N_B_GROUPS - 1:
                    outs, lses = zip(*[done.pop((lp, k)) for k in range(N_B_GROUPS)])
                    ob_ref[:, lp * LANES:(lp + 1) * LANES] = _combine3(outs, lses)

            units.append((first, second))

    hrow = lax.broadcasted_iota(jnp.int32, (A_GROUP * T_NEW, 1), 0) // T_NEW
    for kh in range(A_KV_HEADS):
        def first(kh=kh):
            if kh == 0:
                nk = pltpu.roll(na_ref[0], shift, 1)
                nv = pltpu.roll(na_ref[1], shift, 1)
                pending["a"] = (nk, nv, _roll_cache(ca_ref, nk, ra_ref, 0, 0, A_KV_W),
                                _roll_cache(ca_ref, nv, ra_ref, 1, 0, A_KV_W))
            nk, nv, kt_a, vt_a = pending["a"]
            hs = slice(kh * HEAD_DIM, (kh + 1) * HEAD_DIM)
            dup = lambda x: jnp.concatenate([x[hs], x[hs]], axis=0)
            tiles = [_stack_halves(qa_ref[:, (kh * (A_GROUP // 2) + p) * LANES:(kh * (A_GROUP // 2) + p + 1) * LANES], lo)
                     for p in range(A_GROUP // 2)]
            qs = jnp.concatenate(tiles, axis=0).astype(BF16)
            sink = jnp.zeros((A_GROUP * T_NEW, 1), F32)
            for hh in range(A_GROUP):
                sink = jnp.where(hrow == hh, sink_ref[kh * A_GROUP + hh], sink)
            pending["a", kh] = _sample_probs(qs, dup(kt_a), dup(nk), 1, sink) + (dup(vt_a), dup(nv))

        def second(kh=kh):
            o, _ = _sample_values(*pending.pop(("a", kh)))
            for p in range(A_GROUP // 2):
                o0 = o[(2 * p) * T_NEW:(2 * p + 1) * T_NEW, :]
                o1 = o[(2 * p + 1) * T_NEW:(2 * p + 2) * T_NEW, :]
                c0 = (kh * (A_GROUP // 2) + p) * LANES
                oa_ref[:, c0:c0 + LANES] = jnp.where(lo, o0, o1)

        units.append((first, second))
    return units


N_SAMPLE_IN, N_SAMPLE_OUT = 11, 6


def _sample_specs(idx, nt, cache_a, cache_bs):
    rows = S_PAIRS * LANES
    seq = lambda *g: idx(*g)[0]
    chunk = lambda *g: idx(*g)[1]
    lane_blk = lambda *g: idx(*g)[0] // SEQS_PER_LANE_BLOCK
    in_specs = [
        pl.BlockSpec(memory_space=pltpu.SMEM),
        pl.BlockSpec((T_NEW, A_Q_W), lambda *g: (seq(*g), 0)),
        pl.BlockSpec((N_B_GROUPS * B_PAIRS, T_NEW, LANES), lambda *g: (0, seq(*g), 0)),
        pl.BlockSpec((2, A_KV_W, LANES), lambda *g: (0, 0, lane_blk(*g))),
    ]
    in_specs += [pl.BlockSpec((2, B_OUT_W, LANES), lambda *g: (0, 0, lane_blk(*g)))] * N_B_GROUPS
    cache_specs = [pl.BlockSpec((None, 2, A_KV_W, cache_a.shape[-1]), lambda *g: (seq(*g), 0, 0, 0))]
    cache_specs += [pl.BlockSpec((None, 2, rows, cb.shape[-1]), lambda *g: (seq(*g), 0, chunk(*g), 0))
                    for cb in cache_bs]
    out_specs = [pl.BlockSpec((T_NEW, A_Q_W), lambda *g: (seq(*g), 0)),
                 pl.BlockSpec((T_NEW, rows), lambda *g: (seq(*g), chunk(*g)))] + cache_specs
    out_shape = [jax.ShapeDtypeStruct((nt, A_Q_W), F32), jax.ShapeDtypeStruct((nt, B_OUT_W), F32),
                 jax.ShapeDtypeStruct(cache_a.shape, F32)] + [jax.ShapeDtypeStruct(cb.shape, F32) for cb in cache_bs]
    return in_specs + cache_specs, out_specs, out_shape


def _layer_norm(z, g, b):
    mu = jnp.mean(z, axis=-1, keepdims=True)
    zc = z - mu
    var = jnp.mean(zc * zc, axis=-1, keepdims=True)
    return zc * lax.rsqrt(var + LN_EPS) * g + b


def _mix_kernel(x_ref, oa_ref, ob_ref, g_ref, wa_ref, wb_ref, wo_ref, lng_ref, lnb_ref, h_ref):
    br_a = jnp.dot(oa_ref[...].astype(BF16), wa_ref[...], preferred_element_type=F32)
    br_b = jnp.dot(ob_ref[...].astype(BF16), wb_ref[...], preferred_element_type=F32)
    ga = jax.nn.sigmoid(g_ref[:, 0:D_MODEL].astype(F32))
    gb = jax.nn.sigmoid(g_ref[:, D_MODEL:2 * D_MODEL].astype(F32))
    merged = (ga * br_a + gb * br_b).astype(BF16)
    mix = jnp.dot(merged, wo_ref[...], preferred_element_type=F32)
    h_ref[...] = _layer_norm(ALPHA * x_ref[...] + mix, lng_ref[...], lnb_ref[...])


def _mix(x2d, oa, ob, gates, wa, wb, wo, ln_g, ln_b, tm):
    m = x2d.shape[0]
    row = lambda w: pl.BlockSpec((tm, w), lambda i: (i, 0))
    const = lambda a: pl.BlockSpec(a.shape, lambda i: (0, 0), pipeline_mode=pl.Buffered(1))
    return pl.pallas_call(
        _mix_kernel, grid=(m // tm,),
        in_specs=[row(D_MODEL), row(A_Q_W), row(B_OUT_W), row(2 * D_MODEL),
                  const(wa), const(wb), const(wo), const(ln_g), const(ln_b)],
        out_specs=row(D_MODEL), out_shape=jax.ShapeDtypeStruct((m, D_MODEL), F32),
        compiler_params=_cparams(("parallel",)), name="mix",
    )(x2d, oa, ob, gates, wa, wb, wo, ln_g, ln_b)


FFN_OUT_CHUNK = 512
LN_ROWS = 128


N_FFN_IN = 7


def _ffn_kernel(*refs, with_sample, row_split):
    h_ref, w1_ref, b1_ref, w2_ref, b2_ref, lng_ref, lnb_ref = refs[:N_FFN_IN]
    if with_sample:
        sample_in = refs[N_FFN_IN:N_FFN_IN + N_SAMPLE_IN]
        y_ref = refs[N_FFN_IN + N_SAMPLE_IN]
        sample_out = refs[N_FFN_IN + N_SAMPLE_IN + 1:N_FFN_IN + N_SAMPLE_IN + 1 + N_SAMPLE_OUT]
    else:
        y_ref = refs[N_FFN_IN]
    hb_ref = refs[-1]
    f, r = pl.program_id(1), pl.program_id(2)
    n_f, n_r = pl.num_programs(1), pl.num_programs(2)
    sub = h_ref.shape[0] // row_split
    tf = w1_ref.shape[1]
    rows = pl.ds(pl.multiple_of(r * sub, sub), sub)

    @pl.when((f == 0) & (r == 0))
    def _():
        hb_ref[...] = h_ref[...].astype(BF16)
        y_ref[...] = jnp.zeros_like(y_ref)

    parts = {}

    def up(k):
        cs = slice(k * (tf // 2), (k + 1) * (tf // 2))
        t = jnp.dot(hb_ref[rows, :], w1_ref[:, cs], preferred_element_type=F32) + b1_ref[:, cs]
        parts[k] = jnp.square(jnp.maximum(t, 0.0)).astype(BF16)

    def down(k):
        if "u" not in parts:
            parts["u"] = jnp.concatenate([parts.pop(0), parts.pop(1)], axis=1)
        cs = slice(k * FFN_OUT_CHUNK, (k + 1) * FFN_OUT_CHUNK)
        y_ref[rows, cs] += jnp.dot(parts["u"], w2_ref[:, cs], preferred_element_type=F32)

    pieces = [functools.partial(up, 0), functools.partial(up, 1)]
    pieces += [functools.partial(down, k) for k in range(D_MODEL // FFN_OUT_CHUNK)]
    if with_sample:
        step = (pl.program_id(0) * n_f + f) * n_r + r
        units = _sample_units(step // SAMPLE_CHUNKS, step % SAMPLE_CHUNKS, *sample_in, *sample_out)
        units[0][0]()
        for k, piece in enumerate(pieces):
            if k + 1 < len(units):
                units[k + 1][0]()
            piece()
            units[k][1]()
        for k in range(len(pieces) + 1, len(units)):
            units[k][0]()
        for k in range(len(pieces), len(units)):
            units[k][1]()
    else:
        for piece in pieces:
            piece()

    @pl.when((f == n_f - 1) & (r == n_r - 1))
    def _():
        def ln_rows(k, carry):
            rs = pl.ds(pl.multiple_of(k * LN_ROWS, LN_ROWS), LN_ROWS)
            z = ALPHA * h_ref[rs, :] + y_ref[rs, :] + b2_ref[...]
            y_ref[rs, :] = _layer_norm(z, lng_ref[...], lnb_ref[...])
            return carry
        lax.fori_loop(0, h_ref.shape[0] // LN_ROWS, ln_rows, 0)


def _ffn(h, w1, b1, w2, b2, ln_g, ln_b, tm, tf, row_split, sample_args=None):
    m = h.shape[0]
    grid = (m // tm, D_FF // tf, row_split)
    const = lambda a: pl.BlockSpec(a.shape, lambda i, f, r: (0, 0))
    once = pl.Buffered(1) if row_split > 1 else None
    in_specs = [pl.BlockSpec((tm, D_MODEL), lambda i, f, r: (i, 0), pipeline_mode=once),
                pl.BlockSpec((D_MODEL, tf), lambda i, f, r: (0, f)),
                pl.BlockSpec((1, tf), lambda i, f, r: (0, f)),
                pl.BlockSpec((tf, D_MODEL), lambda i, f, r: (f, 0)),
                const(b2), const(ln_g), const(ln_b)]
    out_specs = [pl.BlockSpec((tm, D_MODEL), lambda i, f, r: (i, 0), pipeline_mode=once)]
    out_shape = [jax.ShapeDtypeStruct((m, D_MODEL), F32)]
    args = [h, w1, b1, w2, b2, ln_g, ln_b]
    if sample_args is not None:
        sinks, qa, qb, new_a, new_bs, cache_a, cache_bs = sample_args
        nt = qa.shape[0]
        assert grid[0] * grid[1] * grid[2] == (nt // T_NEW) * SAMPLE_CHUNKS

        def idx(i, f, r):
            step = (i * grid[1] + f) * grid[2] + r
            return step // SAMPLE_CHUNKS, step % SAMPLE_CHUNKS

        s_in, s_out, s_shape = _sample_specs(idx, nt, cache_a, cache_bs)
        in_specs += s_in
        out_specs += s_out
        out_shape += s_shape
        args += [sinks, qa, qb, new_a, *new_bs, cache_a, *cache_bs]
    out = pl.pallas_call(
        functools.partial(_ffn_kernel, with_sample=sample_args is not None, row_split=row_split),
        grid=grid, in_specs=in_specs, out_specs=out_specs, out_shape=out_shape,
        scratch_shapes=[pltpu.VMEM((tm, D_MODEL), BF16)],
        compiler_params=_cparams(("arbitrary", "arbitrary", "arbitrary")),
        name="ffn_sample" if sample_args is not None else "ffn",
    )(*args)
    return out if sample_args is not None else out[0]


def _to_cache_layout(kvt, heads):
    n, _, _, rows = kvt.shape
    return jnp.transpose(kvt.reshape(n, 2, heads, HEAD_DIM, rows), (0, 4, 1, 2, 3))[None]


def _from_cache_layout(cache):
    _, n, rows, _, heads, _ = cache.shape
    return jnp.transpose(cache[0], (0, 2, 3, 4, 1)).reshape(n, 2, heads * HEAD_DIM, rows)


def kernel(x_prompt, x_sample, cache_a_kv, cache_b1_kv, cache_b2_kv, cache_b3_kv, w_in, a_sinks,
           w_branch_a, w_branch_b, w_out, ln1_g, ln1_b, w_ff1, b_ff1, w_ff2, b_ff2, ln2_g, ln2_b):
    n_p, seq, _ = x_prompt.shape
    n_s, t_new, _ = x_sample.shape
    assert t_new == T_NEW and w_in.shape[0] == 1
    b_caches = (cache_b1_kv, cache_b2_kv, cache_b3_kv)

    w_cat, w_g = _reorder_w_in(w_in[0])
    wa, wb, wo = w_branch_a[0].astype(BF16), w_branch_b[0].astype(BF16), w_out[0].astype(BF16)
    sinks = a_sinks[0].reshape(A_Q_HEADS).astype(F32)
    row = lambda v: v[0].reshape(1, -1)

    def finish(x2d, oa, ob, gates, sample_args=None):
        h = _mix(x2d, oa, ob, gates, wa, wb, wo, row(ln1_g), row(ln1_b), tm=512)
        tm, row_split = (1024, 2) if sample_args is not None else (1024, 1)
        return _ffn(h, w1, row(b_ff1), w2, row(b_ff2), row(ln2_g), row(ln2_b), tm, 512, row_split, sample_args)

    xp = x_prompt.reshape(n_p * seq, D_MODEL)
    cos_p, sin_p = _rope_tables(np.arange(seq))
    (qa, kva, kvt_a, qb, kvb0, kvt0, kvb1, kvt1, kvb2, kvt2) = _proj(
        xp, w_cat, cos_p, sin_p, BF16, 512, n_p, seq)
    gates, w1, w2 = _gates(xp, w_g, 1024, cast=(w_ff1[0], w_ff2[0]))
    kvbs, kvts = (kvb0, kvb1, kvb2), (kvt0, kvt1, kvt2)
    nt = n_s * T_NEW
    xs = x_sample.reshape(nt, D_MODEL)
    cos_s, sin_s = _rope_tables(PAST_LEN + np.arange(nt) % T_NEW)
    (qa_s, _, kvt_a_s, qb_s, _, kvt0_s, _, kvt1_s, _, kvt2_s) = _proj(
        xs, w_cat, cos_s, sin_s, F32, 512, 1, nt)
    gates_s = _gates(xs, w_g, tm=nt)

    o_a = _attn_a_prompt(sinks, qa, kva, n_p, seq)
    outs, lses = [], []
    for g, (win, dil) in enumerate(B_PATTERNS):
        o, lse = _attn_b_prompt(qb, kvbs[g], g, dil, n_p, seq)
        outs.append(o)
        lses.append(lse)
    o_b = _combine(outs, lses, tm=1024)
    sample_args = (sinks, qa_s, qb_s, kvt_a_s[0], [kvt0_s[0], kvt1_s[0], kvt2_s[0]],
                   _from_cache_layout(cache_a_kv), [_from_cache_layout(cb) for cb in b_caches])
    y_p, o_a_s, o_b_s, r_a, r_b1, r_b2, r_b3 = finish(xp, o_a, o_b, gates, sample_args)
    y_p = y_p.reshape(n_p, seq, D_MODEL)
    a_kv_p = _to_cache_layout(kvt_a[..., seq - min(A_WINDOW, seq):], A_KV_HEADS)
    b_kv_p = [_to_cache_layout(kvts[g][..., seq - min(win, seq):], B_HEADS)
              for g, (win, dil) in enumerate(B_PATTERNS)]

    y_s = finish(xs, o_a_s, o_b_s, gates_s).reshape(n_s, T_NEW, D_MODEL)
    a_kv_s = _to_cache_layout(r_a, A_KV_HEADS)
    b_kv_s = [_to_cache_layout(r, B_HEADS) for r in (r_b1, r_b2, r_b3)]

    return (y_p, y_s, a_kv_p, a_kv_s,
            b_kv_p[0], b_kv_s[0], b_kv_p[1], b_kv_s[1], b_kv_p[2], b_kv_s[2])
```

```python
import functools

import numpy as np
import jax
import jax.numpy as jnp
from jax import lax
from jax.experimental import pallas as pl
from jax.experimental.pallas import tpu as pltpu

D_MODEL = 2048
HEAD_DIM = 64
HALF = HEAD_DIM // 2
A_Q_HEADS = 16
A_KV_HEADS = 2
A_GROUP = A_Q_HEADS // A_KV_HEADS
A_WINDOW = 128
B_PATTERNS = ((128, 1), (512, 4), (2048, 16))
N_B_GROUPS = len(B_PATTERNS)
B_HEADS = 8
B_KEYS = 128
BLOCK = 128
D_FF = 4 * D_MODEL
ROPE_THETA = 10000.0
ALPHA = 2.0 ** 0.25
LN_EPS = 1e-5
NEG = -1e30
PAST_LEN = 8192

A_Q_W = A_Q_HEADS * HEAD_DIM
A_KV_W = A_KV_HEADS * HEAD_DIM
B_W = N_B_GROUPS * B_HEADS * HEAD_DIM
B_OUT_W = B_HEADS * HEAD_DIM
LANES = 128
B_PAIRS = B_OUT_W // LANES
VMEM_LIMIT = 60 * 1024 * 1024

F32 = jnp.float32
BF16 = jnp.bfloat16
NT_DIMS = (((1,), (1,)), ((), ()))


def _cparams(sem):
    return pltpu.CompilerParams(dimension_semantics=sem, vmem_limit_bytes=VMEM_LIMIT)


PROJ_TN = 512
_T_QA, _T_KVA, _T_QB, _T_KVB = 0, 2, 3, 6
PROJ_TILES = 12


def _reorder_w_in(w_in):
    a_end = A_Q_W + 2 * A_KV_W
    qkv_end = a_end + 3 * B_W
    pad = jnp.zeros((w_in.shape[0], PROJ_TN - 2 * A_KV_W), BF16)
    w_qkv = jnp.concatenate([w_in[:, :a_end].astype(BF16), pad, w_in[:, a_end:qkv_end].astype(BF16)], axis=1)
    return w_qkv, w_in[:, qkv_end:].astype(BF16)


def _rope_tables(pos):
    inv = (np.float32(1.0) / np.power(np.float32(ROPE_THETA), np.arange(HALF, dtype=np.float32) / np.float32(HALF)))
    ang = (np.asarray(pos, np.float32)[:, None] * inv[None, :]).astype(np.float32)
    cos, sin = np.cos(ang).astype(np.float32), np.sin(ang).astype(np.float32)
    cosf = np.concatenate([cos, cos, cos, cos], axis=1)
    sinf = np.concatenate([-sin, sin, -sin, sin], axis=1)
    return jnp.asarray(cosf), jnp.asarray(sinf)


def _rope_chunk(t, cos, sin, lo):
    sw = jnp.where(lo, pltpu.roll(t, LANES - HALF, 1), pltpu.roll(t, HALF, 1))
    return t * cos + sw * sin


def _proj_kernel(x_ref, w_ref, cos_ref, sin_ref,
                 qa_ref, kva_ref, kvta_ref, qb_ref,
                 kvb0_ref, kvt0_ref, kvb1_ref, kvt1_ref, kvb2_ref, kvt2_ref, xb_ref, acc_ref):
    j = pl.program_id(1)

    @pl.when(j == 0)
    def _():
        xb_ref[...] = x_ref[...].astype(BF16)

    lane = lax.broadcasted_iota(jnp.int32, (1, LANES), 1)
    lo = (lane % HEAD_DIM) < HALF
    scale = HEAD_DIM ** -0.5
    kv_refs = ((kvb0_ref, kvt0_ref), (kvb1_ref, kvt1_ref), (kvb2_ref, kvt2_ref))

    def multiply(t):
        n_cols = 2 * A_KV_W if t == _T_KVA else PROJ_TN
        acc_ref[t % 2, :, 0:n_cols] = jnp.dot(xb_ref[...], w_ref[:, 0:n_cols], preferred_element_type=F32)

    def finish(t):
        def chunk(c, rope, mul=1.0):
            v = acc_ref[t % 2, :, c * LANES:(c + 1) * LANES]
            if rope:
                v = _rope_chunk(v, cos_ref[...], sin_ref[...], lo)
            return v if mul == 1.0 else v * mul

        if t < _T_KVA:
            for c in range(PROJ_TN // LANES):
                qa_ref[:, c * LANES:(c + 1) * LANES] = chunk(c, True, scale).astype(qa_ref.dtype)
        elif t == _T_KVA:
            k, v = chunk(0, True), chunk(1, False)
            kva_ref[:, 0:LANES] = k
            kva_ref[:, LANES:2 * LANES] = v
            kvta_ref[0] = k.T
            kvta_ref[1] = v.T
        elif t < _T_KVB:
            for c in range(B_PAIRS):
                qb_ref[c] = chunk(c, True, scale)
        else:
            is_v, g = divmod(t - _T_KVB, N_B_GROUPS)
            slab_ref, t_ref = kv_refs[g]
            for c in range(B_PAIRS):
                v = chunk(c, not is_v)
                slab_ref[c] = v
                t_ref[c * LANES:(c + 1) * LANES, :] = v.T

    for t in range(PROJ_TILES + 1):
        @pl.when(j == t)
        def _(t=t):
            if t < PROJ_TILES:
                multiply(t)
            if t > 0:
                finish(t - 1)


def _proj(x2d, w_cat, cosf, sinf, q_dtype, tm, n_seq, seq):
    m = x2d.shape[0]
    assert m == n_seq * seq and seq % tm == 0
    per_seq = seq // tm
    grid = (m // tm, PROJ_TILES + 1)

    def clamp(j, lo, n):
        return jnp.clip(j - lo, 0, n - 1)

    in_specs = [
        pl.BlockSpec((tm, D_MODEL), lambda i, j: (i, 0)),
        pl.BlockSpec((D_MODEL, PROJ_TN), lambda i, j: (0, jnp.minimum(j, PROJ_TILES - 1))),
        pl.BlockSpec((tm, LANES), lambda i, j: (i % per_seq, 0)),
        pl.BlockSpec((tm, LANES), lambda i, j: (i % per_seq, 0)),
    ]
    out_specs = [
        pl.BlockSpec((tm, PROJ_TN), lambda i, j: (i, clamp(j, _T_QA + 1, 2))),
        pl.BlockSpec((tm, 2 * A_KV_W), lambda i, j: (i, 0)),
        pl.BlockSpec((None, 2, A_KV_W, tm), lambda i, j: (i // per_seq, 0, 0, i % per_seq)),
        pl.BlockSpec((B_PAIRS, tm, LANES), lambda i, j: (clamp(j, _T_QB + 1, N_B_GROUPS), i, 0)),
    ]
    out_shape = [
        jax.ShapeDtypeStruct((m, A_Q_W), q_dtype),
        jax.ShapeDtypeStruct((m, 2 * A_KV_W), F32),
        jax.ShapeDtypeStruct((n_seq, 2, A_KV_W, seq), F32),
        jax.ShapeDtypeStruct((N_B_GROUPS * B_PAIRS, m, LANES), F32),
    ]
    for g in range(N_B_GROUPS):
        is_v = lambda j, g=g: (j >= _T_KVB + N_B_GROUPS + g + 1).astype(jnp.int32)
        out_specs.append(pl.BlockSpec((B_PAIRS, tm, LANES), lambda i, j, is_v=is_v: (is_v(j), i, 0)))
        out_specs.append(pl.BlockSpec((None, None, B_OUT_W, tm),
                                      lambda i, j, is_v=is_v: (i // per_seq, is_v(j), 0, i % per_seq)))
        out_shape.append(jax.ShapeDtypeStruct((2 * B_PAIRS, m, LANES), F32))
        out_shape.append(jax.ShapeDtypeStruct((n_seq, 2, B_OUT_W, seq), F32))
    return pl.pallas_call(
        _proj_kernel, grid=grid, in_specs=in_specs, out_specs=out_specs, out_shape=out_shape,
        scratch_shapes=[pltpu.VMEM((tm, D_MODEL), BF16), pltpu.VMEM((2, tm, PROJ_TN), F32)],
        compiler_params=_cparams(("parallel", "arbitrary")), name="proj",
    )(x2d, w_cat, cosf, sinf)


GATE_TN = 512


def _gates_kernel(*refs):
    n_cast = (len(refs) - 4) // 2
    x_ref, w_ref = refs[:2]
    g_ref, xb_ref = refs[2 + n_cast], refs[-1]

    @pl.when(pl.program_id(1) == 0)
    def _():
        xb_ref[...] = x_ref[...].astype(BF16)

    g_ref[...] = jnp.dot(xb_ref[...], w_ref[...], preferred_element_type=F32).astype(g_ref.dtype)
    for src, dst in zip(refs[2:2 + n_cast], refs[3 + n_cast:3 + 2 * n_cast]):
        dst[...] = src[...].astype(BF16)


def _gates(x2d, w_g, tm, cast=()):
    m = x2d.shape[0]
    n = w_g.shape[1]
    grid = (m // tm, n // GATE_TN)
    steps = grid[0] * grid[1]
    slab = lambda a: pl.BlockSpec((a.shape[0] // steps, a.shape[1]), lambda i, j: (i * grid[1] + j, 0))
    assert all(a.shape[0] % (16 * steps) == 0 for a in cast)
    out = pl.pallas_call(
        _gates_kernel, grid=grid,
        in_specs=[pl.BlockSpec((tm, D_MODEL), lambda i, j: (i, 0)),
                  pl.BlockSpec((D_MODEL, GATE_TN), lambda i, j: (0, j))] + [slab(a) for a in cast],
        out_specs=[pl.BlockSpec((tm, GATE_TN), lambda i, j: (i, j))] + [slab(a) for a in cast],
        out_shape=[jax.ShapeDtypeStruct((m, n), BF16)] + [jax.ShapeDtypeStruct(a.shape, BF16) for a in cast],
        scratch_shapes=[pltpu.VMEM((tm, D_MODEL), BF16)],
        compiler_params=_cparams(("arbitrary", "arbitrary")), name="gates",
    )(x2d, w_g, *cast)
    return out if cast else out[0]


def _band_mask(first_block, rows):
    qi = lax.broadcasted_iota(jnp.int32, (rows, 2 * BLOCK), 0) % BLOCK
    sj = lax.broadcasted_iota(jnp.int32, (rows, 2 * BLOCK), 1)
    dist = BLOCK + qi - sj
    lo = jnp.where(first_block, BLOCK, 0)
    return (dist >= 0) & (dist < B_KEYS) & (sj >= lo)


def _softmax_rows(s, valid, sink=None):
    s = jnp.where(valid, s, NEG)
    m = jnp.max(s, axis=-1, keepdims=True)
    if sink is not None:
        m = jnp.maximum(m, sink)
    p = jnp.exp(s - m)
    l = jnp.sum(p, axis=-1, keepdims=True)
    if sink is not None:
        l = l + jnp.exp(sink - m)
    return p, l, m


def _stack_halves(q128, lo):
    return jnp.concatenate([jnp.where(lo, q128, 0.0), jnp.where(lo, 0.0, q128)], axis=0)


def _attn_b_kernel(*refs, dil, has_prev):
    if has_prev:
        q_ref, kvc_ref, kvp_ref, o_ref, lse_ref = refs
    else:
        q_ref, kvc_ref, o_ref, lse_ref = refs
    lane = lax.broadcasted_iota(jnp.int32, (1, LANES), 1)
    lo = lane < HEAD_DIM
    if has_prev:
        valid = _band_mask(pl.program_id(1) == 0, 2 * BLOCK)
    else:
        qi = lax.broadcasted_iota(jnp.int32, (2 * BLOCK, BLOCK), 0) % BLOCK
        sj = lax.broadcasted_iota(jnp.int32, (2 * BLOCK, BLOCK), 1)
        valid = qi >= sj

    def residue(r):
        rows = pl.ds(r, BLOCK, stride=dil) if dil > 1 else pl.ds(0, BLOCK)
        for p in range(B_PAIRS):
            qs = _stack_halves(q_ref[p, rows, :], lo).astype(BF16)
            k2, v2 = kvc_ref[p, rows, :], kvc_ref[B_PAIRS + p, rows, :]
            if has_prev:
                k2 = jnp.concatenate([kvp_ref[p, rows, :], k2], axis=0)
                v2 = jnp.concatenate([kvp_ref[B_PAIRS + p, rows, :], v2], axis=0)
            s = lax.dot_general(qs, k2.astype(BF16), NT_DIMS, preferred_element_type=F32)
            pr, l, m = _softmax_rows(s, valid)
            o = jnp.dot(pr.astype(BF16), v2.astype(BF16), preferred_element_type=F32) * (1.0 / l)
            lse = jnp.broadcast_to(m + jnp.log(l), o.shape)
            o_ref[p, rows, :] = jnp.where(lo, o[0:BLOCK], o[BLOCK:2 * BLOCK])
            lse_ref[p, rows, :] = jnp.where(lo, lse[0:BLOCK], lse[BLOCK:2 * BLOCK])

    if dil == 1:
        residue(0)
    else:
        def body(r, c):
            residue(r)
            return c
        lax.fori_loop(0, dil, body, 0, unroll=4)


def _attn_b_prompt(qb, kvb, g, dil, n_seq, seq):
    t = qb.shape[1]
    rows = BLOCK * dil
    nb = seq // rows
    qv = qb.reshape(qb.shape[0], n_seq, seq, LANES)
    kvv = kvb.reshape(kvb.shape[0], n_seq, seq, LANES)
    in_specs = [pl.BlockSpec((B_PAIRS, None, rows, LANES), lambda n, b: (g, n, b, 0)),
                pl.BlockSpec((2 * B_PAIRS, None, rows, LANES), lambda n, b: (0, n, b, 0))]
    args = [qv, kvv]
    if nb > 1:
        in_specs.append(pl.BlockSpec((2 * B_PAIRS, None, rows, LANES), lambda n, b: (0, n, jnp.maximum(b - 1, 0), 0)))
        args.append(kvv)
    out_spec = pl.BlockSpec((B_PAIRS, None, rows, LANES), lambda n, b: (0, n, b, 0))
    o, lse = pl.pallas_call(
        functools.partial(_attn_b_kernel, dil=dil, has_prev=nb > 1),
        grid=(n_seq, nb), in_specs=in_specs, out_specs=[out_spec, out_spec],
        out_shape=[jax.ShapeDtypeStruct((B_PAIRS, n_seq, seq, LANES), F32)] * 2,
        compiler_params=_cparams(("parallel", "arbitrary")), name=f"attn_b{g}",
    )(*args)
    return o.reshape(B_PAIRS, t, LANES), lse.reshape(B_PAIRS, t, LANES)


def _dup_heads(x128, lo):
    xr = pltpu.roll(x128, HEAD_DIM, 1)
    return jnp.where(lo, x128, xr), jnp.where(lo, xr, x128)


def _attn_a_kernel(sink_ref, q_ref, kvc_ref, kvp_ref, o_ref):
    valid = _band_mask(pl.program_id(1) == 0, BLOCK)
    kv2 = jnp.concatenate([kvp_ref[...], kvc_ref[...]], axis=0)
    lane = lax.broadcasted_iota(jnp.int32, (1, LANES), 1)
    lo = lane < HEAD_DIM
    kdup = [t.astype(BF16) for t in _dup_heads(kv2[:, 0:LANES], lo)]
    vdup = [t.astype(BF16) for t in _dup_heads(kv2[:, LANES:2 * LANES], lo)]
    for p in range(A_Q_HEADS // 2):
        kh = (2 * p) // A_GROUP
        q128 = q_ref[:, p * LANES:(p + 1) * LANES]
        halves = []
        for e in range(2):
            qm = jnp.where(lo if e == 0 else jnp.logical_not(lo), q128, jnp.zeros_like(q128))
            s = lax.dot_general(qm, kdup[kh], NT_DIMS, preferred_element_type=F32)
            pe, l, _ = _softmax_rows(s, valid, sink_ref[2 * p + e])
            o = jnp.dot(pe.astype(BF16), vdup[kh], preferred_element_type=F32)
            halves.append(o * (1.0 / l))
        o_ref[:, p * LANES:(p + 1) * LANES] = jnp.where(lo, halves[0], halves[1]).astype(o_ref.dtype)


def _attn_a_prompt(sinks, qa, kva, n_seq, seq):
    t = qa.shape[0]
    nb = seq // BLOCK
    qv = qa.reshape(n_seq, seq, A_Q_W)
    kvv = kva.reshape(n_seq, seq, 2 * A_KV_W)
    in_specs = [
        pl.BlockSpec(memory_space=pltpu.SMEM),
        pl.BlockSpec((None, BLOCK, A_Q_W), lambda n, b: (n, b, 0)),
        pl.BlockSpec((None, BLOCK, 2 * A_KV_W), lambda n, b: (n, b, 0)),
        pl.BlockSpec((None, BLOCK, 2 * A_KV_W), lambda n, b: (n, jnp.maximum(b - 1, 0), 0)),
    ]
    o = pl.pallas_call(
        _attn_a_kernel, grid=(n_seq, nb), in_specs=in_specs,
        out_specs=pl.BlockSpec((None, BLOCK, A_Q_W), lambda n, b: (n, b, 0)),
        out_shape=jax.ShapeDtypeStruct((n_seq, seq, A_Q_W), BF16),
        compiler_params=_cparams(("parallel", "arbitrary")), name="attn_a",
    )(sinks, qv, kvv, kvv)
    return o.reshape(t, A_Q_W)


def _combine3(outs, lses):
    m = jnp.maximum(jnp.maximum(lses[0], lses[1]), lses[2])
    es = [jnp.exp(x - m) for x in lses]
    num = es[0] * outs[0] + es[1] * outs[1] + es[2] * outs[2]
    return num / (es[0] + es[1] + es[2])


def _combine_kernel(o0_ref, o1_ref, o2_ref, l0_ref, l1_ref, l2_ref, ob_ref):
    for p in range(B_PAIRS):
        ob_ref[:, p * LANES:(p + 1) * LANES] = _combine3(
            [o0_ref[p], o1_ref[p], o2_ref[p]], [l0_ref[p], l1_ref[p], l2_ref[p]]).astype(ob_ref.dtype)


def _combine(outs, lses, tm):
    t = outs[0].shape[1]
    spec = pl.BlockSpec((B_PAIRS, tm, LANES), lambda i: (0, i, 0))
    return pl.pallas_call(
        _combine_kernel, grid=(t // tm,), in_specs=[spec] * 6,
        out_specs=pl.BlockSpec((tm, B_OUT_W), lambda i: (i, 0)),
        out_shape=jax.ShapeDtypeStruct((t, B_OUT_W), BF16),
        compiler_params=_cparams(("parallel",)), name="combine",
    )(*outs, *lses)


T_NEW = 8
SEQS_PER_LANE_BLOCK = LANES // T_NEW
S_PAIRS = 2
SAMPLE_CHUNKS = B_PAIRS // S_PAIRS
ROLL_ROWS = 64


def _roll_cache(c_ref, new_rolled, o_ref, kv, row0, nrows):
    lb = c_ref.shape[-1]
    lane = lax.broadcasted_iota(jnp.int32, (1, LANES), 1)
    tail = lane >= LANES - T_NEW
    pieces = []
    for r0 in range(0, nrows, ROLL_ROWS):
        rs = slice(row0 + r0, row0 + r0 + ROLL_ROWS)
        x = c_ref[kv, rs, :]
        pieces.append(x.astype(BF16))
        t = pltpu.roll(x, lb - T_NEW, 1)
        if lb > LANES:
            o_ref[kv, rs, 0:lb - LANES] = t[:, 0:lb - LANES]
        o_ref[kv, rs, lb - LANES:lb] = jnp.where(tail, new_rolled[r0:r0 + ROLL_ROWS, :], t[:, lb - LANES:lb])
    return jnp.concatenate(pieces, axis=0)


def _sample_scores(qs, kt, nk, dil, lb):
    rows = qs.shape[0]
    s_c = jnp.dot(qs, kt, preferred_element_type=F32)
    s_n = jnp.dot(qs, nk.astype(BF16), preferred_element_type=F32)
    tok = lax.broadcasted_iota(jnp.int32, (rows, 1), 0) % T_NEW
    pos = lax.broadcasted_iota(jnp.int32, (rows, lb), 1)
    d_c = lb + tok - pos
    valid_c = ((d_c & (dil - 1)) == 0) & (d_c <= (B_KEYS - 1) * dil)
    lane = lax.broadcasted_iota(jnp.int32, (rows, LANES), 1)
    d_n = tok - (lane - (LANES - T_NEW))
    valid_n = (lane >= LANES - T_NEW) & (d_n >= 0) & ((d_n & (dil - 1)) == 0)
    return jnp.where(valid_c, s_c, NEG), jnp.where(valid_n, s_n, NEG)


def _sample_probs(qs, kt, nk, dil, sink=None):
    s_c, s_n = _sample_scores(qs, kt, nk, dil, kt.shape[-1])
    m = jnp.maximum(jnp.max(s_c, axis=-1, keepdims=True), jnp.max(s_n, axis=-1, keepdims=True))
    if sink is not None:
        m = jnp.maximum(m, sink)
    p_c, p_n = jnp.exp(s_c - m), jnp.exp(s_n - m)
    l = jnp.sum(p_c, axis=-1, keepdims=True) + jnp.sum(p_n, axis=-1, keepdims=True)
    if sink is not None:
        l = l + jnp.exp(sink - m)
    return p_c.astype(BF16), p_n.astype(BF16), l, m


def _sample_values(p_c, p_n, l, m, vt, nv):
    o = lax.dot_general(p_c, vt, NT_DIMS, preferred_element_type=F32)
    o = o + lax.dot_general(p_n, nv.astype(BF16), NT_DIMS, preferred_element_type=F32)
    return o * (1.0 / l), m + jnp.log(l)


def _sample_units(n, c, sink_ref, qa_ref, qb_ref, na_ref, n0_ref, n1_ref, n2_ref,
                  ca_ref, c0_ref, c1_ref, c2_ref,
                  oa_ref, ob_ref, ra_ref, r0_ref, r1_ref, r2_ref):
    shift = (LANES - T_NEW) - (n % SEQS_PER_LANE_BLOCK) * T_NEW
    lane = lax.broadcasted_iota(jnp.int32, (1, LANES), 1)
    lo = lane < HEAD_DIM
    pending, done = {}, {}
    units = []

    groups = ((c0_ref, n0_ref, r0_ref), (c1_ref, n1_ref, r1_ref), (c2_ref, n2_ref, r2_ref))
    for lp in range(S_PAIRS):
        for g, (c_ref, n_ref, r_ref) in enumerate(groups):
            def first(lp=lp, g=g, c_ref=c_ref, n_ref=n_ref, r_ref=r_ref):
                nrows = pl.ds(pl.multiple_of((c * S_PAIRS + lp) * LANES, LANES), LANES)
                nk = pltpu.roll(n_ref[0, nrows, :], shift, 1)
                nv = pltpu.roll(n_ref[1, nrows, :], shift, 1)
                qs = _stack_halves(qb_ref[g * B_PAIRS + c * S_PAIRS + lp], lo).astype(BF16)
                kt = _roll_cache(c_ref, nk, r_ref, 0, lp * LANES, LANES)
                vt = _roll_cache(c_ref, nv, r_ref, 1, lp * LANES, LANES)
                pending[lp, g] = _sample_probs(qs, kt, nk, B_PATTERNS[g][1]) + (vt, nv)

            def second(lp=lp, g=g):
                o, lse = _sample_values(*pending.pop((lp, g)))
                lse = jnp.broadcast_to(lse, o.shape)
                done[lp, g] = (jnp.where(lo, o[0:T_NEW], o[T_NEW:2 * T_NEW]),
                               jnp.where(lo, lse[0:T_NEW], lse[T_NEW:2 * T_NEW]))
                if g == N_B_GROUPS - 1:
                    outs, lses = zip(*[done.pop((lp, k)) for k in range(N_B_GROUPS)])
                    ob_ref[:, lp * LANES:(lp + 1) * LANES] = _combine3(outs, lses)

            units.append((first, second))

    hrow = lax.broadcasted_iota(jnp.int32, (A_GROUP * T_NEW, 1), 0) // T_NEW
    for kh in range(A_KV_HEADS):
        def first(kh=kh):
            if kh == 0:
                nk = pltpu.roll(na_ref[0], shift, 1)
                nv = pltpu.roll(na_ref[1], shift, 1)
                pending["a"] = (nk, nv, _roll_cache(ca_ref, nk, ra_ref, 0, 0, A_KV_W),
                                _roll_cache(ca_ref, nv, ra_ref, 1, 0, A_KV_W))
            nk, nv, kt_a, vt_a = pending["a"]
            hs = slice(kh * HEAD_DIM, (kh + 1) * HEAD_DIM)
            dup = lambda x: jnp.concatenate([x[hs], x[hs]], axis=0)
            tiles = [_stack_halves(qa_ref[:, (kh * (A_GROUP // 2) + p) * LANES:(kh * (A_GROUP // 2) + p + 1) * LANES], lo)
                     for p in range(A_GROUP // 2)]
            qs = jnp.concatenate(tiles, axis=0).astype(BF16)
            sink = jnp.zeros((A_GROUP * T_NEW, 1), F32)
            for hh in range(A_GROUP):
                sink = jnp.where(hrow == hh, sink_ref[kh * A_GROUP + hh], sink)
            pending["a", kh] = _sample_probs(qs, dup(kt_a), dup(nk), 1, sink) + (dup(vt_a), dup(nv))

        def second(kh=kh):
            o, _ = _sample_values(*pending.pop(("a", kh)))
            for p in range(A_GROUP // 2):
                o0 = o[(2 * p) * T_NEW:(2 * p + 1) * T_NEW, :]
                o1 = o[(2 * p + 1) * T_NEW:(2 * p + 2) * T_NEW, :]
                c0 = (kh * (A_GROUP // 2) + p) * LANES
                oa_ref[:, c0:c0 + LANES] = jnp.where(lo, o0, o1)

        units.append((first, second))
    return units


N_SAMPLE_IN, N_SAMPLE_OUT = 11, 6


def _sample_specs(idx, nt, cache_a, cache_bs):
    rows = S_PAIRS * LANES
    seq = lambda *g: idx(*g)[0]
    chunk = lambda *g: idx(*g)[1]
    lane_blk = lambda *g: idx(*g)[0] // SEQS_PER_LANE_BLOCK
    in_specs = [
        pl.BlockSpec(memory_space=pltpu.SMEM),
        pl.BlockSpec((T_NEW, A_Q_W), lambda *g: (seq(*g), 0)),
        pl.BlockSpec((N_B_GROUPS * B_PAIRS, T_NEW, LANES), lambda *g: (0, seq(*g), 0)),
        pl.BlockSpec((2, A_KV_W, LANES), lambda *g: (0, 0, lane_blk(*g))),
    ]
    in_specs += [pl.BlockSpec((2, B_OUT_W, LANES), lambda *g: (0, 0, lane_blk(*g)))] * N_B_GROUPS
    cache_specs = [pl.BlockSpec((None, 2, A_KV_W, cache_a.shape[-1]), lambda *g: (seq(*g), 0, 0, 0))]
    cache_specs += [pl.BlockSpec((None, 2, rows, cb.shape[-1]), lambda *g: (seq(*g), 0, chunk(*g), 0))
                    for cb in cache_bs]
    out_specs = [pl.BlockSpec((T_NEW, A_Q_W), lambda *g: (seq(*g), 0)),
                 pl.BlockSpec((T_NEW, rows), lambda *g: (seq(*g), chunk(*g)))] + cache_specs
    out_shape = [jax.ShapeDtypeStruct((nt, A_Q_W), F32), jax.ShapeDtypeStruct((nt, B_OUT_W), F32),
                 jax.ShapeDtypeStruct(cache_a.shape, F32)] + [jax.ShapeDtypeStruct(cb.shape, F32) for cb in cache_bs]
    return in_specs + cache_specs, out_specs, out_shape


def _layer_norm(z, g, b):
    mu = jnp.mean(z, axis=-1, keepdims=True)
    zc = z - mu
    var = jnp.mean(zc * zc, axis=-1, keepdims=True)
    return zc * lax.rsqrt(var + LN_EPS) * g + b


def _mix_kernel(x_ref, oa_ref, ob_ref, g_ref, wa_ref, wb_ref, wo_ref, lng_ref, lnb_ref, h_ref):
    br_a = jnp.dot(oa_ref[...].astype(BF16), wa_ref[...], preferred_element_type=F32)
    br_b = jnp.dot(ob_ref[...].astype(BF16), wb_ref[...], preferred_element_type=F32)
    ga = jax.nn.sigmoid(g_ref[:, 0:D_MODEL].astype(F32))
    gb = jax.nn.sigmoid(g_ref[:, D_MODEL:2 * D_MODEL].astype(F32))
    merged = (ga * br_a + gb * br_b).astype(BF16)
    mix = jnp.dot(merged, wo_ref[...], preferred_element_type=F32)
    h_ref[...] = _layer_norm(ALPHA * x_ref[...] + mix, lng_ref[...], lnb_ref[...])


def _mix(x2d, oa, ob, gates, wa, wb, wo, ln_g, ln_b, tm):
    m = x2d.shape[0]
    row = lambda w: pl.BlockSpec((tm, w), lambda i: (i, 0))
    const = lambda a: pl.BlockSpec(a.shape, lambda i: (0, 0), pipeline_mode=pl.Buffered(1))
    return pl.pallas_call(
        _mix_kernel, grid=(m // tm,),
        in_specs=[row(D_MODEL), row(A_Q_W), row(B_OUT_W), row(2 * D_MODEL),
                  const(wa), const(wb), const(wo), const(ln_g), const(ln_b)],
        out_specs=row(D_MODEL), out_shape=jax.ShapeDtypeStruct((m, D_MODEL), F32),
        compiler_params=_cparams(("parallel",)), name="mix",
    )(x2d, oa, ob, gates, wa, wb, wo, ln_g, ln_b)


FFN_OUT_CHUNK = 256
LN_ROWS = 128


N_FFN_IN = 7


def _ffn_kernel(*refs, with_sample, row_split):
    h_ref, w1_ref, b1_ref, w2_ref, b2_ref, lng_ref, lnb_ref = refs[:N_FFN_IN]
    if with_sample:
        sample_in = refs[N_FFN_IN:N_FFN_IN + N_SAMPLE_IN]
        y_ref = refs[N_FFN_IN + N_SAMPLE_IN]
        sample_out = refs[N_FFN_IN + N_SAMPLE_IN + 1:N_FFN_IN + N_SAMPLE_IN + 1 + N_SAMPLE_OUT]
    else:
        y_ref = refs[N_FFN_IN]
    hb_ref = refs[-1]
    f, r = pl.program_id(1), pl.program_id(2)
    n_f, n_r = pl.num_programs(1), pl.num_programs(2)
    sub = h_ref.shape[0] // row_split
    tf = w1_ref.shape[1]
    rows = pl.ds(pl.multiple_of(r * sub, sub), sub)

    @pl.when((f == 0) & (r == 0))
    def _():
        hb_ref[...] = h_ref[...].astype(BF16)
        y_ref[...] = jnp.zeros_like(y_ref)

    parts = {}

    def up(k):
        cs = slice(k * (tf // 2), (k + 1) * (tf // 2))
        t = jnp.dot(hb_ref[rows, :], w1_ref[:, cs], preferred_element_type=F32) + b1_ref[:, cs]
        parts[k] = jnp.square(jnp.maximum(t, 0.0)).astype(BF16)

    def down(k):
        if "u" not in parts:
            parts["u"] = jnp.concatenate([parts.pop(0), parts.pop(1)], axis=1)
        cs = slice(k * FFN_OUT_CHUNK, (k + 1) * FFN_OUT_CHUNK)
        y_ref[rows, cs] += jnp.dot(parts["u"], w2_ref[:, cs], preferred_element_type=F32)

    pieces = [functools.partial(up, 0), functools.partial(up, 1)]
    pieces += [functools.partial(down, k) for k in range(D_MODEL // FFN_OUT_CHUNK)]
    if with_sample:
        step = (pl.program_id(0) * n_f + f) * n_r + r
        units = _sample_units(step // SAMPLE_CHUNKS, step % SAMPLE_CHUNKS, *sample_in, *sample_out)
        units[0][0]()
        for k in range(max(len(units), len(pieces))):
            if k + 1 < len(units):
                units[k + 1][0]()
            if k < len(pieces):
                pieces[k]()
            if k < len(units):
                units[k][1]()
    else:
        for piece in pieces:
            piece()

    @pl.when((f == n_f - 1) & (r == n_r - 1))
    def _():
        def ln_rows(k, carry):
            rs = pl.ds(pl.multiple_of(k * LN_ROWS, LN_ROWS), LN_ROWS)
            z = ALPHA * h_ref[rs, :] + y_ref[rs, :] + b2_ref[...]
            y_ref[rs, :] = _layer_norm(z, lng_ref[...], lnb_ref[...])
            return carry
        lax.fori_loop(0, h_ref.shape[0] // LN_ROWS, ln_rows, 0)


def _ffn(h, w1, b1, w2, b2, ln_g, ln_b, tm, tf, row_split, sample_args=None):
    m = h.shape[0]
    grid = (m // tm, D_FF // tf, row_split)
    const = lambda a: pl.BlockSpec(a.shape, lambda i, f, r: (0, 0))
    once = pl.Buffered(1) if row_split > 1 else None
    in_specs = [pl.BlockSpec((tm, D_MODEL), lambda i, f, r: (i, 0), pipeline_mode=once),
                pl.BlockSpec((D_MODEL, tf), lambda i, f, r: (0, f)),
                pl.BlockSpec((1, tf), lambda i, f, r: (0, f)),
                pl.BlockSpec((tf, D_MODEL), lambda i, f, r: (f, 0)),
                const(b2), const(ln_g), const(ln_b)]
    out_specs = [pl.BlockSpec((tm, D_MODEL), lambda i, f, r: (i, 0), pipeline_mode=once)]
    out_shape = [jax.ShapeDtypeStruct((m, D_MODEL), F32)]
    args = [h, w1, b1, w2, b2, ln_g, ln_b]
    if sample_args is not None:
        sinks, qa, qb, new_a, new_bs, cache_a, cache_bs = sample_args
        nt = qa.shape[0]
        assert grid[0] * grid[1] * grid[2] == (nt // T_NEW) * SAMPLE_CHUNKS

        def idx(i, f, r):
            step = (i * grid[1] + f) * grid[2] + r
            return step // SAMPLE_CHUNKS, step % SAMPLE_CHUNKS

        s_in, s_out, s_shape = _sample_specs(idx, nt, cache_a, cache_bs)
        in_specs += s_in
        out_specs += s_out
        out_shape += s_shape
        args += [sinks, qa, qb, new_a, *new_bs, cache_a, *cache_bs]
    out = pl.pallas_call(
        functools.partial(_ffn_kernel, with_sample=sample_args is not None, row_split=row_split),
        grid=grid, in_specs=in_specs, out_specs=out_specs, out_shape=out_shape,
        scratch_shapes=[pltpu.VMEM((tm, D_MODEL), BF16)],
        compiler_params=_cparams(("arbitrary", "arbitrary", "arbitrary")),
        name="ffn_sample" if sample_args is not None else "ffn",
    )(*args)
    return out if sample_args is not None else out[0]


def _to_cache_layout(kvt, heads):
    n, _, _, rows = kvt.shape
    return jnp.transpose(kvt.reshape(n, 2, heads, HEAD_DIM, rows), (0, 4, 1, 2, 3))[None]


def _from_cache_layout(cache):
    _, n, rows, _, heads, _ = cache.shape
    return jnp.transpose(cache[0], (0, 2, 3, 4, 1)).reshape(n, 2, heads * HEAD_DIM, rows)


def kernel(x_prompt, x_sample, cache_a_kv, cache_b1_kv, cache_b2_kv, cache_b3_kv, w_in, a_sinks,
           w_branch_a, w_branch_b, w_out, ln1_g, ln1_b, w_ff1, b_ff1, w_ff2, b_ff2, ln2_g, ln2_b):
    n_p, seq, _ = x_prompt.shape
    n_s, t_new, _ = x_sample.shape
    assert t_new == T_NEW and w_in.shape[0] == 1
    b_caches = (cache_b1_kv, cache_b2_kv, cache_b3_kv)

    w_cat, w_g = _reorder_w_in(w_in[0])
    wa, wb, wo = w_branch_a[0].astype(BF16), w_branch_b[0].astype(BF16), w_out[0].astype(BF16)
    sinks = a_sinks[0].reshape(A_Q_HEADS).astype(F32)
    row = lambda v: v[0].reshape(1, -1)

    def finish(x2d, oa, ob, gates, sample_args=None):
        h = _mix(x2d, oa, ob, gates, wa, wb, wo, row(ln1_g), row(ln1_b), tm=512)
        tm, row_split = (1024, 2) if sample_args is not None else (1024, 1)
        return _ffn(h, w1, row(b_ff1), w2, row(b_ff2), row(ln2_g), row(ln2_b), tm, 512, row_split, sample_args)

    xp = x_prompt.reshape(n_p * seq, D_MODEL)
    cos_p, sin_p = _rope_tables(np.arange(seq))
    (qa, kva, kvt_a, qb, kvb0, kvt0, kvb1, kvt1, kvb2, kvt2) = _proj(
        xp, w_cat, cos_p, sin_p, BF16, 512, n_p, seq)
    gates, w1, w2 = _gates(xp, w_g, 1024, cast=(w_ff1[0], w_ff2[0]))
    kvbs, kvts = (kvb0, kvb1, kvb2), (kvt0, kvt1, kvt2)
    nt = n_s * T_NEW
    xs = x_sample.reshape(nt, D_MODEL)
    cos_s, sin_s = _rope_tables(PAST_LEN + np.arange(nt) % T_NEW)
    (qa_s, _, kvt_a_s, qb_s, _, kvt0_s, _, kvt1_s, _, kvt2_s) = _proj(
        xs, w_cat, cos_s, sin_s, F32, 512, 1, nt)
    gates_s = _gates(xs, w_g, tm=nt)

    o_a = _attn_a_prompt(sinks, qa, kva, n_p, seq)
    outs, lses = [], []
    for g, (win, dil) in enumerate(B_PATTERNS):
        o, lse = _attn_b_prompt(qb, kvbs[g], g, dil, n_p, seq)
        outs.append(o)
        lses.append(lse)
    o_b = _combine(outs, lses, tm=1024)
    sample_args = (sinks, qa_s, qb_s, kvt_a_s[0], [kvt0_s[0], kvt1_s[0], kvt2_s[0]],
                   _from_cache_layout(cache_a_kv), [_from_cache_layout(cb) for cb in b_caches])
    y_p, o_a_s, o_b_s, r_a, r_b1, r_b2, r_b3 = finish(xp, o_a, o_b, gates, sample_args)
    y_p = y_p.reshape(n_p, seq, D_MODEL)
    a_kv_p = _to_cache_layout(kvt_a[..., seq - min(A_WINDOW, seq):], A_KV_HEADS)
    b_kv_p = [_to_cache_layout(kvts[g][..., seq - min(win, seq):], B_HEADS)
              for g, (win, dil) in enumerate(B_PATTERNS)]

    y_s = finish(xs, o_a_s, o_b_s, gates_s).reshape(n_s, T_NEW, D_MODEL)
    a_kv_s = _to_cache_layout(r_a, A_KV_HEADS)
    b_kv_s = [_to_cache_layout(r, B_HEADS) for r in (r_b1, r_b2, r_b3)]

    return (y_p, y_s, a_kv_p, a_kv_s,
            b_kv_p[0], b_kv_s[0], b_kv_p[1], b_kv_s[1], b_kv_p[2], b_kv_s[2])
```

```python
import functools

import numpy as np
import jax
import jax.numpy as jnp
from jax import lax
from jax.experimental import pallas as pl
from jax.experimental.pallas import tpu as pltpu

D_MODEL = 2048
HEAD_DIM = 64
HALF = HEAD_DIM // 2
A_Q_HEADS = 16
A_KV_HEADS = 2
A_GROUP = A_Q_HEADS // A_KV_HEADS
A_WINDOW = 128
B_PATTERNS = ((128, 1), (512, 4), (2048, 16))
N_B_GROUPS = len(B_PATTERNS)
B_HEADS = 8
B_KEYS = 128
BLOCK = 128
D_FF = 4 * D_MODEL
ROPE_THETA = 10000.0
ALPHA = 2.0 ** 0.25
LN_EPS = 1e-5
NEG = -1e30
PAST_LEN = 8192

A_Q_W = A_Q_HEADS * HEAD_DIM
A_KV_W = A_KV_HEADS * HEAD_DIM
B_W = N_B_GROUPS * B_HEADS * HEAD_DIM
B_OUT_W = B_HEADS * HEAD_DIM
LANES = 128
B_PAIRS = B_OUT_W // LANES
VMEM_LIMIT = 60 * 1024 * 1024

F32 = jnp.float32
BF16 = jnp.bfloat16
NT_DIMS = (((1,), (1,)), ((), ()))


def _cparams(sem):
    return pltpu.CompilerParams(dimension_semantics=sem, vmem_limit_bytes=VMEM_LIMIT)


PROJ_TN = 512
_T_QA, _T_KVA, _T_QB, _T_KVB = 0, 2, 3, 6
PROJ_TILES = 12


def _reorder_w_in(w_in):
    a_end = A_Q_W + 2 * A_KV_W
    qkv_end = a_end + 3 * B_W
    pad = jnp.zeros((w_in.shape[0], PROJ_TN - 2 * A_KV_W), BF16)
    w_qkv = jnp.concatenate([w_in[:, :a_end].astype(BF16), pad, w_in[:, a_end:qkv_end].astype(BF16)], axis=1)
    return w_qkv, w_in[:, qkv_end:].astype(BF16)


def _rope_tables(pos):
    inv = (np.float32(1.0) / np.power(np.float32(ROPE_THETA), np.arange(HALF, dtype=np.float32) / np.float32(HALF)))
    ang = (np.asarray(pos, np.float32)[:, None] * inv[None, :]).astype(np.float32)
    cos, sin = np.cos(ang).astype(np.float32), np.sin(ang).astype(np.float32)
    cosf = np.concatenate([cos, cos, cos, cos], axis=1)
    sinf = np.concatenate([-sin, sin, -sin, sin], axis=1)
    return jnp.asarray(cosf), jnp.asarray(sinf)


def _rope_chunk(t, cos, sin, lo):
    sw = jnp.where(lo, pltpu.roll(t, LANES - HALF, 1), pltpu.roll(t, HALF, 1))
    return t * cos + sw * sin


def _proj_kernel(x_ref, w_ref, cos_ref, sin_ref,
                 qa_ref, kva_ref, kvta_ref, qb_ref,
                 kvb0_ref, kvt0_ref, kvb1_ref, kvt1_ref, kvb2_ref, kvt2_ref, xb_ref, acc_ref):
    j = pl.program_id(1)
    tm = x_ref.shape[0]

    @pl.when(j == 0)
    def _():
        xb_ref[...] = x_ref[...].astype(BF16)

    lane = lax.broadcasted_iota(jnp.int32, (1, LANES), 1)
    lo = (lane % HEAD_DIM) < HALF
    scale = HEAD_DIM ** -0.5
    kv_refs = ((kvb0_ref, kvt0_ref), (kvb1_ref, kvt1_ref), (kvb2_ref, kvt2_ref))

    def multiply(t):
        n_cols = 2 * A_KV_W if t == _T_KVA else PROJ_TN
        acc_ref[t % 2, :, 0:n_cols] = jnp.dot(xb_ref[...], w_ref[:, 0:n_cols], preferred_element_type=F32)

    def finish(t):
        def chunk(c, rope, mul=1.0):
            v = acc_ref[t % 2, :, c * LANES:(c + 1) * LANES]
            if rope:
                v = _rope_chunk(v, cos_ref[...], sin_ref[...], lo)
            return v if mul == 1.0 else v * mul

        if t < _T_KVA:
            for c in range(PROJ_TN // LANES):
                qa_ref[:, c * LANES:(c + 1) * LANES] = chunk(c, True, scale).astype(qa_ref.dtype)
        elif t == _T_KVA:
            k, v = chunk(0, True), chunk(1, False)
            kva_ref[:, 0:LANES] = k
            kva_ref[:, LANES:2 * LANES] = v
            kvta_ref[0] = k.T[:, tm - kvta_ref.shape[-1]:]
            kvta_ref[1] = v.T[:, tm - kvta_ref.shape[-1]:]
        elif t < _T_KVB:
            for c in range(B_PAIRS):
                qb_ref[c] = chunk(c, True, scale)
        else:
            is_v, g = divmod(t - _T_KVB, N_B_GROUPS)
            slab_ref, t_ref = kv_refs[g]
            for c in range(B_PAIRS):
                v = chunk(c, not is_v)
                slab_ref[c] = v
                t_ref[is_v, c * LANES:(c + 1) * LANES, :] = v.T[:, tm - t_ref.shape[-1]:]

    for t in range(PROJ_TILES + 1):
        @pl.when(j == t)
        def _(t=t):
            if t < PROJ_TILES:
                multiply(t)
            if t > 0:
                finish(t - 1)


def _proj(x2d, w_cat, cosf, sinf, q_dtype, tm, n_seq, seq, windows):
    m = x2d.shape[0]
    assert m == n_seq * seq and seq % tm == 0
    per_seq = seq // tm
    grid = (m // tm, PROJ_TILES + 1)

    def clamp(j, lo, n):
        return jnp.clip(j - lo, 0, n - 1)

    def window_spec(rows, win):
        lanes = min(win, tm)
        first = per_seq - win // lanes
        return pl.BlockSpec((None, 2, rows, lanes),
                            lambda i, j: (i // per_seq, 0, 0, jnp.maximum(i % per_seq - first, 0)))

    in_specs = [
        pl.BlockSpec((tm, D_MODEL), lambda i, j: (i, 0)),
        pl.BlockSpec((D_MODEL, PROJ_TN), lambda i, j: (0, jnp.minimum(j, PROJ_TILES - 1))),
        pl.BlockSpec((tm, LANES), lambda i, j: (i % per_seq, 0)),
        pl.BlockSpec((tm, LANES), lambda i, j: (i % per_seq, 0)),
    ]
    out_specs = [
        pl.BlockSpec((tm, PROJ_TN), lambda i, j: (i, clamp(j, _T_QA + 1, 2))),
        pl.BlockSpec((tm, 2 * A_KV_W), lambda i, j: (i, 0)),
        window_spec(A_KV_W, windows[0]),
        pl.BlockSpec((B_PAIRS, tm, LANES), lambda i, j: (clamp(j, _T_QB + 1, N_B_GROUPS), i, 0)),
    ]
    out_shape = [
        jax.ShapeDtypeStruct((m, A_Q_W), q_dtype),
        jax.ShapeDtypeStruct((m, 2 * A_KV_W), F32),
        jax.ShapeDtypeStruct((n_seq, 2, A_KV_W, windows[0]), F32),
        jax.ShapeDtypeStruct((N_B_GROUPS * B_PAIRS, m, LANES), F32),
    ]
    for g in range(N_B_GROUPS):
        is_v = lambda j, g=g: (j >= _T_KVB + N_B_GROUPS + g + 1).astype(jnp.int32)
        out_specs.append(pl.BlockSpec((B_PAIRS, tm, LANES), lambda i, j, is_v=is_v: (is_v(j), i, 0)))
        out_specs.append(window_spec(B_OUT_W, windows[1 + g]))
        out_shape.append(jax.ShapeDtypeStruct((2 * B_PAIRS, m, LANES), F32))
        out_shape.append(jax.ShapeDtypeStruct((n_seq, 2, B_OUT_W, windows[1 + g]), F32))
    return pl.pallas_call(
        _proj_kernel, grid=grid, in_specs=in_specs, out_specs=out_specs, out_shape=out_shape,
        scratch_shapes=[pltpu.VMEM((tm, D_MODEL), BF16), pltpu.VMEM((2, tm, PROJ_TN), F32)],
        compiler_params=_cparams(("arbitrary", "arbitrary")), name="proj",
    )(x2d, w_cat, cosf, sinf)


GATE_TN = 512


def _gates_kernel(*refs):
    n_cast = (len(refs) - 4) // 2
    x_ref, w_ref = refs[:2]
    g_ref, xb_ref = refs[2 + n_cast], refs[-1]

    @pl.when(pl.program_id(1) == 0)
    def _():
        xb_ref[...] = x_ref[...].astype(BF16)

    g_ref[...] = jnp.dot(xb_ref[...], w_ref[...], preferred_element_type=F32).astype(g_ref.dtype)
    for src, dst in zip(refs[2:2 + n_cast], refs[3 + n_cast:3 + 2 * n_cast]):
        dst[...] = src[...].astype(BF16)


def _gates(x2d, w_g, tm, cast=()):
    m = x2d.shape[0]
    n = w_g.shape[1]
    grid = (m // tm, n // GATE_TN)
    steps = grid[0] * grid[1]
    slab = lambda a: pl.BlockSpec((a.shape[0] // steps, a.shape[1]), lambda i, j: (i * grid[1] + j, 0))
    assert all(a.shape[0] % (16 * steps) == 0 for a in cast)
    out = pl.pallas_call(
        _gates_kernel, grid=grid,
        in_specs=[pl.BlockSpec((tm, D_MODEL), lambda i, j: (i, 0)),
                  pl.BlockSpec((D_MODEL, GATE_TN), lambda i, j: (0, j))] + [slab(a) for a in cast],
        out_specs=[pl.BlockSpec((tm, GATE_TN), lambda i, j: (i, j))] + [slab(a) for a in cast],
        out_shape=[jax.ShapeDtypeStruct((m, n), BF16)] + [jax.ShapeDtypeStruct(a.shape, BF16) for a in cast],
        scratch_shapes=[pltpu.VMEM((tm, D_MODEL), BF16)],
        compiler_params=_cparams(("arbitrary", "arbitrary")), name="gates",
    )(x2d, w_g, *cast)
    return out if cast else out[0]


def _band_mask(first_block, rows):
    qi = lax.broadcasted_iota(jnp.int32, (rows, 2 * BLOCK), 0) % BLOCK
    sj = lax.broadcasted_iota(jnp.int32, (rows, 2 * BLOCK), 1)
    dist = BLOCK + qi - sj
    lo = jnp.where(first_block, BLOCK, 0)
    return (dist >= 0) & (dist < B_KEYS) & (sj >= lo)


def _softmax_rows(s, valid, sink=None):
    s = jnp.where(valid, s, NEG)
    m = jnp.max(s, axis=-1, keepdims=True)
    if sink is not None:
        m = jnp.maximum(m, sink)
    p = jnp.exp(s - m)
    l = jnp.sum(p, axis=-1, keepdims=True)
    if sink is not None:
        l = l + jnp.exp(sink - m)
    return p, l, m


def _stack_halves(q128, lo):
    return jnp.concatenate([jnp.where(lo, q128, 0.0), jnp.where(lo, 0.0, q128)], axis=0)


def _attn_b_kernel(*refs, dil, has_prev):
    if has_prev:
        q_ref, kvc_ref, kvp_ref, o_ref, lse_ref = refs
    else:
        q_ref, kvc_ref, o_ref, lse_ref = refs
    lane = lax.broadcasted_iota(jnp.int32, (1, LANES), 1)
    lo = lane < HEAD_DIM
    if has_prev:
        valid = _band_mask(pl.program_id(1) == 0, 2 * BLOCK)
    else:
        qi = lax.broadcasted_iota(jnp.int32, (2 * BLOCK, BLOCK), 0) % BLOCK
        sj = lax.broadcasted_iota(jnp.int32, (2 * BLOCK, BLOCK), 1)
        valid = qi >= sj

    def residue(r):
        rows = pl.ds(r, BLOCK, stride=dil) if dil > 1 else pl.ds(0, BLOCK)
        for p in range(B_PAIRS):
            qs = _stack_halves(q_ref[p, rows, :], lo).astype(BF16)
            k2, v2 = kvc_ref[p, rows, :], kvc_ref[B_PAIRS + p, rows, :]
            if has_prev:
                k2 = jnp.concatenate([kvp_ref[p, rows, :], k2], axis=0)
                v2 = jnp.concatenate([kvp_ref[B_PAIRS + p, rows, :], v2], axis=0)
            s = lax.dot_general(qs, k2.astype(BF16), NT_DIMS, preferred_element_type=F32)
            pr, l, m = _softmax_rows(s, valid)
            o = jnp.dot(pr.astype(BF16), v2.astype(BF16), preferred_element_type=F32) * (1.0 / l)
            lse = jnp.broadcast_to(m + jnp.log(l), o.shape)
            o_ref[p, rows, :] = jnp.where(lo, o[0:BLOCK], o[BLOCK:2 * BLOCK])
            lse_ref[p, rows, :] = jnp.where(lo, lse[0:BLOCK], lse[BLOCK:2 * BLOCK])

    if dil == 1:
        residue(0)
    else:
        def body(r, c):
            residue(r)
            return c
        lax.fori_loop(0, dil, body, 0, unroll=4)


def _attn_b_prompt(qb, kvb, g, dil, n_seq, seq):
    t = qb.shape[1]
    rows = BLOCK * dil
    nb = seq // rows
    qv = qb.reshape(qb.shape[0], n_seq, seq, LANES)
    kvv = kvb.reshape(kvb.shape[0], n_seq, seq, LANES)
    in_specs = [pl.BlockSpec((B_PAIRS, None, rows, LANES), lambda n, b: (g, n, b, 0)),
                pl.BlockSpec((2 * B_PAIRS, None, rows, LANES), lambda n, b: (0, n, b, 0))]
    args = [qv, kvv]
    if nb > 1:
        in_specs.append(pl.BlockSpec((2 * B_PAIRS, None, rows, LANES), lambda n, b: (0, n, jnp.maximum(b - 1, 0), 0)))
        args.append(kvv)
    out_spec = pl.BlockSpec((B_PAIRS, None, rows, LANES), lambda n, b: (0, n, b, 0))
    o, lse = pl.pallas_call(
        functools.partial(_attn_b_kernel, dil=dil, has_prev=nb > 1),
        grid=(n_seq, nb), in_specs=in_specs, out_specs=[out_spec, out_spec],
        out_shape=[jax.ShapeDtypeStruct((B_PAIRS, n_seq, seq, LANES), F32)] * 2,
        compiler_params=_cparams(("parallel", "arbitrary")), name=f"attn_b{g}",
    )(*args)
    return o.reshape(B_PAIRS, t, LANES), lse.reshape(B_PAIRS, t, LANES)


def _dup_heads(x128, lo):
    xr = pltpu.roll(x128, HEAD_DIM, 1)
    return jnp.where(lo, x128, xr), jnp.where(lo, xr, x128)


def _attn_a_kernel(sink_ref, q_ref, kvc_ref, kvp_ref, o_ref):
    valid = _band_mask(pl.program_id(1) == 0, BLOCK)
    kv2 = jnp.concatenate([kvp_ref[...], kvc_ref[...]], axis=0)
    lane = lax.broadcasted_iota(jnp.int32, (1, LANES), 1)
    lo = lane < HEAD_DIM
    kdup = [t.astype(BF16) for t in _dup_heads(kv2[:, 0:LANES], lo)]
    vdup = [t.astype(BF16) for t in _dup_heads(kv2[:, LANES:2 * LANES], lo)]
    for p in range(A_Q_HEADS // 2):
        kh = (2 * p) // A_GROUP
        q128 = q_ref[:, p * LANES:(p + 1) * LANES]
        halves = []
        for e in range(2):
            qm = jnp.where(lo if e == 0 else jnp.logical_not(lo), q128, jnp.zeros_like(q128))
            s = lax.dot_general(qm, kdup[kh], NT_DIMS, preferred_element_type=F32)
            pe, l, _ = _softmax_rows(s, valid, sink_ref[2 * p + e])
            o = jnp.dot(pe.astype(BF16), vdup[kh], preferred_element_type=F32)
            halves.append(o * (1.0 / l))
        o_ref[:, p * LANES:(p + 1) * LANES] = jnp.where(lo, halves[0], halves[1]).astype(o_ref.dtype)


def _attn_a_prompt(sinks, qa, kva, n_seq, seq):
    t = qa.shape[0]
    nb = seq // BLOCK
    qv = qa.reshape(n_seq, seq, A_Q_W)
    kvv = kva.reshape(n_seq, seq, 2 * A_KV_W)
    in_specs = [
        pl.BlockSpec(memory_space=pltpu.SMEM),
        pl.BlockSpec((None, BLOCK, A_Q_W), lambda n, b: (n, b, 0)),
        pl.BlockSpec((None, BLOCK, 2 * A_KV_W), lambda n, b: (n, b, 0)),
        pl.BlockSpec((None, BLOCK, 2 * A_KV_W), lambda n, b: (n, jnp.maximum(b - 1, 0), 0)),
    ]
    o = pl.pallas_call(
        _attn_a_kernel, grid=(n_seq, nb), in_specs=in_specs,
        out_specs=pl.BlockSpec((None, BLOCK, A_Q_W), lambda n, b: (n, b, 0)),
        out_shape=jax.ShapeDtypeStruct((n_seq, seq, A_Q_W), BF16),
        compiler_params=_cparams(("parallel", "arbitrary")), name="attn_a",
    )(sinks, qv, kvv, kvv)
    return o.reshape(t, A_Q_W)


def _combine3(outs, lses):
    m = jnp.maximum(jnp.maximum(lses[0], lses[1]), lses[2])
    es = [jnp.exp(x - m) for x in lses]
    num = es[0] * outs[0] + es[1] * outs[1] + es[2] * outs[2]
    return num / (es[0] + es[1] + es[2])


def _combine_kernel(o0_ref, o1_ref, o2_ref, l0_ref, l1_ref, l2_ref, ob_ref):
    for p in range(B_PAIRS):
        ob_ref[:, p * LANES:(p + 1) * LANES] = _combine3(
            [o0_ref[p], o1_ref[p], o2_ref[p]], [l0_ref[p], l1_ref[p], l2_ref[p]]).astype(ob_ref.dtype)


def _combine(outs, lses, tm):
    t = outs[0].shape[1]
    spec = pl.BlockSpec((B_PAIRS, tm, LANES), lambda i: (0, i, 0))
    return pl.pallas_call(
        _combine_kernel, grid=(t // tm,), in_specs=[spec] * 6,
        out_specs=pl.BlockSpec((tm, B_OUT_W), lambda i: (i, 0)),
        out_shape=jax.ShapeDtypeStruct((t, B_OUT_W), BF16),
        compiler_params=_cparams(("parallel",)), name="combine",
    )(*outs, *lses)


T_NEW = 8
SEQS_PER_LANE_BLOCK = LANES // T_NEW
S_PAIRS = 2
SAMPLE_CHUNKS = B_PAIRS // S_PAIRS
ROLL_ROWS = 64


def _roll_cache(c_ref, new_rolled, o_ref, kv, row0, nrows):
    lb = c_ref.shape[-1]
    lane = lax.broadcasted_iota(jnp.int32, (1, LANES), 1)
    tail = lane >= LANES - T_NEW
    pieces = []
    for r0 in range(0, nrows, ROLL_ROWS):
        rs = slice(row0 + r0, row0 + r0 + ROLL_ROWS)
        x = c_ref[kv, rs, :]
        pieces.append(x.astype(BF16))
        t = pltpu.roll(x, lb - T_NEW, 1)
        if lb > LANES:
            o_ref[kv, rs, 0:lb - LANES] = t[:, 0:lb - LANES]
        o_ref[kv, rs, lb - LANES:lb] = jnp.where(tail, new_rolled[r0:r0 + ROLL_ROWS, :], t[:, lb - LANES:lb])
    return jnp.concatenate(pieces, axis=0)


def _sample_scores(qs, kt, nk, dil, lb):
    rows = qs.shape[0]
    s_c = jnp.dot(qs, kt, preferred_element_type=F32)
    s_n = jnp.dot(qs, nk.astype(BF16), preferred_element_type=F32)
    tok = lax.broadcasted_iota(jnp.int32, (rows, 1), 0) % T_NEW
    pos = lax.broadcasted_iota(jnp.int32, (rows, lb), 1)
    d_c = lb + tok - pos
    valid_c = ((d_c & (dil - 1)) == 0) & (d_c <= (B_KEYS - 1) * dil)
    lane = lax.broadcasted_iota(jnp.int32, (rows, LANES), 1)
    d_n = tok - (lane - (LANES - T_NEW))
    valid_n = (lane >= LANES - T_NEW) & (d_n >= 0) & ((d_n & (dil - 1)) == 0)
    return jnp.where(valid_c, s_c, NEG), jnp.where(valid_n, s_n, NEG)


def _sample_probs(qs, kt, nk, dil, sink=None):
    s_c, s_n = _sample_scores(qs, kt, nk, dil, kt.shape[-1])
    m = jnp.maximum(jnp.max(s_c, axis=-1, keepdims=True), jnp.max(s_n, axis=-1, keepdims=True))
    if sink is not None:
        m = jnp.maximum(m, sink)
    p_c, p_n = jnp.exp(s_c - m), jnp.exp(s_n - m)
    l = jnp.sum(p_c, axis=-1, keepdims=True) + jnp.sum(p_n, axis=-1, keepdims=True)
    if sink is not None:
        l = l + jnp.exp(sink - m)
    return p_c.astype(BF16), p_n.astype(BF16), l, m


def _sample_values(p_c, p_n, l, m, vt, nv):
    o = lax.dot_general(p_c, vt, NT_DIMS, preferred_element_type=F32)
    o = o + lax.dot_general(p_n, nv.astype(BF16), NT_DIMS, preferred_element_type=F32)
    return o * (1.0 / l), m + jnp.log(l)


def _sample_units(n, c, sink_ref, qa_ref, qb_ref, na_ref, n0_ref, n1_ref, n2_ref,
                  ca_ref, c0_ref, c1_ref, c2_ref,
                  oa_ref, ob_ref, ra_ref, r0_ref, r1_ref, r2_ref):
    shift = (LANES - T_NEW) - (n % SEQS_PER_LANE_BLOCK) * T_NEW
    lane = lax.broadcasted_iota(jnp.int32, (1, LANES), 1)
    lo = lane < HEAD_DIM
    pending, done = {}, {}
    units = []

    groups = ((c0_ref, n0_ref, r0_ref), (c1_ref, n1_ref, r1_ref), (c2_ref, n2_ref, r2_ref))
    for lp in range(S_PAIRS):
        for g, (c_ref, n_ref, r_ref) in enumerate(groups):
            def first(lp=lp, g=g, c_ref=c_ref, n_ref=n_ref, r_ref=r_ref):
                nrows = pl.ds(pl.multiple_of((c * S_PAIRS + lp) * LANES, LANES), LANES)
                nk = pltpu.roll(n_ref[0, nrows, :], shift, 1)
                nv = pltpu.roll(n_ref[1, nrows, :], shift, 1)
                qs = _stack_halves(qb_ref[g * B_PAIRS + c * S_PAIRS + lp], lo).astype(BF16)
                kt = _roll_cache(c_ref, nk, r_ref, 0, lp * LANES, LANES)
                vt = _roll_cache(c_ref, nv, r_ref, 1, lp * LANES, LANES)
                pending[lp, g] = _sample_probs(qs, kt, nk, B_PATTERNS[g][1]) + (vt, nv)

            def second(lp=lp, g=g):
                o, lse = _sample_values(*pending.pop((lp, g)))
                lse = jnp.broadcast_to(lse, o.shape)
                done[lp, g] = (jnp.where(lo, o[0:T_NEW], o[T_NEW:2 * T_NEW]),
                               jnp.where(lo, lse[0:T_NEW], lse[T_NEW:2 * T_NEW]))
                if g == N_B_GROUPS - 1:
                    outs, lses = zip(*[done.pop((lp, k)) for k in range(N_B_GROUPS)])
                    ob_ref[:, lp * LANES:(lp + 1) * LANES] = _combine3(outs, lses)

            units.append((first, second))

    hrow = lax.broadcasted_iota(jnp.int32, (A_GROUP * T_NEW, 1), 0) // T_NEW
    for kh in range(A_KV_HEADS):
        def first(kh=kh):
            if kh == 0:
                nk = pltpu.roll(na_ref[0], shift, 1)
                nv = pltpu.roll(na_ref[1], shift, 1)
                pending["a"] = (nk, nv, _roll_cache(ca_ref, nk, ra_ref, 0, 0, A_KV_W),
                                _roll_cache(ca_ref, nv, ra_ref, 1, 0, A_KV_W))
            nk, nv, kt_a, vt_a = pending["a"]
            hs = slice(kh * HEAD_DIM, (kh + 1) * HEAD_DIM)
            dup = lambda x: jnp.concatenate([x[hs], x[hs]], axis=0)
            tiles = [_stack_halves(qa_ref[:, (kh * (A_GROUP // 2) + p) * LANES:(kh * (A_GROUP // 2) + p + 1) * LANES], lo)
                     for p in range(A_GROUP // 2)]
            qs = jnp.concatenate(tiles, axis=0).astype(BF16)
            sink = jnp.zeros((A_GROUP * T_NEW, 1), F32)
            for hh in range(A_GROUP):
                sink = jnp.where(hrow == hh, sink_ref[kh * A_GROUP + hh], sink)
            pending["a", kh] = _sample_probs(qs, dup(kt_a), dup(nk), 1, sink) + (dup(vt_a), dup(nv))

        def second(kh=kh):
            o, _ = _sample_values(*pending.pop(("a", kh)))
            for p in range(A_GROUP // 2):
                o0 = o[(2 * p) * T_NEW:(2 * p + 1) * T_NEW, :]
                o1 = o[(2 * p + 1) * T_NEW:(2 * p + 2) * T_NEW, :]
                c0 = (kh * (A_GROUP // 2) + p) * LANES
                oa_ref[:, c0:c0 + LANES] = jnp.where(lo, o0, o1)

        units.append((first, second))
    return units


N_SAMPLE_IN, N_SAMPLE_OUT = 11, 6


def _sample_specs(idx, nt, cache_a, cache_bs):
    rows = S_PAIRS * LANES
    seq = lambda *g: idx(*g)[0]
    chunk = lambda *g: idx(*g)[1]
    lane_blk = lambda *g: idx(*g)[0] // SEQS_PER_LANE_BLOCK
    in_specs = [
        pl.BlockSpec(memory_space=pltpu.SMEM),
        pl.BlockSpec((T_NEW, A_Q_W), lambda *g: (seq(*g), 0)),
        pl.BlockSpec((N_B_GROUPS * B_PAIRS, T_NEW, LANES), lambda *g: (0, seq(*g), 0)),
        pl.BlockSpec((2, A_KV_W, LANES), lambda *g: (0, 0, lane_blk(*g))),
    ]
    in_specs += [pl.BlockSpec((2, B_OUT_W, LANES), lambda *g: (0, 0, lane_blk(*g)))] * N_B_GROUPS
    cache_specs = [pl.BlockSpec((None, 2, A_KV_W, cache_a.shape[-1]), lambda *g: (seq(*g), 0, 0, 0))]
    cache_specs += [pl.BlockSpec((None, 2, rows, cb.shape[-1]), lambda *g: (seq(*g), 0, chunk(*g), 0))
                    for cb in cache_bs]
    out_specs = [pl.BlockSpec((T_NEW, A_Q_W), lambda *g: (seq(*g), 0)),
                 pl.BlockSpec((T_NEW, rows), lambda *g: (seq(*g), chunk(*g)))] + cache_specs
    out_shape = [jax.ShapeDtypeStruct((nt, A_Q_W), F32), jax.ShapeDtypeStruct((nt, B_OUT_W), F32),
                 jax.ShapeDtypeStruct(cache_a.shape, F32)] + [jax.ShapeDtypeStruct(cb.shape, F32) for cb in cache_bs]
    return in_specs + cache_specs, out_specs, out_shape


def _layer_norm(z, g, b):
    mu = jnp.mean(z, axis=-1, keepdims=True)
    zc = z - mu
    var = jnp.mean(zc * zc, axis=-1, keepdims=True)
    return zc * lax.rsqrt(var + LN_EPS) * g + b


def _mix_kernel(x_ref, oa_ref, ob_ref, g_ref, wa_ref, wb_ref, wo_ref, lng_ref, lnb_ref, h_ref):
    br_a = jnp.dot(oa_ref[...].astype(BF16), wa_ref[...], preferred_element_type=F32)
    br_b = jnp.dot(ob_ref[...].astype(BF16), wb_ref[...], preferred_element_type=F32)
    ga = jax.nn.sigmoid(g_ref[:, 0:D_MODEL].astype(F32))
    gb = jax.nn.sigmoid(g_ref[:, D_MODEL:2 * D_MODEL].astype(F32))
    merged = (ga * br_a + gb * br_b).astype(BF16)
    mix = jnp.dot(merged, wo_ref[...], preferred_element_type=F32)
    h_ref[...] = _layer_norm(ALPHA * x_ref[...] + mix, lng_ref[...], lnb_ref[...])


def _mix(x2d, oa, ob, gates, wa, wb, wo, ln_g, ln_b, tm):
    m = x2d.shape[0]
    row = lambda w: pl.BlockSpec((tm, w), lambda i: (i, 0))
    const = lambda a: pl.BlockSpec(a.shape, lambda i: (0, 0), pipeline_mode=pl.Buffered(1))
    return pl.pallas_call(
        _mix_kernel, grid=(m // tm,),
        in_specs=[row(D_MODEL), row(A_Q_W), row(B_OUT_W), row(2 * D_MODEL),
                  const(wa), const(wb), const(wo), const(ln_g), const(ln_b)],
        out_specs=row(D_MODEL), out_shape=jax.ShapeDtypeStruct((m, D_MODEL), F32),
        compiler_params=_cparams(("parallel",)), name="mix",
    )(x2d, oa, ob, gates, wa, wb, wo, ln_g, ln_b)


FFN_OUT_CHUNK = 512
LN_ROWS = 128


N_FFN_IN = 7


def _ffn_kernel(*refs, with_sample, row_split):
    h_ref, w1_ref, b1_ref, w2_ref, b2_ref, lng_ref, lnb_ref = refs[:N_FFN_IN]
    if with_sample:
        sample_in = refs[N_FFN_IN:N_FFN_IN + N_SAMPLE_IN]
        y_ref = refs[N_FFN_IN + N_SAMPLE_IN]
        sample_out = refs[N_FFN_IN + N_SAMPLE_IN + 1:N_FFN_IN + N_SAMPLE_IN + 1 + N_SAMPLE_OUT]
    else:
        y_ref = refs[N_FFN_IN]
    hb_ref = refs[-1]
    f, r = pl.program_id(1), pl.program_id(2)
    n_f, n_r = pl.num_programs(1), pl.num_programs(2)
    sub = h_ref.shape[0] // row_split
    tf = w1_ref.shape[1]
    rows = pl.ds(pl.multiple_of(r * sub, sub), sub)

    @pl.when((f == 0) & (r == 0))
    def _():
        hb_ref[...] = h_ref[...].astype(BF16)
        y_ref[...] = jnp.zeros_like(y_ref)

    parts = {}

    def up(k):
        cs = slice(k * (tf // 2), (k + 1) * (tf // 2))
        t = jnp.dot(hb_ref[rows, :], w1_ref[:, cs], preferred_element_type=F32) + b1_ref[:, cs]
        parts[k] = jnp.square(jnp.maximum(t, 0.0)).astype(BF16)

    def down(k):
        if "u" not in parts:
            parts["u"] = jnp.concatenate([parts.pop(0), parts.pop(1)], axis=1)
        cs = slice(k * FFN_OUT_CHUNK, (k + 1) * FFN_OUT_CHUNK)
        y_ref[rows, cs] += jnp.dot(parts["u"], w2_ref[:, cs], preferred_element_type=F32)

    pieces = [functools.partial(up, 0), functools.partial(up, 1)]
    pieces += [functools.partial(down, k) for k in range(D_MODEL // FFN_OUT_CHUNK)]
    if with_sample:
        step = (pl.program_id(0) * n_f + f) * n_r + r
        units = _sample_units(step // SAMPLE_CHUNKS, step % SAMPLE_CHUNKS, *sample_in, *sample_out)
        units[0][0]()
        for k in range(max(len(units), len(pieces))):
            if k + 1 < len(units):
                units[k + 1][0]()
            if k < len(pieces):
                pieces[k]()
            if k < len(units):
                units[k][1]()
    else:
        for piece in pieces:
            piece()

    @pl.when((f == n_f - 1) & (r == n_r - 1))
    def _():
        def ln_rows(k, carry):
            rs = pl.ds(pl.multiple_of(k * LN_ROWS, LN_ROWS), LN_ROWS)
            z = ALPHA * h_ref[rs, :] + y_ref[rs, :] + b2_ref[...]
            y_ref[rs, :] = _layer_norm(z, lng_ref[...], lnb_ref[...])
            return carry
        lax.fori_loop(0, h_ref.shape[0] // LN_ROWS, ln_rows, 0)


def _ffn(h, w1, b1, w2, b2, ln_g, ln_b, tm, tf, row_split, sample_args=None):
    m = h.shape[0]
    grid = (m // tm, D_FF // tf, row_split)
    const = lambda a: pl.BlockSpec(a.shape, lambda i, f, r: (0, 0))
    once = pl.Buffered(1) if row_split > 1 else None
    in_specs = [pl.BlockSpec((tm, D_MODEL), lambda i, f, r: (i, 0), pipeline_mode=once),
                pl.BlockSpec((D_MODEL, tf), lambda i, f, r: (0, f)),
                pl.BlockSpec((1, tf), lambda i, f, r: (0, f)),
                pl.BlockSpec((tf, D_MODEL), lambda i, f, r: (f, 0)),
                const(b2), const(ln_g), const(ln_b)]
    out_specs = [pl.BlockSpec((tm, D_MODEL), lambda i, f, r: (i, 0), pipeline_mode=once)]
    out_shape = [jax.ShapeDtypeStruct((m, D_MODEL), F32)]
    args = [h, w1, b1, w2, b2, ln_g, ln_b]
    if sample_args is not None:
        sinks, qa, qb, new_a, new_bs, cache_a, cache_bs = sample_args
        nt = qa.shape[0]
        assert grid[0] * grid[1] * grid[2] == (nt // T_NEW) * SAMPLE_CHUNKS

        def idx(i, f, r):
            step = (i * grid[1] + f) * grid[2] + r
            return step // SAMPLE_CHUNKS, step % SAMPLE_CHUNKS

        s_in, s_out, s_shape = _sample_specs(idx, nt, cache_a, cache_bs)
        in_specs += s_in
        out_specs += s_out
        out_shape += s_shape
        args += [sinks, qa, qb, new_a, *new_bs, cache_a, *cache_bs]
    out = pl.pallas_call(
        functools.partial(_ffn_kernel, with_sample=sample_args is not None, row_split=row_split),
        grid=grid, in_specs=in_specs, out_specs=out_specs, out_shape=out_shape,
        scratch_shapes=[pltpu.VMEM((tm, D_MODEL), BF16)],
        compiler_params=_cparams(("arbitrary", "arbitrary", "arbitrary")),
        name="ffn_sample" if sample_args is not None else "ffn",
    )(*args)
    return out if sample_args is not None else out[0]


def _to_cache_layout(kvt, heads):
    n, _, _, rows = kvt.shape
    return jnp.transpose(kvt.reshape(n, 2, heads, HEAD_DIM, rows), (0, 4, 1, 2, 3))[None]


def _from_cache_layout(cache):
    _, n, rows, _, heads, _ = cache.shape
    return jnp.transpose(cache[0], (0, 2, 3, 4, 1)).reshape(n, 2, heads * HEAD_DIM, rows)


def kernel(x_prompt, x_sample, cache_a_kv, cache_b1_kv, cache_b2_kv, cache_b3_kv, w_in, a_sinks,
           w_branch_a, w_branch_b, w_out, ln1_g, ln1_b, w_ff1, b_ff1, w_ff2, b_ff2, ln2_g, ln2_b):
    n_p, seq, _ = x_prompt.shape
    n_s, t_new, _ = x_sample.shape
    assert t_new == T_NEW and w_in.shape[0] == 1
    b_caches = (cache_b1_kv, cache_b2_kv, cache_b3_kv)

    w_cat, w_g = _reorder_w_in(w_in[0])
    wa, wb, wo = w_branch_a[0].astype(BF16), w_branch_b[0].astype(BF16), w_out[0].astype(BF16)
    sinks = a_sinks[0].reshape(A_Q_HEADS).astype(F32)
    row = lambda v: v[0].reshape(1, -1)

    def finish(x2d, oa, ob, gates, sample_args=None):
        h = _mix(x2d, oa, ob, gates, wa, wb, wo, row(ln1_g), row(ln1_b), tm=512)
        tm, row_split = (1024, 2) if sample_args is not None else (1024, 1)
        return _ffn(h, w1, row(b_ff1), w2, row(b_ff2), row(ln2_g), row(ln2_b), tm, 512, row_split, sample_args)

    xp = x_prompt.reshape(n_p * seq, D_MODEL)
    cos_p, sin_p = _rope_tables(np.arange(seq))
    windows = (min(A_WINDOW, seq),) + tuple(min(win, seq) for win, _ in B_PATTERNS)
    (qa, kva, kvt_a, qb, kvb0, kvt0, kvb1, kvt1, kvb2, kvt2) = _proj(
        xp, w_cat, cos_p, sin_p, BF16, 512, n_p, seq, windows)
    gates, w1, w2 = _gates(xp, w_g, 1024, cast=(w_ff1[0], w_ff2[0]))
    kvbs, kvts = (kvb0, kvb1, kvb2), (kvt0, kvt1, kvt2)
    nt = n_s * T_NEW
    xs = x_sample.reshape(nt, D_MODEL)
    cos_s, sin_s = _rope_tables(PAST_LEN + np.arange(nt) % T_NEW)
    (qa_s, _, kvt_a_s, qb_s, _, kvt0_s, _, kvt1_s, _, kvt2_s) = _proj(
        xs, w_cat, cos_s, sin_s, F32, 512, 1, nt, (nt,) * (1 + N_B_GROUPS))
    gates_s = _gates(xs, w_g, tm=nt)

    o_a = _attn_a_prompt(sinks, qa, kva, n_p, seq)
    outs, lses = [], []
    for g, (win, dil) in enumerate(B_PATTERNS):
        o, lse = _attn_b_prompt(qb, kvbs[g], g, dil, n_p, seq)
        outs.append(o)
        lses.append(lse)
    o_b = _combine(outs, lses, tm=1024)
    sample_args = (sinks, qa_s, qb_s, kvt_a_s[0], [kvt0_s[0], kvt1_s[0], kvt2_s[0]],
                   _from_cache_layout(cache_a_kv), [_from_cache_layout(cb) for cb in b_caches])
    y_p, o_a_s, o_b_s, r_a, r_b1, r_b2, r_b3 = finish(xp, o_a, o_b, gates, sample_args)
    y_p = y_p.reshape(n_p, seq, D_MODEL)
    a_kv_p = _to_cache_layout(kvt_a, A_KV_HEADS)
    b_kv_p = [_to_cache_layout(kvts[g], B_HEADS) for g in range(N_B_GROUPS)]

    y_s = finish(xs, o_a_s, o_b_s, gates_s).reshape(n_s, T_NEW, D_MODEL)
    a_kv_s = _to_cache_layout(r_a, A_KV_HEADS)
    b_kv_s = [_to_cache_layout(r, B_HEADS) for r in (r_b1, r_b2, r_b3)]

    return (y_p, y_s, a_kv_p, a_kv_s,
            b_kv_p[0], b_kv_s[0], b_kv_p[1], b_kv_s[1], b_kv_p[2], b_kv_s[2])
```

```python
import functools

import numpy as np
import jax
import jax.numpy as jnp
from jax import lax
from jax.experimental import pallas as pl
from jax.experimental.pallas import tpu as pltpu

D_MODEL = 2048
HEAD_DIM = 64
HALF = HEAD_DIM // 2
A_Q_HEADS = 16
A_KV_HEADS = 2
A_GROUP = A_Q_HEADS // A_KV_HEADS
A_WINDOW = 128
B_PATTERNS = ((128, 1), (512, 4), (2048, 16))
N_B_GROUPS = len(B_PATTERNS)
B_HEADS = 8
B_KEYS = 128
BLOCK = 128
D_FF = 4 * D_MODEL
ROPE_THETA = 10000.0
ALPHA = 2.0 ** 0.25
LN_EPS = 1e-5
NEG = -1e30
PAST_LEN = 8192

A_Q_W = A_Q_HEADS * HEAD_DIM
A_KV_W = A_KV_HEADS * HEAD_DIM
B_W = N_B_GROUPS * B_HEADS * HEAD_DIM
B_OUT_W = B_HEADS * HEAD_DIM
LANES = 128
B_PAIRS = B_OUT_W // LANES
VMEM_LIMIT = 60 * 1024 * 1024

F32 = jnp.float32
BF16 = jnp.bfloat16
NT_DIMS = (((1,), (1,)), ((), ()))


def _cparams(sem):
    return pltpu.CompilerParams(dimension_semantics=sem, vmem_limit_bytes=VMEM_LIMIT)


PROJ_TN = 512
_T_QA, _T_KVA, _T_QB, _T_KVB = 0, 2, 3, 6
PROJ_TILES = 12


PREP_TN = 2 * A_KV_W
_PAD_TILE = (A_Q_W + 2 * A_KV_W) // PREP_TN
_QKV_TILES = PROJ_TILES * PROJ_TN // PREP_TN


def _prep_w_kernel(w_ref, qkv_ref, g_ref):
    s = pl.program_id(0)

    @pl.when(s == _PAD_TILE)
    def _():
        qkv_ref[...] = jnp.zeros_like(qkv_ref)

    @pl.when((s < _QKV_TILES) & (s != _PAD_TILE))
    def _():
        qkv_ref[...] = w_ref[...].astype(BF16)

    @pl.when(s >= _QKV_TILES)
    def _():
        g_ref[...] = w_ref[...].astype(BF16)


def _prep_w_in(w_in):
    k, n = w_in.shape
    n_g = n - (_QKV_TILES - 1) * PREP_TN
    steps = _QKV_TILES + n_g // PREP_TN
    col = lambda c: pl.BlockSpec((k, PREP_TN), c)
    return pl.pallas_call(
        _prep_w_kernel, grid=(steps,),
        in_specs=[col(lambda s: (0, jnp.where(s <= _PAD_TILE, jnp.minimum(s, _PAD_TILE - 1), s - 1)))],
        out_specs=[col(lambda s: (0, jnp.minimum(s, _QKV_TILES - 1))),
                   col(lambda s: (0, jnp.maximum(s - _QKV_TILES, 0)))],
        out_shape=[jax.ShapeDtypeStruct((k, _QKV_TILES * PREP_TN), BF16), jax.ShapeDtypeStruct((k, n_g), BF16)],
        compiler_params=_cparams(("arbitrary",)), name="prep_w",
    )(w_in)


def _rope_tables(pos):
    inv = (np.float32(1.0) / np.power(np.float32(ROPE_THETA), np.arange(HALF, dtype=np.float32) / np.float32(HALF)))
    ang = (np.asarray(pos, np.float32)[:, None] * inv[None, :]).astype(np.float32)
    cos, sin = np.cos(ang).astype(np.float32), np.sin(ang).astype(np.float32)
    cosf = np.concatenate([cos, cos, cos, cos], axis=1)
    sinf = np.concatenate([-sin, sin, -sin, sin], axis=1)
    return jnp.asarray(cosf), jnp.asarray(sinf)


def _rope_chunk(t, cos, sin, lo):
    sw = jnp.where(lo, pltpu.roll(t, LANES - HALF, 1), pltpu.roll(t, HALF, 1))
    return t * cos + sw * sin


def _proj_kernel(x_ref, w_ref, cos_ref, sin_ref,
                 qa_ref, kva_ref, kvta_ref, qb_ref,
                 kvb0_ref, kvt0_ref, kvb1_ref, kvt1_ref, kvb2_ref, kvt2_ref, xb_ref, acc_ref):
    j = pl.program_id(1)
    tm = x_ref.shape[0]

    @pl.when(j == 0)
    def _():
        xb_ref[...] = x_ref[...].astype(BF16)

    lane = lax.broadcasted_iota(jnp.int32, (1, LANES), 1)
    lo = (lane % HEAD_DIM) < HALF
    scale = HEAD_DIM ** -0.5
    kv_refs = ((kvb0_ref, kvt0_ref), (kvb1_ref, kvt1_ref), (kvb2_ref, kvt2_ref))

    def multiply(t):
        n_cols = 2 * A_KV_W if t == _T_KVA else PROJ_TN
        acc_ref[t % 2, :, 0:n_cols] = jnp.dot(xb_ref[...], w_ref[:, 0:n_cols], preferred_element_type=F32)

    def finish(t):
        def chunk(c, rope, mul=1.0):
            v = acc_ref[t % 2, :, c * LANES:(c + 1) * LANES]
            if rope:
                v = _rope_chunk(v, cos_ref[...], sin_ref[...], lo)
            return v if mul == 1.0 else v * mul

        if t < _T_KVA:
            for c in range(PROJ_TN // LANES):
                qa_ref[:, c * LANES:(c + 1) * LANES] = chunk(c, True, scale).astype(qa_ref.dtype)
        elif t == _T_KVA:
            k, v = chunk(0, True), chunk(1, False)
            kva_ref[:, 0:LANES] = k
            kva_ref[:, LANES:2 * LANES] = v
            kvta_ref[0] = k.T[:, tm - kvta_ref.shape[-1]:]
            kvta_ref[1] = v.T[:, tm - kvta_ref.shape[-1]:]
        elif t < _T_KVB:
            for c in range(B_PAIRS):
                qb_ref[c] = chunk(c, True, scale)
        else:
            is_v, g = divmod(t - _T_KVB, N_B_GROUPS)
            slab_ref, t_ref = kv_refs[g]
            for c in range(B_PAIRS):
                v = chunk(c, not is_v)
                slab_ref[c] = v
                t_ref[is_v, c * LANES:(c + 1) * LANES, :] = v.T[:, tm - t_ref.shape[-1]:]

    for t in range(PROJ_TILES + 1):
        @pl.when(j == t)
        def _(t=t):
            if t < PROJ_TILES:
                multiply(t)
            if t > 0:
                finish(t - 1)


def _proj(x2d, w_cat, cosf, sinf, q_dtype, tm, n_seq, seq, windows):
    m = x2d.shape[0]
    assert m == n_seq * seq and seq % tm == 0
    per_seq = seq // tm
    grid = (m // tm, PROJ_TILES + 1)

    def clamp(j, lo, n):
        return jnp.clip(j - lo, 0, n - 1)

    def window_spec(rows, win):
        lanes = min(win, tm)
        first = per_seq - win // lanes
        return pl.BlockSpec((None, 2, rows, lanes),
                            lambda i, j: (i // per_seq, 0, 0, jnp.maximum(i % per_seq - first, 0)))

    in_specs = [
        pl.BlockSpec((tm, D_MODEL), lambda i, j: (i, 0)),
        pl.BlockSpec((D_MODEL, PROJ_TN), lambda i, j: (0, jnp.minimum(j, PROJ_TILES - 1))),
        pl.BlockSpec((tm, LANES), lambda i, j: (i % per_seq, 0)),
        pl.BlockSpec((tm, LANES), lambda i, j: (i % per_seq, 0)),
    ]
    out_specs = [
        pl.BlockSpec((tm, PROJ_TN), lambda i, j: (i, clamp(j, _T_QA + 1, 2))),
        pl.BlockSpec((tm, 2 * A_KV_W), lambda i, j: (i, 0)),
        window_spec(A_KV_W, windows[0]),
        pl.BlockSpec((B_PAIRS, tm, LANES), lambda i, j: (clamp(j, _T_QB + 1, N_B_GROUPS), i, 0)),
    ]
    out_shape = [
        jax.ShapeDtypeStruct((m, A_Q_W), q_dtype),
        jax.ShapeDtypeStruct((m, 2 * A_KV_W), F32),
        jax.ShapeDtypeStruct((n_seq, 2, A_KV_W, windows[0]), F32),
        jax.ShapeDtypeStruct((N_B_GROUPS * B_PAIRS, m, LANES), F32),
    ]
    for g in range(N_B_GROUPS):
        is_v = lambda j, g=g: (j >= _T_KVB + N_B_GROUPS + g + 1).astype(jnp.int32)
        out_specs.append(pl.BlockSpec((B_PAIRS, tm, LANES), lambda i, j, is_v=is_v: (is_v(j), i, 0)))
        out_specs.append(window_spec(B_OUT_W, windows[1 + g]))
        out_shape.append(jax.ShapeDtypeStruct((2 * B_PAIRS, m, LANES), F32))
        out_shape.append(jax.ShapeDtypeStruct((n_seq, 2, B_OUT_W, windows[1 + g]), F32))
    return pl.pallas_call(
        _proj_kernel, grid=grid, in_specs=in_specs, out_specs=out_specs, out_shape=out_shape,
        scratch_shapes=[pltpu.VMEM((tm, D_MODEL), BF16), pltpu.VMEM((2, tm, PROJ_TN), F32)],
        compiler_params=_cparams(("arbitrary", "arbitrary")), name="proj",
    )(x2d, w_cat, cosf, sinf)


GATE_TN = 512


def _gates_kernel(*refs):
    n_cast = (len(refs) - 4) // 2
    x_ref, w_ref = refs[:2]
    g_ref, xb_ref = refs[2 + n_cast], refs[-1]

    @pl.when(pl.program_id(1) == 0)
    def _():
        xb_ref[...] = x_ref[...].astype(BF16)

    g_ref[...] = jnp.dot(xb_ref[...], w_ref[...], preferred_element_type=F32).astype(g_ref.dtype)
    for src, dst in zip(refs[2:2 + n_cast], refs[3 + n_cast:3 + 2 * n_cast]):
        dst[...] = src[...].astype(BF16)


def _gates(x2d, w_g, tm, cast=()):
    m = x2d.shape[0]
    n = w_g.shape[1]
    grid = (m // tm, n // GATE_TN)
    steps = grid[0] * grid[1]
    slab = lambda a: pl.BlockSpec((a.shape[0] // steps, a.shape[1]), lambda i, j: (i * grid[1] + j, 0))
    assert all(a.shape[0] % (16 * steps) == 0 for a in cast)
    out = pl.pallas_call(
        _gates_kernel, grid=grid,
        in_specs=[pl.BlockSpec((tm, D_MODEL), lambda i, j: (i, 0)),
                  pl.BlockSpec((D_MODEL, GATE_TN), lambda i, j: (0, j))] + [slab(a) for a in cast],
        out_specs=[pl.BlockSpec((tm, GATE_TN), lambda i, j: (i, j))] + [slab(a) for a in cast],
        out_shape=[jax.ShapeDtypeStruct((m, n), BF16)] + [jax.ShapeDtypeStruct(a.shape, BF16) for a in cast],
        scratch_shapes=[pltpu.VMEM((tm, D_MODEL), BF16)],
        compiler_params=_cparams(("arbitrary", "arbitrary")), name="gates",
    )(x2d, w_g, *cast)
    return out if cast else out[0]


def _band_mask(first_block, rows):
    qi = lax.broadcasted_iota(jnp.int32, (rows, 2 * BLOCK), 0) % BLOCK
    sj = lax.broadcasted_iota(jnp.int32, (rows, 2 * BLOCK), 1)
    dist = BLOCK + qi - sj
    lo = jnp.where(first_block, BLOCK, 0)
    return (dist >= 0) & (dist < B_KEYS) & (sj >= lo)


def _softmax_rows(s, valid, sink=None):
    s = jnp.where(valid, s, NEG)
    m = jnp.max(s, axis=-1, keepdims=True)
    if sink is not None:
        m = jnp.maximum(m, sink)
    p = jnp.exp(s - m)
    l = jnp.sum(p, axis=-1, keepdims=True)
    if sink is not None:
        l = l + jnp.exp(sink - m)
    return p, l, m


def _stack_halves(q128, lo):
    return jnp.concatenate([jnp.where(lo, q128, 0.0), jnp.where(lo, 0.0, q128)], axis=0)


def _attn_b_kernel(*refs, dil, has_prev):
    if has_prev:
        q_ref, kvc_ref, kvp_ref, o_ref, lse_ref = refs
    else:
        q_ref, kvc_ref, o_ref, lse_ref = refs
    lane = lax.broadcasted_iota(jnp.int32, (1, LANES), 1)
    lo = lane < HEAD_DIM
    if has_prev:
        valid = _band_mask(pl.program_id(1) == 0, 2 * BLOCK)
    else:
        qi = lax.broadcasted_iota(jnp.int32, (2 * BLOCK, BLOCK), 0) % BLOCK
        sj = lax.broadcasted_iota(jnp.int32, (2 * BLOCK, BLOCK), 1)
        valid = qi >= sj

    def residue(r):
        rows = pl.ds(r, BLOCK, stride=dil) if dil > 1 else pl.ds(0, BLOCK)
        for p in range(B_PAIRS):
            qs = _stack_halves(q_ref[p, rows, :], lo).astype(BF16)
            k2, v2 = kvc_ref[p, rows, :], kvc_ref[B_PAIRS + p, rows, :]
            if has_prev:
                k2 = jnp.concatenate([kvp_ref[p, rows, :], k2], axis=0)
                v2 = jnp.concatenate([kvp_ref[B_PAIRS + p, rows, :], v2], axis=0)
            s = lax.dot_general(qs, k2.astype(BF16), NT_DIMS, preferred_element_type=F32)
            pr, l, m = _softmax_rows(s, valid)
            o = jnp.dot(pr.astype(BF16), v2.astype(BF16), preferred_element_type=F32) * (1.0 / l)
            lse = jnp.broadcast_to(m + jnp.log(l), o.shape)
            o_ref[p, rows, :] = jnp.where(lo, o[0:BLOCK], o[BLOCK:2 * BLOCK])
            lse_ref[p, rows, :] = jnp.where(lo, lse[0:BLOCK], lse[BLOCK:2 * BLOCK])

    if dil == 1:
        residue(0)
    else:
        def body(r, c):
            residue(r)
            return c
        lax.fori_loop(0, dil, body, 0, unroll=4)


def _attn_b_prompt(qb, kvb, g, dil, n_seq, seq):
    t = qb.shape[1]
    rows = BLOCK * dil
    nb = seq // rows
    qv = qb.reshape(qb.shape[0], n_seq, seq, LANES)
    kvv = kvb.reshape(kvb.shape[0], n_seq, seq, LANES)
    in_specs = [pl.BlockSpec((B_PAIRS, None, rows, LANES), lambda n, b: (g, n, b, 0)),
                pl.BlockSpec((2 * B_PAIRS, None, rows, LANES), lambda n, b: (0, n, b, 0))]
    args = [qv, kvv]
    if nb > 1:
        in_specs.append(pl.BlockSpec((2 * B_PAIRS, None, rows, LANES), lambda n, b: (0, n, jnp.maximum(b - 1, 0), 0)))
        args.append(kvv)
    out_spec = pl.BlockSpec((B_PAIRS, None, rows, LANES), lambda n, b: (0, n, b, 0))
    o, lse = pl.pallas_call(
        functools.partial(_attn_b_kernel, dil=dil, has_prev=nb > 1),
        grid=(n_seq, nb), in_specs=in_specs, out_specs=[out_spec, out_spec],
        out_shape=[jax.ShapeDtypeStruct((B_PAIRS, n_seq, seq, LANES), F32)] * 2,
        compiler_params=_cparams(("parallel", "arbitrary")), name=f"attn_b{g}",
    )(*args)
    return o.reshape(B_PAIRS, t, LANES), lse.reshape(B_PAIRS, t, LANES)


def _dup_heads(x128, lo):
    xr = pltpu.roll(x128, HEAD_DIM, 1)
    return jnp.where(lo, x128, xr), jnp.where(lo, xr, x128)


def _attn_a_kernel(sink_ref, q_ref, kvc_ref, kvp_ref, o_ref):
    valid = _band_mask(pl.program_id(1) == 0, BLOCK)
    kv2 = jnp.concatenate([kvp_ref[...], kvc_ref[...]], axis=0)
    lane = lax.broadcasted_iota(jnp.int32, (1, LANES), 1)
    lo = lane < HEAD_DIM
    kdup = [t.astype(BF16) for t in _dup_heads(kv2[:, 0:LANES], lo)]
    vdup = [t.astype(BF16) for t in _dup_heads(kv2[:, LANES:2 * LANES], lo)]
    for p in range(A_Q_HEADS // 2):
        kh = (2 * p) // A_GROUP
        q128 = q_ref[:, p * LANES:(p + 1) * LANES]
        halves = []
        for e in range(2):
            qm = jnp.where(lo if e == 0 else jnp.logical_not(lo), q128, jnp.zeros_like(q128))
            s = lax.dot_general(qm, kdup[kh], NT_DIMS, preferred_element_type=F32)
            pe, l, _ = _softmax_rows(s, valid, sink_ref[2 * p + e])
            o = jnp.dot(pe.astype(BF16), vdup[kh], preferred_element_type=F32)
            halves.append(o * (1.0 / l))
        o_ref[:, p * LANES:(p + 1) * LANES] = jnp.where(lo, halves[0], halves[1]).astype(o_ref.dtype)


def _attn_a_prompt(sinks, qa, kva, n_seq, seq):
    t = qa.shape[0]
    nb = seq // BLOCK
    qv = qa.reshape(n_seq, seq, A_Q_W)
    kvv = kva.reshape(n_seq, seq, 2 * A_KV_W)
    in_specs = [
        pl.BlockSpec(memory_space=pltpu.SMEM),
        pl.BlockSpec((None, BLOCK, A_Q_W), lambda n, b: (n, b, 0)),
        pl.BlockSpec((None, BLOCK, 2 * A_KV_W), lambda n, b: (n, b, 0)),
        pl.BlockSpec((None, BLOCK, 2 * A_KV_W), lambda n, b: (n, jnp.maximum(b - 1, 0), 0)),
    ]
    o = pl.pallas_call(
        _attn_a_kernel, grid=(n_seq, nb), in_specs=in_specs,
        out_specs=pl.BlockSpec((None, BLOCK, A_Q_W), lambda n, b: (n, b, 0)),
        out_shape=jax.ShapeDtypeStruct((n_seq, seq, A_Q_W), BF16),
        compiler_params=_cparams(("parallel", "arbitrary")), name="attn_a",
    )(sinks, qv, kvv, kvv)
    return o.reshape(t, A_Q_W)


def _combine3(outs, lses):
    m = jnp.maximum(jnp.maximum(lses[0], lses[1]), lses[2])
    es = [jnp.exp(x - m) for x in lses]
    num = es[0] * outs[0] + es[1] * outs[1] + es[2] * outs[2]
    return num / (es[0] + es[1] + es[2])


T_NEW = 8
SEQS_PER_LANE_BLOCK = LANES // T_NEW
S_PAIRS = 2
SAMPLE_CHUNKS = B_PAIRS // S_PAIRS
ROLL_ROWS = 64


def _roll_cache(c_ref, new_rolled, o_ref, kv, row0, nrows):
    lb = c_ref.shape[-1]
    lane = lax.broadcasted_iota(jnp.int32, (1, LANES), 1)
    tail = lane >= LANES - T_NEW
    pieces = []
    for r0 in range(0, nrows, ROLL_ROWS):
        rs = slice(row0 + r0, row0 + r0 + ROLL_ROWS)
        x = c_ref[kv, rs, :]
        pieces.append(x.astype(BF16))
        t = pltpu.roll(x, lb - T_NEW, 1)
        if lb > LANES:
            o_ref[kv, rs, 0:lb - LANES] = t[:, 0:lb - LANES]
        o_ref[kv, rs, lb - LANES:lb] = jnp.where(tail, new_rolled[r0:r0 + ROLL_ROWS, :], t[:, lb - LANES:lb])
    return jnp.concatenate(pieces, axis=0)


def _sample_scores(qs, kt, nk, dil, lb):
    rows = qs.shape[0]
    s_c = jnp.dot(qs, kt, preferred_element_type=F32)
    s_n = jnp.dot(qs, nk.astype(BF16), preferred_element_type=F32)
    tok = lax.broadcasted_iota(jnp.int32, (rows, 1), 0) % T_NEW
    pos = lax.broadcasted_iota(jnp.int32, (rows, lb), 1)
    d_c = lb + tok - pos
    valid_c = ((d_c & (dil - 1)) == 0) & (d_c <= (B_KEYS - 1) * dil)
    lane = lax.broadcasted_iota(jnp.int32, (rows, LANES), 1)
    d_n = tok - (lane - (LANES - T_NEW))
    valid_n = (lane >= LANES - T_NEW) & (d_n >= 0) & ((d_n & (dil - 1)) == 0)
    return jnp.where(valid_c, s_c, NEG), jnp.where(valid_n, s_n, NEG)


def _sample_probs(qs, kt, nk, dil, sink=None):
    s_c, s_n = _sample_scores(qs, kt, nk, dil, kt.shape[-1])
    m = jnp.maximum(jnp.max(s_c, axis=-1, keepdims=True), jnp.max(s_n, axis=-1, keepdims=True))
    if sink is not None:
        m = jnp.maximum(m, sink)
    p_c, p_n = jnp.exp(s_c - m), jnp.exp(s_n - m)
    l = jnp.sum(p_c, axis=-1, keepdims=True) + jnp.sum(p_n, axis=-1, keepdims=True)
    if sink is not None:
        l = l + jnp.exp(sink - m)
    return p_c.astype(BF16), p_n.astype(BF16), l, m


def _sample_values(p_c, p_n, l, m, vt, nv):
    o = lax.dot_general(p_c, vt, NT_DIMS, preferred_element_type=F32)
    o = o + lax.dot_general(p_n, nv.astype(BF16), NT_DIMS, preferred_element_type=F32)
    return o * (1.0 / l), m + jnp.log(l)


def _sample_units(n, c, sink_ref, qa_ref, qb_ref, na_ref, n0_ref, n1_ref, n2_ref,
                  ca_ref, c0_ref, c1_ref, c2_ref,
                  oa_ref, ob_ref, ra_ref, r0_ref, r1_ref, r2_ref):
    shift = (LANES - T_NEW) - (n % SEQS_PER_LANE_BLOCK) * T_NEW
    lane = lax.broadcasted_iota(jnp.int32, (1, LANES), 1)
    lo = lane < HEAD_DIM
    pending, done = {}, {}
    units = []

    groups = ((c0_ref, n0_ref, r0_ref), (c1_ref, n1_ref, r1_ref), (c2_ref, n2_ref, r2_ref))
    for lp in range(S_PAIRS):
        for g, (c_ref, n_ref, r_ref) in enumerate(groups):
            def first(lp=lp, g=g, c_ref=c_ref, n_ref=n_ref, r_ref=r_ref):
                nrows = pl.ds(pl.multiple_of((c * S_PAIRS + lp) * LANES, LANES), LANES)
                nk = pltpu.roll(n_ref[0, nrows, :], shift, 1)
                nv = pltpu.roll(n_ref[1, nrows, :], shift, 1)
                qs = _stack_halves(qb_ref[g * B_PAIRS + c * S_PAIRS + lp], lo).astype(BF16)
                kt = _roll_cache(c_ref, nk, r_ref, 0, lp * LANES, LANES)
                vt = _roll_cache(c_ref, nv, r_ref, 1, lp * LANES, LANES)
                pending[lp, g] = _sample_probs(qs, kt, nk, B_PATTERNS[g][1]) + (vt, nv)

            def second(lp=lp, g=g):
                o, lse = _sample_values(*pending.pop((lp, g)))
                lse = jnp.broadcast_to(lse, o.shape)
                done[lp, g] = (jnp.where(lo, o[0:T_NEW], o[T_NEW:2 * T_NEW]),
                               jnp.where(lo, lse[0:T_NEW], lse[T_NEW:2 * T_NEW]))
                if g == N_B_GROUPS - 1:
                    outs, lses = zip(*[done.pop((lp, k)) for k in range(N_B_GROUPS)])
                    ob_ref[:, lp * LANES:(lp + 1) * LANES] = _combine3(outs, lses)

            units.append((first, second))

    hrow = lax.broadcasted_iota(jnp.int32, (A_GROUP * T_NEW, 1), 0) // T_NEW
    for kh in range(A_KV_HEADS):
        def first(kh=kh):
            if kh == 0:
                nk = pltpu.roll(na_ref[0], shift, 1)
                nv = pltpu.roll(na_ref[1], shift, 1)
                pending["a"] = (nk, nv, _roll_cache(ca_ref, nk, ra_ref, 0, 0, A_KV_W),
                                _roll_cache(ca_ref, nv, ra_ref, 1, 0, A_KV_W))
            nk, nv, kt_a, vt_a = pending["a"]
            hs = slice(kh * HEAD_DIM, (kh + 1) * HEAD_DIM)
            dup = lambda x: jnp.concatenate([x[hs], x[hs]], axis=0)
            tiles = [_stack_halves(qa_ref[:, (kh * (A_GROUP // 2) + p) * LANES:(kh * (A_GROUP // 2) + p + 1) * LANES], lo)
                     for p in range(A_GROUP // 2)]
            qs = jnp.concatenate(tiles, axis=0).astype(BF16)
            sink = jnp.zeros((A_GROUP * T_NEW, 1), F32)
            for hh in range(A_GROUP):
                sink = jnp.where(hrow == hh, sink_ref[kh * A_GROUP + hh], sink)
            pending["a", kh] = _sample_probs(qs, dup(kt_a), dup(nk), 1, sink) + (dup(vt_a), dup(nv))

        def second(kh=kh):
            o, _ = _sample_values(*pending.pop(("a", kh)))
            for p in range(A_GROUP // 2):
                o0 = o[(2 * p) * T_NEW:(2 * p + 1) * T_NEW, :]
                o1 = o[(2 * p + 1) * T_NEW:(2 * p + 2) * T_NEW, :]
                c0 = (kh * (A_GROUP // 2) + p) * LANES
                oa_ref[:, c0:c0 + LANES] = jnp.where(lo, o0, o1)

        units.append((first, second))
    return units


N_SAMPLE_IN, N_SAMPLE_OUT = 11, 6


def _sample_specs(idx, nt, cache_a, cache_bs):
    rows = S_PAIRS * LANES
    seq = lambda *g: idx(*g)[0]
    chunk = lambda *g: idx(*g)[1]
    lane_blk = lambda *g: idx(*g)[0] // SEQS_PER_LANE_BLOCK
    in_specs = [
        pl.BlockSpec(memory_space=pltpu.SMEM),
        pl.BlockSpec((T_NEW, A_Q_W), lambda *g: (seq(*g), 0)),
        pl.BlockSpec((N_B_GROUPS * B_PAIRS, T_NEW, LANES), lambda *g: (0, seq(*g), 0)),
        pl.BlockSpec((2, A_KV_W, LANES), lambda *g: (0, 0, lane_blk(*g))),
    ]
    in_specs += [pl.BlockSpec((2, B_OUT_W, LANES), lambda *g: (0, 0, lane_blk(*g)))] * N_B_GROUPS
    cache_specs = [pl.BlockSpec((None, 2, A_KV_W, cache_a.shape[-1]), lambda *g: (seq(*g), 0, 0, 0))]
    cache_specs += [pl.BlockSpec((None, 2, rows, cb.shape[-1]), lambda *g: (seq(*g), 0, chunk(*g), 0))
                    for cb in cache_bs]
    out_specs = [pl.BlockSpec((T_NEW, A_Q_W), lambda *g: (seq(*g), 0)),
                 pl.BlockSpec((T_NEW, rows), lambda *g: (seq(*g), chunk(*g)))] + cache_specs
    out_shape = [jax.ShapeDtypeStruct((nt, A_Q_W), F32), jax.ShapeDtypeStruct((nt, B_OUT_W), F32),
                 jax.ShapeDtypeStruct(cache_a.shape, F32)] + [jax.ShapeDtypeStruct(cb.shape, F32) for cb in cache_bs]
    return in_specs + cache_specs, out_specs, out_shape


def _layer_norm(z, g, b):
    mu = jnp.mean(z, axis=-1, keepdims=True)
    zc = z - mu
    var = jnp.mean(zc * zc, axis=-1, keepdims=True)
    return zc * lax.rsqrt(var + LN_EPS) * g + b


def _mix_kernel(x_ref, oa_ref, *refs):
    ob_refs, (g_ref, wa_ref, wb_ref, wo_ref, lng_ref, lnb_ref, h_ref) = refs[:-7], refs[-7:]
    if len(ob_refs) == 1:
        ob = ob_refs[0][...].astype(BF16)
    else:
        o_refs, l_refs = ob_refs[:N_B_GROUPS], ob_refs[N_B_GROUPS:]
        ob = jnp.concatenate([_combine3([r[p] for r in o_refs], [r[p] for r in l_refs]).astype(BF16)
                              for p in range(B_PAIRS)], axis=1)
    br_a = jnp.dot(oa_ref[...].astype(BF16), wa_ref[...], preferred_element_type=F32)
    br_b = jnp.dot(ob, wb_ref[...], preferred_element_type=F32)
    ga = jax.nn.sigmoid(g_ref[:, 0:D_MODEL].astype(F32))
    gb = jax.nn.sigmoid(g_ref[:, D_MODEL:2 * D_MODEL].astype(F32))
    merged = (ga * br_a + gb * br_b).astype(BF16)
    mix = jnp.dot(merged, wo_ref[...], preferred_element_type=F32)
    h_ref[...] = _layer_norm(ALPHA * x_ref[...] + mix, lng_ref[...], lnb_ref[...])


def _mix(x2d, oa, ob, gates, wa, wb, wo, ln_g, ln_b, tm):
    m = x2d.shape[0]
    row = lambda w: pl.BlockSpec((tm, w), lambda i: (i, 0))
    const = lambda a: pl.BlockSpec(a.shape, lambda i: (0, 0), pipeline_mode=pl.Buffered(1))
    if isinstance(ob, tuple):
        ob_args = [*ob[0], *ob[1]]
        ob_specs = [pl.BlockSpec((B_PAIRS, tm, LANES), lambda i: (0, i, 0))] * len(ob_args)
    else:
        ob_args, ob_specs = [ob], [row(B_OUT_W)]
    return pl.pallas_call(
        _mix_kernel, grid=(m // tm,),
        in_specs=[row(D_MODEL), row(A_Q_W), *ob_specs, row(2 * D_MODEL),
                  const(wa), const(wb), const(wo), const(ln_g), const(ln_b)],
        out_specs=row(D_MODEL), out_shape=jax.ShapeDtypeStruct((m, D_MODEL), F32),
        compiler_params=_cparams(("parallel",)), name="mix",
    )(x2d, oa, *ob_args, gates, wa, wb, wo, ln_g, ln_b)


FFN_OUT_CHUNK = 512
LN_ROWS = 128


N_FFN_IN = 7


def _ffn_kernel(*refs, with_sample, row_split):
    h_ref, w1_ref, b1_ref, w2_ref, b2_ref, lng_ref, lnb_ref = refs[:N_FFN_IN]
    if with_sample:
        sample_in = refs[N_FFN_IN:N_FFN_IN + N_SAMPLE_IN]
        y_ref = refs[N_FFN_IN + N_SAMPLE_IN]
        sample_out = refs[N_FFN_IN + N_SAMPLE_IN + 1:N_FFN_IN + N_SAMPLE_IN + 1 + N_SAMPLE_OUT]
    else:
        y_ref = refs[N_FFN_IN]
    hb_ref = refs[-1]
    f, r = pl.program_id(1), pl.program_id(2)
    n_f, n_r = pl.num_programs(1), pl.num_programs(2)
    sub = h_ref.shape[0] // row_split
    tf = w1_ref.shape[1]
    rows = pl.ds(pl.multiple_of(r * sub, sub), sub)

    @pl.when((f == 0) & (r == 0))
    def _():
        hb_ref[...] = h_ref[...].astype(BF16)
        y_ref[...] = jnp.zeros_like(y_ref)

    parts = {}

    def up(k):
        cs = slice(k * (tf // 2), (k + 1) * (tf // 2))
        t = jnp.dot(hb_ref[rows, :], w1_ref[:, cs], preferred_element_type=F32) + b1_ref[:, cs]
        parts[k] = jnp.square(jnp.maximum(t, 0.0)).astype(BF16)

    def down(k):
        if "u" not in parts:
            parts["u"] = jnp.concatenate([parts.pop(0), parts.pop(1)], axis=1)
        cs = slice(k * FFN_OUT_CHUNK, (k + 1) * FFN_OUT_CHUNK)
        y_ref[rows, cs] += jnp.dot(parts["u"], w2_ref[:, cs], preferred_element_type=F32)

    pieces = [functools.partial(up, 0), functools.partial(up, 1)]
    pieces += [functools.partial(down, k) for k in range(D_MODEL // FFN_OUT_CHUNK)]
    if with_sample:
        step = (pl.program_id(0) * n_f + f) * n_r + r
        units = _sample_units(step // SAMPLE_CHUNKS, step % SAMPLE_CHUNKS, *sample_in, *sample_out)
        units[0][0]()
        for k in range(max(len(units), len(pieces))):
            if k + 1 < len(units):
                units[k + 1][0]()
            if k < len(pieces):
                pieces[k]()
            if k < len(units):
                units[k][1]()
    else:
        for piece in pieces:
            piece()

    @pl.when((f == n_f - 1) & (r == n_r - 1))
    def _():
        def ln_rows(k, carry):
            rs = pl.ds(pl.multiple_of(k * LN_ROWS, LN_ROWS), LN_ROWS)
            z = ALPHA * h_ref[rs, :] + y_ref[rs, :] + b2_ref[...]
            y_ref[rs, :] = _layer_norm(z, lng_ref[...], lnb_ref[...])
            return carry
        lax.fori_loop(0, h_ref.shape[0] // LN_ROWS, ln_rows, 0)


def _ffn(h, w1, b1, w2, b2, ln_g, ln_b, tm, tf, row_split, sample_args=None):
    m = h.shape[0]
    grid = (m // tm, D_FF // tf, row_split)
    const = lambda a: pl.BlockSpec(a.shape, lambda i, f, r: (0, 0))
    once = pl.Buffered(1) if row_split > 1 else None
    in_specs = [pl.BlockSpec((tm, D_MODEL), lambda i, f, r: (i, 0), pipeline_mode=once),
                pl.BlockSpec((D_MODEL, tf), lambda i, f, r: (0, f)),
                pl.BlockSpec((1, tf), lambda i, f, r: (0, f)),
                pl.BlockSpec((tf, D_MODEL), lambda i, f, r: (f, 0)),
                const(b2), const(ln_g), const(ln_b)]
    out_specs = [pl.BlockSpec((tm, D_MODEL), lambda i, f, r: (i, 0), pipeline_mode=once)]
    out_shape = [jax.ShapeDtypeStruct((m, D_MODEL), F32)]
    args = [h, w1, b1, w2, b2, ln_g, ln_b]
    if sample_args is not None:
        sinks, qa, qb, new_a, new_bs, cache_a, cache_bs = sample_args
        nt = qa.shape[0]
        assert grid[0] * grid[1] * grid[2] == (nt // T_NEW) * SAMPLE_CHUNKS

        def idx(i, f, r):
            step = (i * grid[1] + f) * grid[2] + r
            return step // SAMPLE_CHUNKS, step % SAMPLE_CHUNKS

        s_in, s_out, s_shape = _sample_specs(idx, nt, cache_a, cache_bs)
        in_specs += s_in
        out_specs += s_out
        out_shape += s_shape
        args += [sinks, qa, qb, new_a, *new_bs, cache_a, *cache_bs]
    out = pl.pallas_call(
        functools.partial(_ffn_kernel, with_sample=sample_args is not None, row_split=row_split),
        grid=grid, in_specs=in_specs, out_specs=out_specs, out_shape=out_shape,
        scratch_shapes=[pltpu.VMEM((tm, D_MODEL), BF16)],
        compiler_params=_cparams(("arbitrary", "arbitrary", "arbitrary")),
        name="ffn_sample" if sample_args is not None else "ffn",
    )(*args)
    return out if sample_args is not None else out[0]


def _to_cache_layout(kvt, heads):
    n, _, _, rows = kvt.shape
    return jnp.transpose(kvt.reshape(n, 2, heads, HEAD_DIM, rows), (0, 4, 1, 2, 3))[None]


def _from_cache_layout(cache):
    _, n, rows, _, heads, _ = cache.shape
    return jnp.transpose(cache[0], (0, 2, 3, 4, 1)).reshape(n, 2, heads * HEAD_DIM, rows)


def kernel(x_prompt, x_sample, cache_a_kv, cache_b1_kv, cache_b2_kv, cache_b3_kv, w_in, a_sinks,
           w_branch_a, w_branch_b, w_out, ln1_g, ln1_b, w_ff1, b_ff1, w_ff2, b_ff2, ln2_g, ln2_b):
    n_p, seq, _ = x_prompt.shape
    n_s, t_new, _ = x_sample.shape
    assert t_new == T_NEW and w_in.shape[0] == 1
    b_caches = (cache_b1_kv, cache_b2_kv, cache_b3_kv)

    w_cat, w_g = _prep_w_in(w_in[0])
    wa, wb, wo = w_branch_a[0].astype(BF16), w_branch_b[0].astype(BF16), w_out[0].astype(BF16)
    sinks = a_sinks[0].reshape(A_Q_HEADS).astype(F32)
    row = lambda v: v[0].reshape(1, -1)

    def finish(x2d, oa, ob, gates, sample_args=None):
        h = _mix(x2d, oa, ob, gates, wa, wb, wo, row(ln1_g), row(ln1_b), tm=512)
        tm, row_split = (1024, 2) if sample_args is not None else (1024, 1)
        return _ffn(h, w1, row(b_ff1), w2, row(b_ff2), row(ln2_g), row(ln2_b), tm, 512, row_split, sample_args)

    xp = x_prompt.reshape(n_p * seq, D_MODEL)
    cos_p, sin_p = _rope_tables(np.arange(seq))
    windows = (min(A_WINDOW, seq),) + tuple(min(win, seq) for win, _ in B_PATTERNS)
    (qa, kva, kvt_a, qb, kvb0, kvt0, kvb1, kvt1, kvb2, kvt2) = _proj(
        xp, w_cat, cos_p, sin_p, BF16, 512, n_p, seq, windows)
    gates, w1, w2 = _gates(xp, w_g, 1024, cast=(w_ff1[0], w_ff2[0]))
    kvbs, kvts = (kvb0, kvb1, kvb2), (kvt0, kvt1, kvt2)
    nt = n_s * T_NEW
    xs = x_sample.reshape(nt, D_MODEL)
    cos_s, sin_s = _rope_tables(PAST_LEN + np.arange(nt) % T_NEW)
    (qa_s, _, kvt_a_s, qb_s, _, kvt0_s, _, kvt1_s, _, kvt2_s) = _proj(
        xs, w_cat, cos_s, sin_s, F32, 512, 1, nt, (nt,) * (1 + N_B_GROUPS))
    gates_s = _gates(xs, w_g, tm=nt)

    o_a = _attn_a_prompt(sinks, qa, kva, n_p, seq)
    outs, lses = [], []
    for g, (win, dil) in enumerate(B_PATTERNS):
        o, lse = _attn_b_prompt(qb, kvbs[g], g, dil, n_p, seq)
        outs.append(o)
        lses.append(lse)
    o_b = (tuple(outs), tuple(lses))
    sample_args = (sinks, qa_s, qb_s, kvt_a_s[0], [kvt0_s[0], kvt1_s[0], kvt2_s[0]],
                   _from_cache_layout(cache_a_kv), [_from_cache_layout(cb) for cb in b_caches])
    y_p, o_a_s, o_b_s, r_a, r_b1, r_b2, r_b3 = finish(xp, o_a, o_b, gates, sample_args)
    y_p = y_p.reshape(n_p, seq, D_MODEL)
    a_kv_p = _to_cache_layout(kvt_a, A_KV_HEADS)
    b_kv_p = [_to_cache_layout(kvts[g], B_HEADS) for g in range(N_B_GROUPS)]

    y_s = finish(xs, o_a_s, o_b_s, gates_s).reshape(n_s, T_NEW, D_MODEL)
    a_kv_s = _to_cache_layout(r_a, A_KV_HEADS)
    b_kv_s = [_to_cache_layout(r, B_HEADS) for r in (r_b1, r_b2, r_b3)]

    return (y_p, y_s, a_kv_p, a_kv_s,
            b_kv_p[0], b_kv_s[0], b_kv_p[1], b_kv_s[1], b_kv_p[2], b_kv_s[2])
```

```python
import functools

import numpy as np
import jax
import jax.numpy as jnp
from jax import lax
from jax.experimental import pallas as pl
from jax.experimental.pallas import tpu as pltpu

D_MODEL = 2048
HEAD_DIM = 64
HALF = HEAD_DIM // 2
A_Q_HEADS = 16
A_KV_HEADS = 2
A_GROUP = A_Q_HEADS // A_KV_HEADS
A_WINDOW = 128
B_PATTERNS = ((128, 1), (512, 4), (2048, 16))
N_B_GROUPS = len(B_PATTERNS)
B_HEADS = 8
B_KEYS = 128
BLOCK = 128
D_FF = 4 * D_MODEL
ROPE_THETA = 10000.0
ALPHA = 2.0 ** 0.25
LN_EPS = 1e-5
NEG = -1e30
PAST_LEN = 8192

A_Q_W = A_Q_HEADS * HEAD_DIM
A_KV_W = A_KV_HEADS * HEAD_DIM
B_W = N_B_GROUPS * B_HEADS * HEAD_DIM
B_OUT_W = B_HEADS * HEAD_DIM
LANES = 128
B_PAIRS = B_OUT_W // LANES
VMEM_LIMIT = 60 * 1024 * 1024

F32 = jnp.float32
BF16 = jnp.bfloat16
NT_DIMS = (((1,), (1,)), ((), ()))


def _cparams(sem):
    return pltpu.CompilerParams(dimension_semantics=sem, vmem_limit_bytes=VMEM_LIMIT)


PROJ_TN = 512
_T_QA, _T_KVA, _T_QB, _T_KVB = 0, 2, 3, 6
PROJ_TILES = 12


PREP_TN = 2 * A_KV_W
_PAD_TILE = (A_Q_W + 2 * A_KV_W) // PREP_TN
_QKV_TILES = PROJ_TILES * PROJ_TN // PREP_TN


def _prep_w_kernel(w_ref, qkv_ref, g_ref):
    s = pl.program_id(0)

    @pl.when(s == _PAD_TILE)
    def _():
        qkv_ref[...] = jnp.zeros_like(qkv_ref)

    @pl.when((s < _QKV_TILES) & (s != _PAD_TILE))
    def _():
        qkv_ref[...] = w_ref[...].astype(BF16)

    @pl.when(s >= _QKV_TILES)
    def _():
        g_ref[...] = w_ref[...].astype(BF16)


def _prep_w_in(w_in):
    k, n = w_in.shape
    n_g = n - (_QKV_TILES - 1) * PREP_TN
    steps = _QKV_TILES + n_g // PREP_TN
    col = lambda c: pl.BlockSpec((k, PREP_TN), c)
    return pl.pallas_call(
        _prep_w_kernel, grid=(steps,),
        in_specs=[col(lambda s: (0, jnp.where(s <= _PAD_TILE, jnp.minimum(s, _PAD_TILE - 1), s - 1)))],
        out_specs=[col(lambda s: (0, jnp.minimum(s, _QKV_TILES - 1))),
                   col(lambda s: (0, jnp.maximum(s - _QKV_TILES, 0)))],
        out_shape=[jax.ShapeDtypeStruct((k, _QKV_TILES * PREP_TN), BF16), jax.ShapeDtypeStruct((k, n_g), BF16)],
        compiler_params=_cparams(("arbitrary",)), name="prep_w",
    )(w_in)


def _rope_tables(pos):
    inv = (np.float32(1.0) / np.power(np.float32(ROPE_THETA), np.arange(HALF, dtype=np.float32) / np.float32(HALF)))
    ang = (np.asarray(pos, np.float32)[:, None] * inv[None, :]).astype(np.float32)
    cos, sin = np.cos(ang).astype(np.float32), np.sin(ang).astype(np.float32)
    cosf = np.concatenate([cos, cos, cos, cos], axis=1)
    sinf = np.concatenate([-sin, sin, -sin, sin], axis=1)
    return jnp.asarray(cosf), jnp.asarray(sinf)


def _rope_chunk(t, cos, sin, lo):
    sw = jnp.where(lo, pltpu.roll(t, LANES - HALF, 1), pltpu.roll(t, HALF, 1))
    return t * cos + sw * sin


def _proj_kernel(x_ref, w_ref, cos_ref, sin_ref,
                 qa_ref, kva_ref, kvta_ref, qb_ref,
                 kvb0_ref, kvt0_ref, kvb1_ref, kvt1_ref, kvb2_ref, kvt2_ref, xb_ref, acc_ref):
    j = pl.program_id(1)
    tm = x_ref.shape[0]

    @pl.when(j == 0)
    def _():
        xb_ref[...] = x_ref[...].astype(BF16)

    lane = lax.broadcasted_iota(jnp.int32, (1, LANES), 1)
    lo = (lane % HEAD_DIM) < HALF
    scale = HEAD_DIM ** -0.5
    kv_refs = ((kvb0_ref, kvt0_ref), (kvb1_ref, kvt1_ref), (kvb2_ref, kvt2_ref))

    def multiply(t):
        n_cols = 2 * A_KV_W if t == _T_KVA else PROJ_TN
        acc_ref[t % 2, :, 0:n_cols] = jnp.dot(xb_ref[...], w_ref[:, 0:n_cols], preferred_element_type=F32)

    def finish(t):
        def chunk(c, rope, mul=1.0):
            v = acc_ref[t % 2, :, c * LANES:(c + 1) * LANES]
            if rope:
                v = _rope_chunk(v, cos_ref[...], sin_ref[...], lo)
            return v if mul == 1.0 else v * mul

        if t < _T_KVA:
            for c in range(PROJ_TN // LANES):
                qa_ref[:, c * LANES:(c + 1) * LANES] = chunk(c, True, scale).astype(qa_ref.dtype)
        elif t == _T_KVA:
            k, v = chunk(0, True), chunk(1, False)
            kva_ref[:, 0:LANES] = k
            kva_ref[:, LANES:2 * LANES] = v
            kvta_ref[0] = k.T[:, tm - kvta_ref.shape[-1]:]
            kvta_ref[1] = v.T[:, tm - kvta_ref.shape[-1]:]
        elif t < _T_KVB:
            for c in range(B_PAIRS):
                qb_ref[c] = chunk(c, True, scale)
        else:
            is_v, g = divmod(t - _T_KVB, N_B_GROUPS)
            slab_ref, t_ref = kv_refs[g]
            for c in range(B_PAIRS):
                v = chunk(c, not is_v)
                slab_ref[c] = v
                t_ref[is_v, c * LANES:(c + 1) * LANES, :] = v.T[:, tm - t_ref.shape[-1]:]

    for t in range(PROJ_TILES + 1):
        @pl.when(j == t)
        def _(t=t):
            if t < PROJ_TILES:
                multiply(t)
            if t > 0:
                finish(t - 1)


def _proj(x2d, w_cat, cosf, sinf, q_dtype, tm, n_seq, seq, windows):
    m = x2d.shape[0]
    assert m == n_seq * seq and seq % tm == 0
    per_seq = seq // tm
    grid = (m // tm, PROJ_TILES + 1)

    def clamp(j, lo, n):
        return jnp.clip(j - lo, 0, n - 1)

    def window_spec(rows, win):
        lanes = min(win, tm)
        first = per_seq - win // lanes
        return pl.BlockSpec((None, 2, rows, lanes),
                            lambda i, j: (i // per_seq, 0, 0, jnp.maximum(i % per_seq - first, 0)),
                            pipeline_mode=once)

    once = pl.Buffered(1) if tm > 512 else None

    in_specs = [
        pl.BlockSpec((tm, D_MODEL), lambda i, j: (i, 0)),
        pl.BlockSpec((D_MODEL, PROJ_TN), lambda i, j: (0, jnp.minimum(j, PROJ_TILES - 1))),
        pl.BlockSpec((tm, LANES), lambda i, j: (i % per_seq, 0)),
        pl.BlockSpec((tm, LANES), lambda i, j: (i % per_seq, 0)),
    ]
    out_specs = [
        pl.BlockSpec((tm, PROJ_TN), lambda i, j: (i, clamp(j, _T_QA + 1, 2))),
        pl.BlockSpec((tm, 2 * A_KV_W), lambda i, j: (i, 0)),
        window_spec(A_KV_W, windows[0]),
        pl.BlockSpec((B_PAIRS, tm, LANES), lambda i, j: (clamp(j, _T_QB + 1, N_B_GROUPS), i, 0)),
    ]
    out_shape = [
        jax.ShapeDtypeStruct((m, A_Q_W), q_dtype),
        jax.ShapeDtypeStruct((m, 2 * A_KV_W), F32),
        jax.ShapeDtypeStruct((n_seq, 2, A_KV_W, windows[0]), F32),
        jax.ShapeDtypeStruct((N_B_GROUPS * B_PAIRS, m, LANES), F32),
    ]
    for g in range(N_B_GROUPS):
        is_v = lambda j, g=g: (j >= _T_KVB + N_B_GROUPS + g + 1).astype(jnp.int32)
        out_specs.append(pl.BlockSpec((B_PAIRS, tm, LANES), lambda i, j, is_v=is_v: (is_v(j), i, 0),
                                      pipeline_mode=once))
        out_specs.append(window_spec(B_OUT_W, windows[1 + g]))
        out_shape.append(jax.ShapeDtypeStruct((2 * B_PAIRS, m, LANES), F32))
        out_shape.append(jax.ShapeDtypeStruct((n_seq, 2, B_OUT_W, windows[1 + g]), F32))
    return pl.pallas_call(
        _proj_kernel, grid=grid, in_specs=in_specs, out_specs=out_specs, out_shape=out_shape,
        scratch_shapes=[pltpu.VMEM((tm, D_MODEL), BF16), pltpu.VMEM((2, tm, PROJ_TN), F32)],
        compiler_params=_cparams(("arbitrary", "arbitrary")), name="proj",
    )(x2d, w_cat, cosf, sinf)


GATE_TN = 512


def _gates_kernel(*refs):
    n_cast = (len(refs) - 4) // 2
    x_ref, w_ref = refs[:2]
    g_ref, xb_ref = refs[2 + n_cast], refs[-1]

    @pl.when(pl.program_id(1) == 0)
    def _():
        xb_ref[...] = x_ref[...].astype(BF16)

    g_ref[...] = jnp.dot(xb_ref[...], w_ref[...], preferred_element_type=F32).astype(g_ref.dtype)
    for src, dst in zip(refs[2:2 + n_cast], refs[3 + n_cast:3 + 2 * n_cast]):
        dst[...] = src[...].astype(BF16)


def _gates(x2d, w_g, tm, cast=()):
    m = x2d.shape[0]
    n = w_g.shape[1]
    grid = (m // tm, n // GATE_TN)
    steps = grid[0] * grid[1]
    slab = lambda a: pl.BlockSpec((a.shape[0] // steps, a.shape[1]), lambda i, j: (i * grid[1] + j, 0))
    assert all(a.shape[0] % (16 * steps) == 0 for a in cast)
    out = pl.pallas_call(
        _gates_kernel, grid=grid,
        in_specs=[pl.BlockSpec((tm, D_MODEL), lambda i, j: (i, 0)),
                  pl.BlockSpec((D_MODEL, GATE_TN), lambda i, j: (0, j))] + [slab(a) for a in cast],
        out_specs=[pl.BlockSpec((tm, GATE_TN), lambda i, j: (i, j))] + [slab(a) for a in cast],
        out_shape=[jax.ShapeDtypeStruct((m, n), BF16)] + [jax.ShapeDtypeStruct(a.shape, BF16) for a in cast],
        scratch_shapes=[pltpu.VMEM((tm, D_MODEL), BF16)],
        compiler_params=_cparams(("arbitrary", "arbitrary")), name="gates",
    )(x2d, w_g, *cast)
    return out if cast else out[0]


def _band_mask(first_block, rows):
    qi = lax.broadcasted_iota(jnp.int32, (rows, 2 * BLOCK), 0) % BLOCK
    sj = lax.broadcasted_iota(jnp.int32, (rows, 2 * BLOCK), 1)
    dist = BLOCK + qi - sj
    lo = jnp.where(first_block, BLOCK, 0)
    return (dist >= 0) & (dist < B_KEYS) & (sj >= lo)


def _softmax_rows(s, valid, sink=None):
    s = jnp.where(valid, s, NEG)
    m = jnp.max(s, axis=-1, keepdims=True)
    if sink is not None:
        m = jnp.maximum(m, sink)
    p = jnp.exp(s - m)
    l = jnp.sum(p, axis=-1, keepdims=True)
    if sink is not None:
        l = l + jnp.exp(sink - m)
    return p, l, m


def _stack_halves(q128, lo):
    return jnp.concatenate([jnp.where(lo, q128, 0.0), jnp.where(lo, 0.0, q128)], axis=0)


def _attn_b_kernel(*refs, dil, has_prev):
    if has_prev:
        q_ref, kvc_ref, kvp_ref, o_ref, lse_ref = refs
    else:
        q_ref, kvc_ref, o_ref, lse_ref = refs
    lane = lax.broadcasted_iota(jnp.int32, (1, LANES), 1)
    lo = lane < HEAD_DIM
    if has_prev:
        valid = _band_mask(pl.program_id(1) == 0, 2 * BLOCK)
    else:
        qi = lax.broadcasted_iota(jnp.int32, (2 * BLOCK, BLOCK), 0) % BLOCK
        sj = lax.broadcasted_iota(jnp.int32, (2 * BLOCK, BLOCK), 1)
        valid = qi >= sj

    def residue(r):
        rows = pl.ds(r, BLOCK, stride=dil) if dil > 1 else pl.ds(0, BLOCK)
        for p in range(B_PAIRS):
            qs = _stack_halves(q_ref[p, rows, :], lo).astype(BF16)
            k2, v2 = kvc_ref[p, rows, :], kvc_ref[B_PAIRS + p, rows, :]
            if has_prev:
                k2 = jnp.concatenate([kvp_ref[p, rows, :], k2], axis=0)
                v2 = jnp.concatenate([kvp_ref[B_PAIRS + p, rows, :], v2], axis=0)
            s = lax.dot_general(qs, k2.astype(BF16), NT_DIMS, preferred_element_type=F32)
            pr, l, m = _softmax_rows(s, valid)
            o = jnp.dot(pr.astype(BF16), v2.astype(BF16), preferred_element_type=F32) * (1.0 / l)
            lse = jnp.broadcast_to(m + jnp.log(l), o.shape)
            o_ref[p, rows, :] = jnp.where(lo, o[0:BLOCK], o[BLOCK:2 * BLOCK])
            lse_ref[p, rows, :] = jnp.where(lo, lse[0:BLOCK], lse[BLOCK:2 * BLOCK])

    if dil == 1:
        residue(0)
    else:
        def body(r, c):
            residue(r)
            return c
        lax.fori_loop(0, dil, body, 0, unroll=4)


def _attn_b_prompt(qb, kvb, g, dil, n_seq, seq):
    t = qb.shape[1]
    rows = BLOCK * dil
    nb = seq // rows
    qv = qb.reshape(qb.shape[0], n_seq, seq, LANES)
    kvv = kvb.reshape(kvb.shape[0], n_seq, seq, LANES)
    in_specs = [pl.BlockSpec((B_PAIRS, None, rows, LANES), lambda n, b: (g, n, b, 0)),
                pl.BlockSpec((2 * B_PAIRS, None, rows, LANES), lambda n, b: (0, n, b, 0))]
    args = [qv, kvv]
    if nb > 1:
        in_specs.append(pl.BlockSpec((2 * B_PAIRS, None, rows, LANES), lambda n, b: (0, n, jnp.maximum(b - 1, 0), 0)))
        args.append(kvv)
    out_spec = pl.BlockSpec((B_PAIRS, None, rows, LANES), lambda n, b: (0, n, b, 0))
    o, lse = pl.pallas_call(
        functools.partial(_attn_b_kernel, dil=dil, has_prev=nb > 1),
        grid=(n_seq, nb), in_specs=in_specs, out_specs=[out_spec, out_spec],
        out_shape=[jax.ShapeDtypeStruct((B_PAIRS, n_seq, seq, LANES), F32)] * 2,
        compiler_params=_cparams(("parallel", "arbitrary")), name=f"attn_b{g}",
    )(*args)
    return o.reshape(B_PAIRS, t, LANES), lse.reshape(B_PAIRS, t, LANES)


def _dup_heads(x128, lo):
    xr = pltpu.roll(x128, HEAD_DIM, 1)
    return jnp.where(lo, x128, xr), jnp.where(lo, xr, x128)


def _attn_a_kernel(sink_ref, q_ref, kvc_ref, kvp_ref, o_ref):
    valid = _band_mask(pl.program_id(1) == 0, BLOCK)
    kv2 = jnp.concatenate([kvp_ref[...], kvc_ref[...]], axis=0)
    lane = lax.broadcasted_iota(jnp.int32, (1, LANES), 1)
    lo = lane < HEAD_DIM
    kdup = [t.astype(BF16) for t in _dup_heads(kv2[:, 0:LANES], lo)]
    vdup = [t.astype(BF16) for t in _dup_heads(kv2[:, LANES:2 * LANES], lo)]
    for p in range(A_Q_HEADS // 2):
        kh = (2 * p) // A_GROUP
        q128 = q_ref[:, p * LANES:(p + 1) * LANES]
        halves = []
        for e in range(2):
            qm = jnp.where(lo if e == 0 else jnp.logical_not(lo), q128, jnp.zeros_like(q128))
            s = lax.dot_general(qm, kdup[kh], NT_DIMS, preferred_element_type=F32)
            pe, l, _ = _softmax_rows(s, valid, sink_ref[2 * p + e])
            o = jnp.dot(pe.astype(BF16), vdup[kh], preferred_element_type=F32)
            halves.append(o * (1.0 / l))
        o_ref[:, p * LANES:(p + 1) * LANES] = jnp.where(lo, halves[0], halves[1]).astype(o_ref.dtype)


def _attn_a_prompt(sinks, qa, kva, n_seq, seq):
    t = qa.shape[0]
    nb = seq // BLOCK
    qv = qa.reshape(n_seq, seq, A_Q_W)
    kvv = kva.reshape(n_seq, seq, 2 * A_KV_W)
    in_specs = [
        pl.BlockSpec(memory_space=pltpu.SMEM),
        pl.BlockSpec((None, BLOCK, A_Q_W), lambda n, b: (n, b, 0)),
        pl.BlockSpec((None, BLOCK, 2 * A_KV_W), lambda n, b: (n, b, 0)),
        pl.BlockSpec((None, BLOCK, 2 * A_KV_W), lambda n, b: (n, jnp.maximum(b - 1, 0), 0)),
    ]
    o = pl.pallas_call(
        _attn_a_kernel, grid=(n_seq, nb), in_specs=in_specs,
        out_specs=pl.BlockSpec((None, BLOCK, A_Q_W), lambda n, b: (n, b, 0)),
        out_shape=jax.ShapeDtypeStruct((n_seq, seq, A_Q_W), BF16),
        compiler_params=_cparams(("parallel", "arbitrary")), name="attn_a",
    )(sinks, qv, kvv, kvv)
    return o.reshape(t, A_Q_W)


def _combine3(outs, lses):
    m = jnp.maximum(jnp.maximum(lses[0], lses[1]), lses[2])
    es = [jnp.exp(x - m) for x in lses]
    num = es[0] * outs[0] + es[1] * outs[1] + es[2] * outs[2]
    return num / (es[0] + es[1] + es[2])


T_NEW = 8
SEQS_PER_LANE_BLOCK = LANES // T_NEW
S_PAIRS = 2
SAMPLE_CHUNKS = B_PAIRS // S_PAIRS
ROLL_ROWS = 64


def _roll_cache(c_ref, new_rolled, o_ref, kv, row0, nrows):
    lb = c_ref.shape[-1]
    lane = lax.broadcasted_iota(jnp.int32, (1, LANES), 1)
    tail = lane >= LANES - T_NEW
    pieces = []
    for r0 in range(0, nrows, ROLL_ROWS):
        rs = slice(row0 + r0, row0 + r0 + ROLL_ROWS)
        x = c_ref[kv, rs, :]
        pieces.append(x.astype(BF16))
        t = pltpu.roll(x, lb - T_NEW, 1)
        if lb > LANES:
            o_ref[kv, rs, 0:lb - LANES] = t[:, 0:lb - LANES]
        o_ref[kv, rs, lb - LANES:lb] = jnp.where(tail, new_rolled[r0:r0 + ROLL_ROWS, :], t[:, lb - LANES:lb])
    return jnp.concatenate(pieces, axis=0)


def _sample_scores(qs, kt, nk, dil, lb):
    rows = qs.shape[0]
    s_c = jnp.dot(qs, kt, preferred_element_type=F32)
    s_n = jnp.dot(qs, nk.astype(BF16), preferred_element_type=F32)
    tok = lax.broadcasted_iota(jnp.int32, (rows, 1), 0) % T_NEW
    pos = lax.broadcasted_iota(jnp.int32, (rows, lb), 1)
    d_c = lb + tok - pos
    valid_c = ((d_c & (dil - 1)) == 0) & (d_c <= (B_KEYS - 1) * dil)
    lane = lax.broadcasted_iota(jnp.int32, (rows, LANES), 1)
    d_n = tok - (lane - (LANES - T_NEW))
    valid_n = (lane >= LANES - T_NEW) & (d_n >= 0) & ((d_n & (dil - 1)) == 0)
    return jnp.where(valid_c, s_c, NEG), jnp.where(valid_n, s_n, NEG)


def _sample_probs(qs, kt, nk, dil, sink=None):
    s_c, s_n = _sample_scores(qs, kt, nk, dil, kt.shape[-1])
    m = jnp.maximum(jnp.max(s_c, axis=-1, keepdims=True), jnp.max(s_n, axis=-1, keepdims=True))
    if sink is not None:
        m = jnp.maximum(m, sink)
    p_c, p_n = jnp.exp(s_c - m), jnp.exp(s_n - m)
    l = jnp.sum(p_c, axis=-1, keepdims=True) + jnp.sum(p_n, axis=-1, keepdims=True)
    if sink is not None:
        l = l + jnp.exp(sink - m)
    return p_c.astype(BF16), p_n.astype(BF16), l, m


def _sample_values(p_c, p_n, l, m, vt, nv):
    o = lax.dot_general(p_c, vt, NT_DIMS, preferred_element_type=F32)
    o = o + lax.dot_general(p_n, nv.astype(BF16), NT_DIMS, preferred_element_type=F32)
    return o * (1.0 / l), m + jnp.log(l)


def _sample_units(n, c, sink_ref, qa_ref, qb_ref, na_ref, n0_ref, n1_ref, n2_ref,
                  ca_ref, c0_ref, c1_ref, c2_ref,
                  oa_ref, ob_ref, ra_ref, r0_ref, r1_ref, r2_ref):
    shift = (LANES - T_NEW) - (n % SEQS_PER_LANE_BLOCK) * T_NEW
    lane = lax.broadcasted_iota(jnp.int32, (1, LANES), 1)
    lo = lane < HEAD_DIM
    pending, done = {}, {}
    units = []

    groups = ((c0_ref, n0_ref, r0_ref), (c1_ref, n1_ref, r1_ref), (c2_ref, n2_ref, r2_ref))
    for lp in range(S_PAIRS):
        for g, (c_ref, n_ref, r_ref) in enumerate(groups):
            def first(lp=lp, g=g, c_ref=c_ref, n_ref=n_ref, r_ref=r_ref):
                nrows = pl.ds(pl.multiple_of((c * S_PAIRS + lp) * LANES, LANES), LANES)
                nk = pltpu.roll(n_ref[0, nrows, :], shift, 1)
                nv = pltpu.roll(n_ref[1, nrows, :], shift, 1)
                qs = _stack_halves(qb_ref[g * B_PAIRS + c * S_PAIRS + lp], lo).astype(BF16)
                kt = _roll_cache(c_ref, nk, r_ref, 0, lp * LANES, LANES)
                vt = _roll_cache(c_ref, nv, r_ref, 1, lp * LANES, LANES)
                pending[lp, g] = _sample_probs(qs, kt, nk, B_PATTERNS[g][1]) + (vt, nv)

            def second(lp=lp, g=g):
                o, lse = _sample_values(*pending.pop((lp, g)))
                lse = jnp.broadcast_to(lse, o.shape)
                done[lp, g] = (jnp.where(lo, o[0:T_NEW], o[T_NEW:2 * T_NEW]),
                               jnp.where(lo, lse[0:T_NEW], lse[T_NEW:2 * T_NEW]))
                if g == N_B_GROUPS - 1:
                    outs, lses = zip(*[done.pop((lp, k)) for k in range(N_B_GROUPS)])
                    ob_ref[:, lp * LANES:(lp + 1) * LANES] = _combine3(outs, lses)

            units.append((first, second))

    hrow = lax.broadcasted_iota(jnp.int32, (A_GROUP * T_NEW, 1), 0) // T_NEW
    for kh in range(A_KV_HEADS):
        def first(kh=kh):
            if kh == 0:
                nk = pltpu.roll(na_ref[0], shift, 1)
                nv = pltpu.roll(na_ref[1], shift, 1)
                pending["a"] = (nk, nv, _roll_cache(ca_ref, nk, ra_ref, 0, 0, A_KV_W),
                                _roll_cache(ca_ref, nv, ra_ref, 1, 0, A_KV_W))
            nk, nv, kt_a, vt_a = pending["a"]
            hs = slice(kh * HEAD_DIM, (kh + 1) * HEAD_DIM)
            dup = lambda x: jnp.concatenate([x[hs], x[hs]], axis=0)
            tiles = [_stack_halves(qa_ref[:, (kh * (A_GROUP // 2) + p) * LANES:(kh * (A_GROUP // 2) + p + 1) * LANES], lo)
                     for p in range(A_GROUP // 2)]
            qs = jnp.concatenate(tiles, axis=0).astype(BF16)
            sink = jnp.zeros((A_GROUP * T_NEW, 1), F32)
            for hh in range(A_GROUP):
                sink = jnp.where(hrow == hh, sink_ref[kh * A_GROUP + hh], sink)
            pending["a", kh] = _sample_probs(qs, dup(kt_a), dup(nk), 1, sink) + (dup(vt_a), dup(nv))

        def second(kh=kh):
            o, _ = _sample_values(*pending.pop(("a", kh)))
            for p in range(A_GROUP // 2):
                o0 = o[(2 * p) * T_NEW:(2 * p + 1) * T_NEW, :]
                o1 = o[(2 * p + 1) * T_NEW:(2 * p + 2) * T_NEW, :]
                c0 = (kh * (A_GROUP // 2) + p) * LANES
                oa_ref[:, c0:c0 + LANES] = jnp.where(lo, o0, o1)

        units.append((first, second))
    return units


N_SAMPLE_IN, N_SAMPLE_OUT = 11, 6


def _sample_specs(idx, nt, cache_a, cache_bs):
    rows = S_PAIRS * LANES
    seq = lambda *g: idx(*g)[0]
    chunk = lambda *g: idx(*g)[1]
    lane_blk = lambda *g: idx(*g)[0] // SEQS_PER_LANE_BLOCK
    in_specs = [
        pl.BlockSpec(memory_space=pltpu.SMEM),
        pl.BlockSpec((T_NEW, A_Q_W), lambda *g: (seq(*g), 0)),
        pl.BlockSpec((N_B_GROUPS * B_PAIRS, T_NEW, LANES), lambda *g: (0, seq(*g), 0)),
        pl.BlockSpec((2, A_KV_W, LANES), lambda *g: (0, 0, lane_blk(*g))),
    ]
    in_specs += [pl.BlockSpec((2, B_OUT_W, LANES), lambda *g: (0, 0, lane_blk(*g)))] * N_B_GROUPS
    cache_specs = [pl.BlockSpec((None, 2, A_KV_W, cache_a.shape[-1]), lambda *g: (seq(*g), 0, 0, 0))]
    cache_specs += [pl.BlockSpec((None, 2, rows, cb.shape[-1]), lambda *g: (seq(*g), 0, chunk(*g), 0))
                    for cb in cache_bs]
    out_specs = [pl.BlockSpec((T_NEW, A_Q_W), lambda *g: (seq(*g), 0)),
                 pl.BlockSpec((T_NEW, rows), lambda *g: (seq(*g), chunk(*g)))] + cache_specs
    out_shape = [jax.ShapeDtypeStruct((nt, A_Q_W), F32), jax.ShapeDtypeStruct((nt, B_OUT_W), F32),
                 jax.ShapeDtypeStruct(cache_a.shape, F32)] + [jax.ShapeDtypeStruct(cb.shape, F32) for cb in cache_bs]
    return in_specs + cache_specs, out_specs, out_shape


def _layer_norm(z, g, b):
    mu = jnp.mean(z, axis=-1, keepdims=True)
    zc = z - mu
    var = jnp.mean(zc * zc, axis=-1, keepdims=True)
    return zc * lax.rsqrt(var + LN_EPS) * g + b


def _mix_kernel(x_ref, oa_ref, *refs):
    ob_refs, (g_ref, wa_ref, wb_ref, wo_ref, lng_ref, lnb_ref, h_ref) = refs[:-7], refs[-7:]
    if len(ob_refs) == 1:
        ob = ob_refs[0][...].astype(BF16)
    else:
        o_refs, l_refs = ob_refs[:N_B_GROUPS], ob_refs[N_B_GROUPS:]
        ob = jnp.concatenate([_combine3([r[p] for r in o_refs], [r[p] for r in l_refs]).astype(BF16)
                              for p in range(B_PAIRS)], axis=1)
    br_a = jnp.dot(oa_ref[...].astype(BF16), wa_ref[...], preferred_element_type=F32)
    br_b = jnp.dot(ob, wb_ref[...], preferred_element_type=F32)
    ga = jax.nn.sigmoid(g_ref[:, 0:D_MODEL].astype(F32))
    gb = jax.nn.sigmoid(g_ref[:, D_MODEL:2 * D_MODEL].astype(F32))
    merged = (ga * br_a + gb * br_b).astype(BF16)
    mix = jnp.dot(merged, wo_ref[...], preferred_element_type=F32)
    h_ref[...] = _layer_norm(ALPHA * x_ref[...] + mix, lng_ref[...], lnb_ref[...])


def _mix(x2d, oa, ob, gates, wa, wb, wo, ln_g, ln_b, tm):
    m = x2d.shape[0]
    row = lambda w: pl.BlockSpec((tm, w), lambda i: (i, 0))
    const = lambda a: pl.BlockSpec(a.shape, lambda i: (0, 0), pipeline_mode=pl.Buffered(1))
    if isinstance(ob, tuple):
        ob_args = [*ob[0], *ob[1]]
        ob_specs = [pl.BlockSpec((B_PAIRS, tm, LANES), lambda i: (0, i, 0))] * len(ob_args)
    else:
        ob_args, ob_specs = [ob], [row(B_OUT_W)]
    return pl.pallas_call(
        _mix_kernel, grid=(m // tm,),
        in_specs=[row(D_MODEL), row(A_Q_W), *ob_specs, row(2 * D_MODEL),
                  const(wa), const(wb), const(wo), const(ln_g), const(ln_b)],
        out_specs=row(D_MODEL), out_shape=jax.ShapeDtypeStruct((m, D_MODEL), F32),
        compiler_params=_cparams(("parallel",)), name="mix",
    )(x2d, oa, *ob_args, gates, wa, wb, wo, ln_g, ln_b)


FFN_OUT_CHUNK = 512
LN_ROWS = 128


N_FFN_IN = 7


def _ffn_kernel(*refs, with_sample, row_split):
    h_ref, w1_ref, b1_ref, w2_ref, b2_ref, lng_ref, lnb_ref = refs[:N_FFN_IN]
    if with_sample:
        sample_in = refs[N_FFN_IN:N_FFN_IN + N_SAMPLE_IN]
        y_ref = refs[N_FFN_IN + N_SAMPLE_IN]
        sample_out = refs[N_FFN_IN + N_SAMPLE_IN + 1:N_FFN_IN + N_SAMPLE_IN + 1 + N_SAMPLE_OUT]
    else:
        y_ref = refs[N_FFN_IN]
    hb_ref = refs[-1]
    f, r = pl.program_id(1), pl.program_id(2)
    n_f, n_r = pl.num_programs(1), pl.num_programs(2)
    sub = h_ref.shape[0] // row_split
    tf = w1_ref.shape[1]
    rows = pl.ds(pl.multiple_of(r * sub, sub), sub)

    @pl.when((f == 0) & (r == 0))
    def _():
        hb_ref[...] = h_ref[...].astype(BF16)
        y_ref[...] = jnp.zeros_like(y_ref)

    parts = {}

    def up(k):
        cs = slice(k * (tf // 2), (k + 1) * (tf // 2))
        t = jnp.dot(hb_ref[rows, :], w1_ref[:, cs], preferred_element_type=F32) + b1_ref[:, cs]
        parts[k] = jnp.square(jnp.maximum(t, 0.0)).astype(BF16)

    def down(k):
        if "u" not in parts:
            parts["u"] = jnp.concatenate([parts.pop(0), parts.pop(1)], axis=1)
        cs = slice(k * FFN_OUT_CHUNK, (k + 1) * FFN_OUT_CHUNK)
        y_ref[rows, cs] += jnp.dot(parts["u"], w2_ref[:, cs], preferred_element_type=F32)

    pieces = [functools.partial(up, 0), functools.partial(up, 1)]
    pieces += [functools.partial(down, k) for k in range(D_MODEL // FFN_OUT_CHUNK)]
    if with_sample:
        step = (pl.program_id(0) * n_f + f) * n_r + r
        units = _sample_units(step // SAMPLE_CHUNKS, step % SAMPLE_CHUNKS, *sample_in, *sample_out)
        units[0][0]()
        for k in range(max(len(units), len(pieces))):
            if k + 1 < len(units):
                units[k + 1][0]()
            if k < len(pieces):
                pieces[k]()
            if k < len(units):
                units[k][1]()
    else:
        for piece in pieces:
            piece()

    @pl.when((f == n_f - 1) & (r == n_r - 1))
    def _():
        def ln_rows(k, carry):
            rs = pl.ds(pl.multiple_of(k * LN_ROWS, LN_ROWS), LN_ROWS)
            z = ALPHA * h_ref[rs, :] + y_ref[rs, :] + b2_ref[...]
            y_ref[rs, :] = _layer_norm(z, lng_ref[...], lnb_ref[...])
            return carry
        lax.fori_loop(0, h_ref.shape[0] // LN_ROWS, ln_rows, 0)


def _ffn(h, w1, b1, w2, b2, ln_g, ln_b, tm, tf, row_split, sample_args=None):
    m = h.shape[0]
    grid = (m // tm, D_FF // tf, row_split)
    const = lambda a: pl.BlockSpec(a.shape, lambda i, f, r: (0, 0))
    once = pl.Buffered(1) if row_split > 1 else None
    in_specs = [pl.BlockSpec((tm, D_MODEL), lambda i, f, r: (i, 0), pipeline_mode=once),
                pl.BlockSpec((D_MODEL, tf), lambda i, f, r: (0, f)),
                pl.BlockSpec((1, tf), lambda i, f, r: (0, f)),
                pl.BlockSpec((tf, D_MODEL), lambda i, f, r: (f, 0)),
                const(b2), const(ln_g), const(ln_b)]
    out_specs = [pl.BlockSpec((tm, D_MODEL), lambda i, f, r: (i, 0), pipeline_mode=once)]
    out_shape = [jax.ShapeDtypeStruct((m, D_MODEL), F32)]
    args = [h, w1, b1, w2, b2, ln_g, ln_b]
    if sample_args is not None:
        sinks, qa, qb, new_a, new_bs, cache_a, cache_bs = sample_args
        nt = qa.shape[0]
        assert grid[0] * grid[1] * grid[2] == (nt // T_NEW) * SAMPLE_CHUNKS

        def idx(i, f, r):
            step = (i * grid[1] + f) * grid[2] + r
            return step // SAMPLE_CHUNKS, step % SAMPLE_CHUNKS

        s_in, s_out, s_shape = _sample_specs(idx, nt, cache_a, cache_bs)
        in_specs += s_in
        out_specs += s_out
        out_shape += s_shape
        args += [sinks, qa, qb, new_a, *new_bs, cache_a, *cache_bs]
    out = pl.pallas_call(
        functools.partial(_ffn_kernel, with_sample=sample_args is not None, row_split=row_split),
        grid=grid, in_specs=in_specs, out_specs=out_specs, out_shape=out_shape,
        scratch_shapes=[pltpu.VMEM((tm, D_MODEL), BF16)],
        compiler_params=_cparams(("arbitrary", "arbitrary", "arbitrary")),
        name="ffn_sample" if sample_args is not None else "ffn",
    )(*args)
    return out if sample_args is not None else out[0]


def _to_cache_layout(kvt, heads):
    n, _, _, rows = kvt.shape
    return jnp.transpose(kvt.reshape(n, 2, heads, HEAD_DIM, rows), (0, 4, 1, 2, 3))[None]


def _from_cache_layout(cache):
    _, n, rows, _, heads, _ = cache.shape
    return jnp.transpose(cache[0], (0, 2, 3, 4, 1)).reshape(n, 2, heads * HEAD_DIM, rows)


def kernel(x_prompt, x_sample, cache_a_kv, cache_b1_kv, cache_b2_kv, cache_b3_kv, w_in, a_sinks,
           w_branch_a, w_branch_b, w_out, ln1_g, ln1_b, w_ff1, b_ff1, w_ff2, b_ff2, ln2_g, ln2_b):
    n_p, seq, _ = x_prompt.shape
    n_s, t_new, _ = x_sample.shape
    assert t_new == T_NEW and w_in.shape[0] == 1
    b_caches = (cache_b1_kv, cache_b2_kv, cache_b3_kv)

    w_cat, w_g = _prep_w_in(w_in[0])
    wb = w_branch_b[0].astype(BF16)
    sinks = a_sinks[0].reshape(A_Q_HEADS).astype(F32)
    row = lambda v: v[0].reshape(1, -1)

    def finish(x2d, oa, ob, gates, sample_args=None):
        h = _mix(x2d, oa, ob, gates, wa, wb, wo, row(ln1_g), row(ln1_b), tm=512)
        tm, row_split = (1024, 2) if sample_args is not None else (1024, 1)
        return _ffn(h, w1, row(b_ff1), w2, row(b_ff2), row(ln2_g), row(ln2_b), tm, 512, row_split, sample_args)

    xp = x_prompt.reshape(n_p * seq, D_MODEL)
    cos_p, sin_p = _rope_tables(np.arange(seq))
    windows = (min(A_WINDOW, seq),) + tuple(min(win, seq) for win, _ in B_PATTERNS)
    (qa, kva, kvt_a, qb, kvb0, kvt0, kvb1, kvt1, kvb2, kvt2) = _proj(
        xp, w_cat, cos_p, sin_p, BF16, 1024, n_p, seq, windows)
    gates, w1, w2, wa, wo = _gates(xp, w_g, 1024, cast=(w_ff1[0], w_ff2[0], w_branch_a[0], w_out[0]))
    kvbs, kvts = (kvb0, kvb1, kvb2), (kvt0, kvt1, kvt2)
    nt = n_s * T_NEW
    xs = x_sample.reshape(nt, D_MODEL)
    cos_s, sin_s = _rope_tables(PAST_LEN + np.arange(nt) % T_NEW)
    (qa_s, _, kvt_a_s, qb_s, _, kvt0_s, _, kvt1_s, _, kvt2_s) = _proj(
        xs, w_cat, cos_s, sin_s, F32, 512, 1, nt, (nt,) * (1 + N_B_GROUPS))
    gates_s = _gates(xs, w_g, tm=nt)

    o_a = _attn_a_prompt(sinks, qa, kva, n_p, seq)
    outs, lses = [], []
    for g, (win, dil) in enumerate(B_PATTERNS):
        o, lse = _attn_b_prompt(qb, kvbs[g], g, dil, n_p, seq)
        outs.append(o)
        lses.append(lse)
    o_b = (tuple(outs), tuple(lses))
    sample_args = (sinks, qa_s, qb_s, kvt_a_s[0], [kvt0_s[0], kvt1_s[0], kvt2_s[0]],
                   _from_cache_layout(cache_a_kv), [_from_cache_layout(cb) for cb in b_caches])
    y_p, o_a_s, o_b_s, r_a, r_b1, r_b2, r_b3 = finish(xp, o_a, o_b, gates, sample_args)
    y_p = y_p.reshape(n_p, seq, D_MODEL)
    a_kv_p = _to_cache_layout(kvt_a, A_KV_HEADS)
    b_kv_p = [_to_cache_layout(kvts[g], B_HEADS) for g in range(N_B_GROUPS)]

    y_s = finish(xs, o_a_s, o_b_s, gates_s).reshape(n_s, T_NEW, D_MODEL)
    a_kv_s = _to_cache_layout(r_a, A_KV_HEADS)
    b_kv_s = [_to_cache_layout(r, B_HEADS) for r in (r_b1, r_b2, r_b3)]

    return (y_p, y_s, a_kv_p, a_kv_s,
            b_kv_p[0], b_kv_s[0], b_kv_p[1], b_kv_s[1], b_kv_p[2], b_kv_s[2])
```
